```python
import jax, jax.numpy as jnp
from jax import lax
import numpy as np

D_MODEL = 1024
BATCH = 1
SEQ = 16384
DEPTH = 1
DEC_BATCH = 4
DEC_SEQ = 4096
PAST_LEN = 128

GRID_W = 64
N_HEADS = 8
N_KV_HEADS = 2
GROUP = N_HEADS // N_KV_HEADS
HEAD_DIM = D_MODEL // N_HEADS
ATTN_Q = N_HEADS * HEAD_DIM
ATTN_KV = N_KV_HEADS * HEAD_DIM
AXIS_DIM = HEAD_DIM // 2
N_FREQ = AXIS_DIM // 2
ROPE_THETA = 10000.0
Q_BLOCK = 128
LRU_WIDTH = D_MODEL
LRU_BLOCKS = 8
LRU_BLOCK_W = LRU_WIDTH // LRU_BLOCKS
CONV_W = 4
CONV_PAD_LEFT = 2
LRU_C = 8.0
N_GROUPS = 4
EXPERTS_PER_GROUP = 8
N_EXPERTS = N_GROUPS * EXPERTS_PER_GROUP
TOP_K = 2
D_EXPERT = D_MODEL // 2
MOE_BLOCK = 128
D_IN = 2 * LRU_WIDTH + ATTN_Q + 2 * ATTN_KV + 2 * D_MODEL
SPLITS = [LRU_WIDTH, 2 * LRU_WIDTH, 2 * LRU_WIDTH + ATTN_Q, 2 * LRU_WIDTH + ATTN_Q + ATTN_KV, 2 * LRU_WIDTH + ATTN_Q + 2 * ATTN_KV]
EPS = 1e-6
F32 = jnp.float32

kernel_name = 'hybrid_rglru_axial_gqa_hiermoe_encoder'


def rmsnorm(x, gain):
    xf = x.astype(F32)
    y = xf * lax.rsqrt(jnp.mean(xf * xf, axis=-1, keepdims=True) + EPS)
    return (y * gain.astype(F32)).astype(x.dtype)


def axial_rope_tables(seq_len):
    rows = seq_len // GRID_W
    row = jnp.repeat(jnp.arange(rows, dtype=F32), GRID_W)
    col = (jnp.arange(seq_len) % GRID_W).astype(F32)
    inv = ROPE_THETA ** (-jnp.arange(0, AXIS_DIM, 2, dtype=F32) / AXIS_DIM)
    ang = jnp.concatenate([row[:, None] * inv, col[:, None] * inv], axis=-1)
    return jnp.cos(ang), jnp.sin(ang)


def apply_axial_rope(x, cos, sin):
    bshape = (1, cos.shape[0]) + (1,) * (x.ndim - 3) + (N_FREQ,)
    xf = x.astype(F32)

    def rot(xs, c, s):
        x1, x2 = xs[..., :N_FREQ], xs[..., N_FREQ:]
        return jnp.concatenate([x1 * c - x2 * s, x2 * c + x1 * s], axis=-1)

    out_row = rot(xf[..., :AXIS_DIM], cos[:, :N_FREQ].reshape(bshape), sin[:, :N_FREQ].reshape(bshape))
    out_col = rot(xf[..., AXIS_DIM:], cos[:, N_FREQ:].reshape(bshape), sin[:, N_FREQ:].reshape(bshape))
    return jnp.concatenate([out_row, out_col], axis=-1).astype(x.dtype)


def blocked_gqa(q, k, v):
    b, s = q.shape[0], q.shape[1]
    nb = s // Q_BLOCK
    qb = q.reshape(b, nb, Q_BLOCK, N_KV_HEADS, GROUP, HEAD_DIM).transpose(1, 0, 2, 3, 4, 5)
    kf = k.astype(F32)
    vf = v.astype(F32)
    scale = HEAD_DIM ** -0.5

    def one_block(q_blk):
        sc = jnp.einsum('bqkgd,bskd->bkgqs', q_blk.astype(F32), kf) * scale
        p = jax.nn.softmax(sc, axis=-1)
        return jnp.einsum('bkgqs,bskd->bqkgd', p, vf).astype(q.dtype)

    ob = lax.map(one_block, qb)
    return ob.transpose(1, 0, 2, 3, 4, 5).reshape(b, s, ATTN_Q)


def centred_depthwise_conv(x, w, bias):
    out = lax.conv_general_dilated(
        x, w[:, None, :], window_strides=(1,),
        padding=[(CONV_PAD_LEFT, CONV_W - 1 - CONV_PAD_LEFT)],
        dimension_numbers=('NWC', 'WIO', 'NWC'), feature_group_count=LRU_WIDTH)
    return out + bias


def _lru_combine(left, right):
    a1, b1 = left
    a2, b2 = right
    return a1 * a2, a2 * b1 + b2


def rglru_scan(xc, w_a, b_a, w_x, b_x, lam, reverse):
    b, s = xc.shape[0], xc.shape[1]
    xb = xc.reshape(b, s, LRU_BLOCKS, LRU_BLOCK_W)
    r = jax.nn.sigmoid(jnp.einsum('bshi,hij->bshj', xb, w_a) + b_a).reshape(b, s, LRU_WIDTH)
    i = jax.nn.sigmoid(jnp.einsum('bshi,hij->bshj', xb, w_x) + b_x).reshape(b, s, LRU_WIDTH)
    log_a = LRU_C * r.astype(F32) * jax.nn.log_sigmoid(lam.astype(F32))
    a = jnp.exp(log_a)
    u = jnp.sqrt(-jnp.expm1(2.0 * log_a)) * (i * xc).astype(F32)
    _, h = lax.associative_scan(_lru_combine, (a, u), reverse=reverse, axis=1)
    return h


def grouped_experts(h, eid, wts, w_gate, w_up, w_down):
    n_tok = h.shape[0]
    n_asg = n_tok * TOP_K
    n_rows = -(-(n_asg + N_EXPERTS * MOE_BLOCK) // MOE_BLOCK) * MOE_BLOCK
    n_blk = n_rows // MOE_BLOCK
    flat_e = eid.reshape(n_asg)
    flat_t = jnp.repeat(jnp.arange(n_tok, dtype=jnp.int32), TOP_K)
    flat_w = wts.reshape(n_asg)
    order = jnp.argsort(flat_e)
    se, st, sw = flat_e[order], flat_t[order], flat_w[order]
    counts = jnp.bincount(flat_e, length=N_EXPERTS)
    start = jnp.cumsum(counts) - counts
    padded = (counts + MOE_BLOCK - 1) // MOE_BLOCK * MOE_BLOCK
    pend = jnp.cumsum(padded)
    pstart = pend - padded
    dest = pstart[se] + jnp.arange(n_asg) - start[se]
    row_tok = jnp.full((n_rows,), n_tok, jnp.int32).at[dest].set(st)
    row_w = jnp.zeros((n_rows,), F32).at[dest].set(sw)
    blk_e = jnp.minimum(jnp.searchsorted(pend, jnp.arange(n_blk) * MOE_BLOCK, side='right'), N_EXPERTS - 1)
    h_pad = jnp.concatenate([h, jnp.zeros((1, D_MODEL), h.dtype)], axis=0)
    xb = h_pad[row_tok].reshape(n_blk, MOE_BLOCK, D_MODEL)

    def expert_block(args):
        x_blk, e = args
        act = jax.nn.silu(x_blk @ w_gate[e]) * (x_blk @ w_up[e])
        return act @ w_down[e]

    yb = lax.map(expert_block, (xb, blk_e)).reshape(n_rows, D_MODEL)
    y = jax.ops.segment_sum(yb.astype(F32) * row_w[:, None], row_tok, num_segments=n_tok + 1)
    return y[:n_tok].astype(h.dtype)


def hierarchical_moe(h, w_rg, b_rg, w_re, b_re, w_gate, w_up, w_down):
    n_tok = h.shape[0]
    lg = (h @ w_rg).astype(F32) + b_rg.astype(F32)
    p_grp = jax.nn.softmax(lg, axis=-1)
    grp = jnp.argmax(lg, axis=-1).astype(jnp.int32)
    p_sel = jnp.take_along_axis(p_grp, grp[:, None], axis=-1)
    le = ((h @ w_re).astype(F32) + b_re.astype(F32)).reshape(n_tok, N_GROUPS, EXPERTS_PER_GROUP)
    le_sel = jnp.take_along_axis(le, grp[:, None, None], axis=1)[:, 0]
    top_p, top_i = lax.top_k(jax.nn.softmax(le_sel, axis=-1), TOP_K)
    wts = p_sel * top_p / jnp.sum(top_p, axis=-1, keepdims=True)
    eid = grp[:, None] * EXPERTS_PER_GROUP + top_i.astype(jnp.int32)
    return grouped_experts(h, eid, wts, w_gate, w_up, w_down)


def encoder_layer(x, g_mix, w_in, b_gate, conv_w, conv_b, lru_wa, lru_ba, lru_wx, lru_bx, lru_lam,
                  q_gain, k_gain, w_out, g_ffn, w_rg, b_rg, w_re, b_re, w_gate, w_up, w_down):
    b, s, _ = x.shape
    h = rmsnorm(x, g_mix)
    z = jnp.einsum('bsd,de->bse', h, w_in)
    u, gr, q, k, v, gates = jnp.split(z, SPLITS, axis=-1)
    xc = centred_depthwise_conv(u, conv_w, conv_b)
    y_lru = (rglru_scan(xc, lru_wa[0], lru_ba[0], lru_wx[0], lru_bx[0], lru_lam[0], False)
             + rglru_scan(xc, lru_wa[1], lru_ba[1], lru_wx[1], lru_bx[1], lru_lam[1], True))
    y_a = jax.nn.gelu(gr) * y_lru.astype(x.dtype)
    q = rmsnorm(q.reshape(b, s, N_KV_HEADS, GROUP, HEAD_DIM), q_gain)
    k = rmsnorm(k.reshape(b, s, N_KV_HEADS, HEAD_DIM), k_gain)
    v = v.reshape(b, s, N_KV_HEADS, HEAD_DIM)
    cos, sin = axial_rope_tables(s)
    y_b = blocked_gqa(apply_axial_rope(q, cos, sin), apply_axial_rope(k, cos, sin), v)
    g = jax.nn.sigmoid(gates + b_gate)
    merged = g[..., :D_MODEL] * y_a + g[..., D_MODEL:] * y_b
    x = x + jnp.einsum('bse,ed->bsd', merged, w_out)
    y_moe = hierarchical_moe(rmsnorm(x, g_ffn).reshape(b * s, D_MODEL), w_rg, b_rg, w_re, b_re, w_gate, w_up, w_down)
    return x + y_moe.reshape(b, s, D_MODEL)


def setup_inputs(seed: int = 0) -> dict:
    key = jax.random.key(seed)
    ks = jax.random.split(key, 23)

    def nrm(k, shape, scale):
        return scale * jax.random.normal(k, shape, F32)

    u = jax.random.uniform(ks[11], (DEPTH, 2, LRU_WIDTH), F32, minval=0.9, maxval=0.999)
    a0 = u ** (1.0 / LRU_C)
    return {
        'x_prompt': nrm(ks[0], (BATCH, SEQ, D_MODEL), 1.0),
        'x_sample': nrm(ks[1], (DEC_BATCH, DEC_SEQ, D_MODEL), 1.0),
        'g_mix': 1.0 + nrm(ks[2], (DEPTH, D_MODEL), 0.01),
        'w_in': nrm(ks[3], (DEPTH, D_MODEL, D_IN), D_MODEL ** -0.5),
        'b_gate': nrm(ks[4], (DEPTH, 2 * D_MODEL), 0.1),
        'conv_w': nrm(ks[5], (DEPTH, CONV_W, LRU_WIDTH), CONV_W ** -0.5),
        'conv_b': nrm(ks[6], (DEPTH, LRU_WIDTH), 0.01),
        'lru_wa': nrm(ks[7], (DEPTH, 2, LRU_BLOCKS, LRU_BLOCK_W, LRU_BLOCK_W), LRU_BLOCK_W ** -0.5),
        'lru_ba': nrm(ks[8], (DEPTH, 2, LRU_BLOCKS, LRU_BLOCK_W), 0.1),
        'lru_wx': nrm(ks[9], (DEPTH, 2, LRU_BLOCKS, LRU_BLOCK_W, LRU_BLOCK_W), LRU_BLOCK_W ** -0.5),
        'lru_bx': nrm(ks[10], (DEPTH, 2, LRU_BLOCKS, LRU_BLOCK_W), 0.1),
        'lru_lam': jnp.log(a0) - jnp.log1p(-a0),
        'q_gain': 1.0 + nrm(ks[12], (DEPTH, HEAD_DIM), 0.01),
        'k_gain': 1.0 + nrm(ks[13], (DEPTH, HEAD_DIM), 0.01),
        'w_out': nrm(ks[14], (DEPTH, D_MODEL, D_MODEL), D_MODEL ** -0.5),
        'g_ffn': 1.0 + nrm(ks[15], (DEPTH, D_MODEL), 0.01),
        'w_rg': nrm(ks[16], (DEPTH, D_MODEL, N_GROUPS), D_MODEL ** -0.5),
        'b_rg': nrm(ks[17], (DEPTH, N_GROUPS), 0.01),
        'w_re': nrm(ks[18], (DEPTH, D_MODEL, N_EXPERTS), D_MODEL ** -0.5),
        'b_re': nrm(ks[19], (DEPTH, N_EXPERTS), 0.01),
        'w_gate': nrm(ks[20], (DEPTH, N_EXPERTS, D_MODEL, D_EXPERT), D_MODEL ** -0.5),
        'w_up': nrm(ks[21], (DEPTH, N_EXPERTS, D_MODEL, D_EXPERT), D_MODEL ** -0.5),
        'w_down': nrm(ks[22], (DEPTH, N_EXPERTS, D_EXPERT, D_MODEL), D_EXPERT ** -0.5),
    }


def reference(x_prompt, x_sample, g_mix, w_in, b_gate, conv_w, conv_b, lru_wa, lru_ba, lru_wx, lru_bx, lru_lam,
              q_gain, k_gain, w_out, g_ffn, w_rg, b_rg, w_re, b_re, w_gate, w_up, w_down):
    def trunk(x):
        for l in range(DEPTH):
            x = encoder_layer(x, g_mix[l], w_in[l], b_gate[l], conv_w[l], conv_b[l], lru_wa[l], lru_ba[l],
                              lru_wx[l], lru_bx[l], lru_lam[l], q_gain[l], k_gain[l], w_out[l], g_ffn[l],
                              w_rg[l], b_rg[l], w_re[l], b_re[l], w_gate[l], w_up[l], w_down[l])
        return x

    y_prompt = trunk(x_prompt)
    y_sample = trunk(x_sample)
    return (y_prompt, y_sample)
```

```python
import functools
import math

import jax
import jax.numpy as jnp
from jax import lax
from jax.experimental import pallas as pl
from jax.experimental.pallas import tpu as pltpu

F32 = jnp.float32
BF16 = jnp.bfloat16

D_MODEL = 1024
N_HEADS = 8
N_KV_HEADS = 2
GROUP = N_HEADS // N_KV_HEADS
HEAD_DIM = D_MODEL // N_HEADS
AXIS_DIM = HEAD_DIM // 2
N_FREQ = AXIS_DIM // 2
GRID_W = 64
ROPE_THETA = 10000.0
LRU_WIDTH = D_MODEL
LRU_BLOCKS = 8
LRU_BLOCK_W = LRU_WIDTH // LRU_BLOCKS
CONV_W = 4
CONV_PAD_LEFT = 2
LRU_C = 8.0
N_GROUPS = 4
EXPERTS_PER_GROUP = 8
N_EXPERTS = N_GROUPS * EXPERTS_PER_GROUP
TOP_K = 2
D_EXPERT = D_MODEL // 2
ATTN_KV = N_KV_HEADS * HEAD_DIM
EPS = 1e-6

LANES = 128
SUBLANES = 8
VMEM_LIMIT_BYTES = 56 * 1024 * 1024

ROW_TILE = 256
KV_CHUNK = 512
MOE_ROWS = 256
ROUTE_LANES = LANES
EXPERT_LANE0 = N_GROUPS

NAT_COLS = 2 * LRU_WIDTH + ATTN_KV + 2 * D_MODEL
TR_ROWS = D_MODEL + ATTN_KV
LOG2E = math.log2(math.e)


def _cparams(semantics):
    return pltpu.CompilerParams(dimension_semantics=semantics, vmem_limit_bytes=VMEM_LIMIT_BYTES)


def _const_spec(shape):
    nd = len(shape)
    return pl.BlockSpec(shape, lambda *_: (0,) * nd)


def _swap_halves_rows(x):
    return jnp.concatenate([x[N_FREQ:AXIS_DIM], x[0:N_FREQ], x[AXIS_DIM + N_FREQ:], x[AXIS_DIM:AXIS_DIM + N_FREQ]], axis=0)


def _inproj_kernel(x_ref, gmix_ref, wnat_ref, wtr_ref, bgate_ref, kgain_ref, qgain_ref,
                   cos_ref, sin_ref, cost_ref, sint_ref,
                   u_ref, ggr_ref, k_ref, gate_ref, qt_ref, vt_ref):
    tm = x_ref.shape[0]
    x = x_ref[...]
    h = (x * lax.rsqrt(jnp.mean(x * x, axis=-1, keepdims=True) + EPS) * gmix_ref[...]).astype(BF16)

    def nat(lo, hi):
        return jnp.dot(h, wnat_ref[:, lo:hi], preferred_element_type=F32)

    u_ref[...] = nat(0, LRU_WIDTH)
    ggr_ref[...] = jax.nn.gelu(nat(LRU_WIDTH, 2 * LRU_WIDTH))
    k0 = 2 * LRU_WIDTH
    kraw = nat(k0, k0 + ATTN_KV)
    g0 = k0 + ATTN_KV
    gate_ref[...] = jax.nn.sigmoid(nat(g0, g0 + 2 * D_MODEL) + bgate_ref[...])

    cos = cos_ref[...]
    sin = sin_ref[...]
    lane = lax.broadcasted_iota(jnp.int32, (tm, HEAD_DIM), 1)
    first_half = (lane % AXIS_DIM) < N_FREQ
    for j in range(N_KV_HEADS):
        kj = kraw[:, j * HEAD_DIM:(j + 1) * HEAD_DIM]
        kn = kj * lax.rsqrt(jnp.mean(kj * kj, axis=-1, keepdims=True) + EPS) * kgain_ref[...]
        partner = jnp.where(first_half, pltpu.roll(kn, HEAD_DIM - N_FREQ, 1), pltpu.roll(kn, N_FREQ, 1))
        k_ref[:, j * HEAD_DIM:(j + 1) * HEAD_DIM] = (kn * cos + partner * sin).astype(BF16)

    zt = lax.dot_general(wtr_ref[...], h, (((1,), (1,)), ((), ())), preferred_element_type=F32)
    cost = cost_ref[...]
    sint = sint_ref[...]
    qgain = qgain_ref[...]
    qscale = (HEAD_DIM ** -0.5) * LOG2E
    for hd in range(N_HEADS):
        xq = zt[hd * HEAD_DIM:(hd + 1) * HEAD_DIM, :]
        xn = xq * lax.rsqrt(jnp.mean(xq * xq, axis=0, keepdims=True) + EPS) * qgain
        rot = (xn * cost + _swap_halves_rows(xn) * sint) * qscale
        g = hd % GROUP
        qt_ref[0, hd // GROUP, :, g * tm:(g + 1) * tm] = rot.astype(BF16)
    for j in range(N_KV_HEADS):
        r0 = D_MODEL + j * HEAD_DIM
        vt_ref[j] = zt[r0:r0 + HEAD_DIM, :].astype(BF16)


def _inproj(x2, seq_len, p, tm):
    t = x2.shape[0]
    n_pos = seq_len // tm
    grid = (t // tm,)
    row = lambda i: (i, 0)
    in_specs = [
        pl.BlockSpec((tm, D_MODEL), row),
        _const_spec((1, D_MODEL)),
        _const_spec((D_MODEL, NAT_COLS)),
        _const_spec((TR_ROWS, D_MODEL)),
        _const_spec((1, 2 * D_MODEL)),
        _const_spec((1, HEAD_DIM)),
        _const_spec((HEAD_DIM, tm)),
        pl.BlockSpec((tm, HEAD_DIM), lambda i: (i % n_pos, 0)),
        pl.BlockSpec((tm, HEAD_DIM), lambda i: (i % n_pos, 0)),
        pl.BlockSpec((HEAD_DIM, tm), lambda i: (0, i % n_pos)),
        pl.BlockSpec((HEAD_DIM, tm), lambda i: (0, i % n_pos)),
    ]
    out_shape = (
        jax.ShapeDtypeStruct((t, LRU_WIDTH), F32),
        jax.ShapeDtypeStruct((t, LRU_WIDTH), F32),
        jax.ShapeDtypeStruct((t, ATTN_KV), BF16),
        jax.ShapeDtypeStruct((t, 2 * D_MODEL), F32),
        jax.ShapeDtypeStruct((t // tm, N_KV_HEADS, HEAD_DIM, GROUP * tm), BF16),
        jax.ShapeDtypeStruct((N_KV_HEADS, HEAD_DIM, t), BF16),
    )
    out_specs = (
        pl.BlockSpec((tm, LRU_WIDTH), row),
        pl.BlockSpec((tm, LRU_WIDTH), row),
        pl.BlockSpec((tm, ATTN_KV), row),
        pl.BlockSpec((tm, 2 * D_MODEL), row),
        pl.BlockSpec((1, N_KV_HEADS, HEAD_DIM, GROUP * tm), lambda i: (i, 0, 0, 0)),
        pl.BlockSpec((N_KV_HEADS, HEAD_DIM, tm), lambda i: (0, 0, i)),
    )
    return pl.pallas_call(
        _inproj_kernel, grid=grid, in_specs=in_specs, out_specs=out_specs, out_shape=out_shape,
        compiler_params=_cparams(("arbitrary",)), name="inproj",
    )(x2, p["g_mix"], p["w_nat"], p["w_tr"], p["b_gate"], p["k_gain"], p["q_gain_t"],
      p["cos"], p["sin"], p["cos_t"], p["sin_t"])


def _lru_kernel(*refs, reverse, tm, nt):
    if reverse:
        (up_ref, uc_ref, un_ref, cw_ref, cb_ref, w_ref, ba_ref, bx_ref, lam_ref, hf_ref, ggr_ref, ga_ref,
         out_ref, ubuf, a_s, g_s, carry) = refs
    else:
        (up_ref, uc_ref, un_ref, cw_ref, cb_ref, w_ref, ba_ref, bx_ref, lam_ref,
         out_ref, ubuf, a_s, g_s, carry) = refs
    i = pl.program_id(1)
    ti = (nt - 1 - i) if reverse else i

    @pl.when(i == 0)
    def _():
        carry[...] = jnp.zeros_like(carry)

    ubuf[0:SUBLANES] = jnp.where(ti == 0, 0.0, up_ref[...])
    ubuf[SUBLANES:SUBLANES + tm] = uc_ref[...]
    ubuf[SUBLANES + tm:2 * SUBLANES + tm] = jnp.where(ti == nt - 1, 0.0, un_ref[...])
    cw = cw_ref[...]
    base = SUBLANES - CONV_PAD_LEFT
    xc = cb_ref[...] + sum(cw[j:j + 1] * ubuf[base + j:base + j + tm] for j in range(CONV_W))

    xcb = xc.astype(BF16)
    lam = lam_ref[...]
    c_log = LRU_C * (jnp.minimum(lam, 0.0) - jnp.log1p(jnp.exp(-jnp.abs(lam))))
    for hb in range(LRU_BLOCKS):
        sl = slice(hb * LRU_BLOCK_W, (hb + 1) * LRU_BLOCK_W)
        gz = jnp.dot(xcb[:, sl], w_ref[hb], preferred_element_type=F32)
        r = jax.nn.sigmoid(gz[:, :LRU_BLOCK_W] + ba_ref[:, sl])
        ig = jax.nn.sigmoid(gz[:, LRU_BLOCK_W:] + bx_ref[:, sl])
        log_a = r * c_log[:, sl]
        a = jnp.exp(log_a)
        a_s[:, sl] = a
        g_s[:, sl] = jnp.sqrt(-jnp.tanh(log_a) * (1.0 + a * a)) * (ig * xc[:, sl])

    n_chunk = tm // SUBLANES
    srow = lax.broadcasted_iota(jnp.int32, (SUBLANES, LRU_WIDTH), 0)

    def chunk(c, h_prev):
        ci = (n_chunk - 1 - c) if reverse else c
        off = pl.multiple_of(ci * SUBLANES, SUBLANES)
        a = a_s[pl.ds(off, SUBLANES), :]
        b = g_s[pl.ds(off, SUBLANES), :]
        for d in (1, 2, 4):
            shift = (SUBLANES - d) if reverse else d
            valid = (srow < SUBLANES - d) if reverse else (srow >= d)
            a_sh = jnp.where(valid, pltpu.roll(a, shift, 0), 1.0)
            b_sh = jnp.where(valid, pltpu.roll(b, shift, 0), 0.0)
            b = a * b_sh + b
            a = a * a_sh
        h = a * h_prev + b
        if reverse:
            rows = pl.ds(off, SUBLANES)
            out_ref[rows, :] = ga_ref[rows, :] * (ggr_ref[rows, :] * (hf_ref[rows, :] + h))
            return h[0:1]
        out_ref[pl.ds(off, SUBLANES), :] = h
        return h[SUBLANES - 1:SUBLANES]

    carry[...] = lax.fori_loop(0, n_chunk, chunk, carry[...])


def _lru_scan(u, gate, ggr, h_fwd, p, d, batch, seq_len, tm, reverse):
    t = u.shape[0]
    nt = seq_len // tm
    per8 = tm // SUBLANES
    n8 = t // SUBLANES

    def tile(b, i):
        return b * nt + ((nt - 1 - i) if reverse else i)

    cur = lambda b, i: (tile(b, i), 0)
    prev = lambda b, i: (jnp.maximum(tile(b, i) * per8 - 1, 0), 0)
    nxt = lambda b, i: (jnp.minimum((tile(b, i) + 1) * per8, n8 - 1), 0)
    const2 = lambda b, i: (0, 0)
    in_specs = [
        pl.BlockSpec((SUBLANES, LRU_WIDTH), prev),
        pl.BlockSpec((tm, LRU_WIDTH), cur),
        pl.BlockSpec((SUBLANES, LRU_WIDTH), nxt),
        pl.BlockSpec((CONV_W, LRU_WIDTH), const2),
        pl.BlockSpec((1, LRU_WIDTH), const2),
        pl.BlockSpec((LRU_BLOCKS, LRU_BLOCK_W, 2 * LRU_BLOCK_W), lambda b, i: (0, 0, 0)),
        pl.BlockSpec((1, LRU_WIDTH), const2),
        pl.BlockSpec((1, LRU_WIDTH), const2),
        pl.BlockSpec((1, LRU_WIDTH), const2),
    ]
    args = [u, u, u, p["conv_w"], p["conv_b"], p["lru_w"][d], p["lru_ba"][d], p["lru_bx"][d], p["lru_lam"][d]]
    if reverse:
        in_specs += [pl.BlockSpec((tm, LRU_WIDTH), cur)] * 3
        args += [h_fwd, ggr, gate]
    return pl.pallas_call(
        functools.partial(_lru_kernel, reverse=reverse, tm=tm, nt=nt),
        grid=(batch, nt), in_specs=in_specs,
        out_specs=pl.BlockSpec((tm, LRU_WIDTH), cur),
        out_shape=jax.ShapeDtypeStruct((t, LRU_WIDTH), F32),
        scratch_shapes=[pltpu.VMEM((tm + 2 * SUBLANES, LRU_WIDTH), F32), pltpu.VMEM((tm, LRU_WIDTH), F32),
                        pltpu.VMEM((tm, LRU_WIDTH), F32), pltpu.VMEM((1, LRU_WIDTH), F32)],
        compiler_params=_cparams(("arbitrary", "arbitrary")), name="lru_bwd" if reverse else "lru_fwd",
    )(*args)


def _attn_kernel(qt_ref, k_ref, vt_ref, gb_ref, o_ref, m_s, l_s, acc_s, *, tq, tk, n_chunks):
    qt = qt_ref[0, 0]
    m_s[...] = jnp.full_like(m_s, -jnp.inf)
    l_s[...] = jnp.zeros_like(l_s)
    acc_s[...] = jnp.zeros_like(acc_s)

    def chunk(c, _):
        off = pl.multiple_of(c * tk, tk)
        s = jnp.dot(k_ref[pl.ds(off, tk), :], qt, preferred_element_type=F32)
        m_old = m_s[...]
        m_new = jnp.maximum(m_old, jnp.max(s, axis=0, keepdims=True))
        alpha = jnp.exp2(m_old - m_new)
        pr = jnp.exp2(s - m_new)
        l_s[...] = alpha * l_s[...] + jnp.sum(pr, axis=0, keepdims=True)
        pv = jnp.dot(vt_ref[0, :, pl.ds(off, tk)], pr.astype(BF16), preferred_element_type=F32)
        acc_s[...] = alpha * acc_s[...] + pv
        m_s[...] = m_new
        return 0

    lax.fori_loop(0, n_chunks, chunk, 0)
    out_t = acc_s[...] / l_s[...]
    for g in range(GROUP):
        cols = slice(g * HEAD_DIM, (g + 1) * HEAD_DIM)
        o_ref[:, cols] = gb_ref[:, cols] * out_t[:, g * tq:(g + 1) * tq].T


def _attention(qt, k, vt, gate, batch, seq_len, tq, tk):
    t = k.shape[0]
    nq = seq_len // tq
    half = GROUP * HEAD_DIM
    gate_col0 = D_MODEL // half
    return pl.pallas_call(
        functools.partial(_attn_kernel, tq=tq, tk=tk, n_chunks=seq_len // tk),
        grid=(batch, N_KV_HEADS, nq),
        in_specs=[
            pl.BlockSpec((1, 1, HEAD_DIM, GROUP * tq), lambda b, j, i: (b * nq + i, j, 0, 0)),
            pl.BlockSpec((seq_len, HEAD_DIM), lambda b, j, i: (b, j)),
            pl.BlockSpec((1, HEAD_DIM, seq_len), lambda b, j, i: (j, 0, b)),
            pl.BlockSpec((tq, half), lambda b, j, i: (b * nq + i, gate_col0 + j)),
        ],
        out_specs=pl.BlockSpec((tq, half), lambda b, j, i: (b * nq + i, j)),
        out_shape=jax.ShapeDtypeStruct((t, D_MODEL), F32),
        scratch_shapes=[pltpu.VMEM((1, GROUP * tq), F32), pltpu.VMEM((1, GROUP * tq), F32),
                        pltpu.VMEM((HEAD_DIM, GROUP * tq), F32)],
        compiler_params=_cparams(("arbitrary", "arbitrary", "arbitrary")), name="attention",
    )(qt, k, vt, gate)


def _outproj_router_kernel(x_ref, ma_ref, mb_ref, wout_ref, gffn_ref, wr_ref, br_ref, cnt0_ref,
                           x1_ref, h2_ref, route_ref, cnt_ref, cnt_s):
    tm = x_ref.shape[0]
    i = pl.program_id(0)

    @pl.when(i == 0)
    def _():
        cnt_s[...] = cnt0_ref[...]

    merged = (ma_ref[...] + mb_ref[...]).astype(BF16)
    x1 = x_ref[...] + jnp.dot(merged, wout_ref[...], preferred_element_type=F32)
    x1_ref[...] = x1
    h2 = x1 * lax.rsqrt(jnp.mean(x1 * x1, axis=-1, keepdims=True) + EPS) * gffn_ref[...]
    h2_ref[...] = h2

    z = jnp.dot(h2, wr_ref[...], preferred_element_type=F32, precision=lax.Precision.HIGHEST) + br_ref[...]
    lane = lax.broadcasted_iota(jnp.int32, (tm, ROUTE_LANES), 1)
    neg = -jnp.inf

    def first_argmax(v):
        m = jnp.max(v, axis=-1, keepdims=True)
        return m, jnp.min(jnp.where(v == m, lane, ROUTE_LANES), axis=-1, keepdims=True)

    zg = jnp.where(lane < N_GROUPS, z, neg)
    mg, grp = first_argmax(zg)
    p_sel = 1.0 / jnp.sum(jnp.exp(zg - mg), axis=-1, keepdims=True)
    e_lo = EXPERT_LANE0 + EXPERTS_PER_GROUP * grp
    ze = jnp.where((lane >= e_lo) & (lane < e_lo + EXPERTS_PER_GROUP), z, neg)
    m1, i1 = first_argmax(ze)
    m2, i2 = first_argmax(jnp.where(lane == i1, neg, ze))
    e2 = jnp.exp(m2 - m1)
    w0 = p_sel / (1.0 + e2)
    w1 = p_sel * e2 / (1.0 + e2)

    hot0 = lane == i1
    hot1 = lane == i2
    hot = jnp.where(hot0 | hot1, 1.0, 0.0)
    rr = lax.broadcasted_iota(jnp.int32, (tm, tm), 0)
    cc = lax.broadcasted_iota(jnp.int32, (tm, tm), 1)
    strict_lower = jnp.where(rr > cc, 1.0, 0.0).astype(BF16)
    before = jnp.dot(strict_lower, hot.astype(BF16), preferred_element_type=F32) + cnt_s[...]
    rank0 = jnp.sum(jnp.where(hot0, before, 0.0), axis=-1, keepdims=True)
    rank1 = jnp.sum(jnp.where(hot1, before, 0.0), axis=-1, keepdims=True)
    cnt_s[...] = cnt_s[...] + jnp.sum(hot, axis=0, keepdims=True)
    cnt_ref[...] = cnt_s[...]

    eid0 = (i1 - EXPERT_LANE0).astype(F32)
    eid1 = (i2 - EXPERT_LANE0).astype(F32)
    rec = jnp.zeros((tm, ROUTE_LANES), F32)
    for col, val in enumerate((eid0, eid1, w0, w1, rank0, rank1)):
        rec = jnp.where(lane == col, val, rec)
    route_ref[...] = rec


def _outproj_router(x2, m_a, m_b, cnt0, p, tm):
    t = x2.shape[0]
    row = lambda i: (i, 0)
    return pl.pallas_call(
        _outproj_router_kernel, grid=(t // tm,),
        in_specs=[pl.BlockSpec((tm, D_MODEL), row), pl.BlockSpec((tm, D_MODEL), row), pl.BlockSpec((tm, D_MODEL), row),
                  _const_spec((D_MODEL, D_MODEL)), _const_spec((1, D_MODEL)), _const_spec((D_MODEL, ROUTE_LANES)),
                  _const_spec((1, ROUTE_LANES)), _const_spec((1, ROUTE_LANES))],
        out_specs=(pl.BlockSpec((tm, D_MODEL), row), pl.BlockSpec((tm, D_MODEL), row),
                   pl.BlockSpec((tm, ROUTE_LANES), row), _const_spec((1, ROUTE_LANES))),
        out_shape=(jax.ShapeDtypeStruct((t, D_MODEL), F32), jax.ShapeDtypeStruct((t, D_MODEL), F32),
                   jax.ShapeDtypeStruct((t, ROUTE_LANES), F32), jax.ShapeDtypeStruct((1, ROUTE_LANES), F32)),
        scratch_shapes=[pltpu.VMEM((1, ROUTE_LANES), F32)],
        compiler_params=_cparams(("arbitrary",)), name="outproj_router",
    )(x2, m_a, m_b, p["w_out"], p["g_ffn"], p["w_route"], p["b_route"], cnt0)


def _dispatch_kernel(dest_ref, h_ref, xs_in_ref, xs_ref, sem, *, tm):
    del xs_in_ref
    i = pl.program_id(0)

    def row_copy(r, k):
        d = dest_ref[(i * tm + r) * TOP_K + k]
        return pltpu.make_async_copy(h_ref.at[pl.ds(r, 1), :], xs_ref.at[pl.ds(d, 1), :], sem)

    def issue(r, _):
        for k in range(TOP_K):
            row_copy(r, k).start()
        return 0

    lax.fori_loop(0, tm, issue, 0)

    def drain(r, _):
        for k in range(TOP_K):
            row_copy(r, k).wait()
        return 0

    lax.fori_loop(0, tm, drain, 0)


def _dispatch(dest_flat, h2, xs, tm):
    t = h2.shape[0]
    return pl.pallas_call(
        functools.partial(_dispatch_kernel, tm=tm),
        grid_spec=pltpu.PrefetchScalarGridSpec(
            num_scalar_prefetch=1, grid=(t // tm,),
            in_specs=[pl.BlockSpec((tm, D_MODEL), lambda i, d: (i, 0)), pl.BlockSpec(memory_space=pl.ANY)],
            out_specs=pl.BlockSpec(memory_space=pl.ANY),
            scratch_shapes=[pltpu.SemaphoreType.DMA]),
        out_shape=jax.ShapeDtypeStruct(xs.shape, xs.dtype),
        input_output_aliases={2: 0},
        compiler_params=_cparams(("arbitrary",)), name="moe_dispatch",
    )(dest_flat, h2, xs)


def _experts_kernel(blk_e_ref, n_used_ref, x_ref, wg_ref, wu_ref, wd_ref, y_ref, wg_s, wu_s, wd_s):
    i = pl.program_id(0)
    changed = jnp.logical_or(i == 0, blk_e_ref[i] != blk_e_ref[jnp.maximum(i - 1, 0)])

    @pl.when(changed)
    def _():
        wg_s[...] = wg_ref[0].astype(BF16)
        wu_s[...] = wu_ref[0].astype(BF16)
        wd_s[...] = wd_ref[0].astype(BF16)

    @pl.when(i < n_used_ref[0])
    def _():
        xb = x_ref[...].astype(BF16)
        gate = jnp.dot(xb, wg_s[...], preferred_element_type=F32)
        up = jnp.dot(xb, wu_s[...], preferred_element_type=F32)
        act = (jax.nn.silu(gate) * up).astype(BF16)
        y_ref[...] = jnp.dot(act, wd_s[...], preferred_element_type=F32)

    @pl.when(i >= n_used_ref[0])
    def _():
        y_ref[...] = jnp.zeros_like(y_ref)


def _experts(blk_e, n_used, xs, w_gate, w_up, w_down):
    n_rows = xs.shape[0]
    rb = MOE_ROWS
    wmap = lambda i, be, nu: (be[i], 0, 0)
    return pl.pallas_call(
        _experts_kernel,
        grid_spec=pltpu.PrefetchScalarGridSpec(
            num_scalar_prefetch=2, grid=(n_rows // rb,),
            in_specs=[pl.BlockSpec((rb, D_MODEL), lambda i, be, nu: (i, 0)),
                      pl.BlockSpec((1, D_MODEL, D_EXPERT), wmap),
                      pl.BlockSpec((1, D_MODEL, D_EXPERT), wmap),
                      pl.BlockSpec((1, D_EXPERT, D_MODEL), wmap)],
            out_specs=pl.BlockSpec((rb, D_MODEL), lambda i, be, nu: (i, 0)),
            scratch_shapes=[pltpu.VMEM((D_MODEL, D_EXPERT), BF16), pltpu.VMEM((D_MODEL, D_EXPERT), BF16),
                            pltpu.VMEM((D_EXPERT, D_MODEL), BF16)]),
        out_shape=jax.ShapeDtypeStruct((n_rows, D_MODEL), F32),
        compiler_params=_cparams(("arbitrary",)), name="moe_experts",
    )(blk_e, n_used, xs, w_gate, w_up, w_down)


def _combine_kernel(dest_ref, x1_ref, route_ref, ys_ref, o_ref, gbuf, sem, *, tm):
    i = pl.program_id(0)

    def row_copy(r, k):
        d = dest_ref[(i * tm + r) * TOP_K + k]
        return pltpu.make_async_copy(ys_ref.at[pl.ds(d, 1), :], gbuf.at[k, pl.ds(r, 1), :], sem)

    def issue(r, _):
        for k in range(TOP_K):
            row_copy(r, k).start()
        return 0

    lax.fori_loop(0, tm, issue, 0)

    def drain(r, _):
        for k in range(TOP_K):
            row_copy(r, k).wait()
        return 0

    lax.fori_loop(0, tm, drain, 0)
    w0 = route_ref[:, 2:3]
    w1 = route_ref[:, 3:4]
    o_ref[...] = x1_ref[...] + (gbuf[0] * w0 + gbuf[1] * w1)


def _combine(dest_flat, x1, route, ys, tm):
    t = x1.shape[0]
    return pl.pallas_call(
        functools.partial(_combine_kernel, tm=tm),
        grid_spec=pltpu.PrefetchScalarGridSpec(
            num_scalar_prefetch=1, grid=(t // tm,),
            in_specs=[pl.BlockSpec((tm, D_MODEL), lambda i, d: (i, 0)),
                      pl.BlockSpec((tm, ROUTE_LANES), lambda i, d: (i, 0)),
                      pl.BlockSpec(memory_space=pl.ANY)],
            out_specs=pl.BlockSpec((tm, D_MODEL), lambda i, d: (i, 0)),
            scratch_shapes=[pltpu.VMEM((TOP_K, tm, D_MODEL), F32), pltpu.SemaphoreType.DMA]),
        out_shape=jax.ShapeDtypeStruct((t, D_MODEL), F32),
        compiler_params=_cparams(("arbitrary",)), name="moe_combine",
    )(dest_flat, x1, route, ys)


def _rope_tables(seq_len):
    pos = jnp.arange(seq_len)
    inv = ROPE_THETA ** (-jnp.arange(0, AXIS_DIM, 2, dtype=F32) / AXIS_DIM)
    ang_r = (pos // GRID_W).astype(F32)[:, None] * inv
    ang_c = (pos % GRID_W).astype(F32)[:, None] * inv
    cos = jnp.concatenate([jnp.cos(ang_r), jnp.cos(ang_r), jnp.cos(ang_c), jnp.cos(ang_c)], axis=-1)
    sin = jnp.concatenate([-jnp.sin(ang_r), jnp.sin(ang_r), -jnp.sin(ang_c), jnp.sin(ang_c)], axis=-1)
    return cos, sin


def _prepare_params(l, max_seq, tm, g_mix, w_in, b_gate, conv_w, conv_b, lru_wa, lru_ba, lru_wx, lru_bx, lru_lam,
                    q_gain, k_gain, w_out, g_ffn, w_rg, b_rg, w_re, b_re):
    c_u, c_gr, c_q, c_k, c_v = (LRU_WIDTH, 2 * LRU_WIDTH, 2 * LRU_WIDTH + D_MODEL,
                                2 * LRU_WIDTH + D_MODEL + ATTN_KV, 2 * LRU_WIDTH + D_MODEL + 2 * ATTN_KV)
    w = w_in[l]
    w_nat = jnp.concatenate([w[:, :c_gr], w[:, c_q:c_k], w[:, c_v:]], axis=1).astype(BF16)
    w_tr = jnp.concatenate([w[:, c_gr:c_q], w[:, c_k:c_v]], axis=1).T.astype(BF16)
    cos, sin = _rope_tables(max_seq)
    pad = ROUTE_LANES - N_GROUPS - N_EXPERTS
    w_route = jnp.concatenate([w_rg[l], w_re[l], jnp.zeros((D_MODEL, pad), F32)], axis=1)
    b_route = jnp.concatenate([b_rg[l], b_re[l], jnp.zeros((pad,), F32)])[None, :]
    lru_w = jnp.concatenate([lru_wa[l], lru_wx[l]], axis=-1).astype(BF16)
    return dict(
        g_mix=g_mix[l][None, :], w_nat=w_nat, w_tr=w_tr, b_gate=b_gate[l][None, :],
        k_gain=k_gain[l][None, :], q_gain_t=jnp.broadcast_to(q_gain[l][:, None], (HEAD_DIM, tm)),
        cos=cos, sin=sin, cos_t=cos.T, sin_t=sin.T,
        conv_w=conv_w[l], conv_b=conv_b[l][None, :], lru_w=lru_w,
        lru_ba=lru_ba[l].reshape(2, 1, LRU_WIDTH), lru_bx=lru_bx[l].reshape(2, 1, LRU_WIDTH),
        lru_lam=lru_lam[l].reshape(2, 1, LRU_WIDTH),
        w_out=w_out[l].astype(BF16), g_ffn=g_ffn[l][None, :], w_route=w_route, b_route=b_route,
    )


def _mixer(x, p, cnt0, tm, tk):
    batch, seq_len, _ = x.shape
    x2 = x.reshape(batch * seq_len, D_MODEL)
    u, ggr, k, gate, qt, vt = _inproj(x2, seq_len, p, tm)
    h_fwd = _lru_scan(u, gate, ggr, None, p, 0, batch, seq_len, tm, reverse=False)
    m_a = _lru_scan(u, gate, ggr, h_fwd, p, 1, batch, seq_len, tm, reverse=True)
    m_b = _attention(qt, k, vt, gate, batch, seq_len, tm, min(tk, seq_len))
    return _outproj_router(x2, m_a, m_b, cnt0, p, tm)


def _layer(xs_in, l, weights):
    (g_mix, w_in, b_gate, conv_w, conv_b, lru_wa, lru_ba, lru_wx, lru_bx, lru_lam,
     q_gain, k_gain, w_out, g_ffn, w_rg, b_rg, w_re, b_re, w_gate, w_up, w_down) = weights
    tm = ROW_TILE
    max_seq = max(x.shape[1] for x in xs_in)
    p = _prepare_params(l, max_seq, tm, g_mix, w_in, b_gate, conv_w, conv_b, lru_wa, lru_ba, lru_wx, lru_bx,
                        lru_lam, q_gain, k_gain, w_out, g_ffn, w_rg, b_rg, w_re, b_re)

    cnt = jnp.zeros((1, ROUTE_LANES), F32)
    x1s, h2s, routes = [], [], []
    for x in xs_in:
        x1, h2, route, cnt = _mixer(x, p, cnt, tm, KV_CHUNK)
        x1s.append(x1)
        h2s.append(h2)
        routes.append(route)

    n_tok = sum(x1.shape[0] for x1 in x1s)
    n_rows = n_tok * TOP_K + N_EXPERTS * MOE_ROWS
    n_blk = n_rows // MOE_ROWS
    counts = cnt[0, EXPERT_LANE0:EXPERT_LANE0 + N_EXPERTS].astype(jnp.int32)
    padded = (counts + MOE_ROWS - 1) // MOE_ROWS * MOE_ROWS
    pend = jnp.cumsum(padded)
    pstart = pend - padded
    blk_e = jnp.minimum(jnp.searchsorted(pend, jnp.arange(n_blk, dtype=jnp.int32) * MOE_ROWS, side="right"),
                        N_EXPERTS - 1).astype(jnp.int32)
    n_used = (pend[-1:] // MOE_ROWS).astype(jnp.int32)

    dests = []
    xs = jnp.zeros((n_rows, D_MODEL), F32)
    for h2, route in zip(h2s, routes):
        eid = route[:, 0:TOP_K].astype(jnp.int32)
        rank = route[:, 4:4 + TOP_K].astype(jnp.int32)
        dest = (pstart[eid] + rank).reshape(-1)
        dests.append(dest)
        xs = _dispatch(dest, h2, xs, tm)
    ys = _experts(blk_e, n_used, xs, w_gate[l], w_up[l], w_down[l])
    outs = []
    for x, x1, route, dest in zip(xs_in, x1s, routes, dests):
        outs.append(_combine(dest, x1, route, ys, tm).reshape(x.shape))
    return outs


def kernel(x_prompt, x_sample, g_mix, w_in, b_gate, conv_w, conv_b, lru_wa, lru_ba, lru_wx, lru_bx, lru_lam,
           q_gain, k_gain, w_out, g_ffn, w_rg, b_rg, w_re, b_re, w_gate, w_up, w_down):
    weights = (g_mix, w_in, b_gate, conv_w, conv_b, lru_wa, lru_ba, lru_wx, lru_bx, lru_lam,
               q_gain, k_gain, w_out, g_ffn, w_rg, b_rg, w_re, b_re, w_gate, w_up, w_down)
    xs = [x_prompt, x_sample]
    for l in range(g_mix.shape[0]):
        xs = _layer(xs, l, weights)
    return tuple(xs)
```

```python
import functools
import math

import jax
import jax.numpy as jnp
from jax import lax
from jax.experimental import pallas as pl
from jax.experimental.pallas import tpu as pltpu

F32 = jnp.float32
BF16 = jnp.bfloat16

D_MODEL = 1024
N_HEADS = 8
N_KV_HEADS = 2
GROUP = N_HEADS // N_KV_HEADS
HEAD_DIM = D_MODEL // N_HEADS
AXIS_DIM = HEAD_DIM // 2
N_FREQ = AXIS_DIM // 2
GRID_W = 64
ROPE_THETA = 10000.0
LRU_WIDTH = D_MODEL
LRU_BLOCKS = 8
LRU_BLOCK_W = LRU_WIDTH // LRU_BLOCKS
CONV_W = 4
CONV_PAD_LEFT = 2
LRU_C = 8.0
N_GROUPS = 4
EXPERTS_PER_GROUP = 8
N_EXPERTS = N_GROUPS * EXPERTS_PER_GROUP
TOP_K = 2
D_EXPERT = D_MODEL // 2
ATTN_KV = N_KV_HEADS * HEAD_DIM
EPS = 1e-6

LANES = 128
SUBLANES = 8
VMEM_LIMIT_BYTES = 56 * 1024 * 1024

ROW_TILE = 256
KV_CHUNK = 512
MOE_ROWS = 256
ROW_DMA_UNROLL = 8
ROUTE_LANES = LANES
EXPERT_LANE0 = N_GROUPS

NAT_COLS = 2 * LRU_WIDTH + ATTN_KV + 2 * D_MODEL
TR_ROWS = D_MODEL + ATTN_KV
LOG2E = math.log2(math.e)
SAFE_EXP2_RANGE = 100.0
SCORE_BOUND_SLACK = 1.01


def _cparams(semantics):
    return pltpu.CompilerParams(dimension_semantics=semantics, vmem_limit_bytes=VMEM_LIMIT_BYTES)


def _const_spec(shape):
    nd = len(shape)
    return pl.BlockSpec(shape, lambda *_: (0,) * nd)


def _swap_halves_rows(x):
    return jnp.concatenate([x[N_FREQ:AXIS_DIM], x[0:N_FREQ], x[AXIS_DIM + N_FREQ:], x[AXIS_DIM:AXIS_DIM + N_FREQ]], axis=0)


def _inproj_kernel(x_ref, gmix_ref, wnat_ref, wtr_ref, bgate_ref, kgain_ref, qgain_ref,
                   cos_ref, sin_ref, cost_ref, sint_ref,
                   u_ref, ggr_ref, k_ref, gate_ref, qt_ref, vt_ref):
    tm = x_ref.shape[0]
    x = x_ref[...]
    h = (x * lax.rsqrt(jnp.mean(x * x, axis=-1, keepdims=True) + EPS) * gmix_ref[...]).astype(BF16)

    def nat(lo, hi):
        return jnp.dot(h, wnat_ref[:, lo:hi], preferred_element_type=F32)

    u_ref[...] = nat(0, LRU_WIDTH)
    ggr_ref[...] = jax.nn.gelu(nat(LRU_WIDTH, 2 * LRU_WIDTH))
    k0 = 2 * LRU_WIDTH
    kraw = nat(k0, k0 + ATTN_KV)
    g0 = k0 + ATTN_KV
    gate_ref[...] = jax.nn.sigmoid(nat(g0, g0 + 2 * D_MODEL) + bgate_ref[...])

    cos = cos_ref[...]
    sin = sin_ref[...]
    lane = lax.broadcasted_iota(jnp.int32, (tm, HEAD_DIM), 1)
    first_half = (lane % AXIS_DIM) < N_FREQ
    for j in range(N_KV_HEADS):
        kj = kraw[:, j * HEAD_DIM:(j + 1) * HEAD_DIM]
        kn = kj * lax.rsqrt(jnp.mean(kj * kj, axis=-1, keepdims=True) + EPS) * kgain_ref[...]
        partner = jnp.where(first_half, pltpu.roll(kn, HEAD_DIM - N_FREQ, 1), pltpu.roll(kn, N_FREQ, 1))
        k_ref[:, j * HEAD_DIM:(j + 1) * HEAD_DIM] = (kn * cos + partner * sin).astype(BF16)

    zt = lax.dot_general(wtr_ref[...], h, (((1,), (1,)), ((), ())), preferred_element_type=F32)
    cost = cost_ref[...]
    sint = sint_ref[...]
    qgain = qgain_ref[...]
    qscale = (HEAD_DIM ** -0.5) * LOG2E
    for hd in range(N_HEADS):
        xq = zt[hd * HEAD_DIM:(hd + 1) * HEAD_DIM, :]
        xn = xq * lax.rsqrt(jnp.mean(xq * xq, axis=0, keepdims=True) + EPS) * qgain
        rot = (xn * cost + _swap_halves_rows(xn) * sint) * qscale
        g = hd % GROUP
        qt_ref[0, hd // GROUP, :, g * tm:(g + 1) * tm] = rot.astype(BF16)
    for j in range(N_KV_HEADS):
        r0 = D_MODEL + j * HEAD_DIM
        vt_ref[j] = zt[r0:r0 + HEAD_DIM, :].astype(BF16)


def _inproj(x2, seq_len, p, tm):
    t = x2.shape[0]
    n_pos = seq_len // tm
    grid = (t // tm,)
    row = lambda i: (i, 0)
    in_specs = [
        pl.BlockSpec((tm, D_MODEL), row),
        _const_spec((1, D_MODEL)),
        _const_spec((D_MODEL, NAT_COLS)),
        _const_spec((TR_ROWS, D_MODEL)),
        _const_spec((1, 2 * D_MODEL)),
        _const_spec((1, HEAD_DIM)),
        _const_spec((HEAD_DIM, tm)),
        pl.BlockSpec((tm, HEAD_DIM), lambda i: (i % n_pos, 0)),
        pl.BlockSpec((tm, HEAD_DIM), lambda i: (i % n_pos, 0)),
        pl.BlockSpec((HEAD_DIM, tm), lambda i: (0, i % n_pos)),
        pl.BlockSpec((HEAD_DIM, tm), lambda i: (0, i % n_pos)),
    ]
    out_shape = (
        jax.ShapeDtypeStruct((t, LRU_WIDTH), F32),
        jax.ShapeDtypeStruct((t, LRU_WIDTH), F32),
        jax.ShapeDtypeStruct((t, ATTN_KV), BF16),
        jax.ShapeDtypeStruct((t, 2 * D_MODEL), F32),
        jax.ShapeDtypeStruct((t // tm, N_KV_HEADS, HEAD_DIM, GROUP * tm), BF16),
        jax.ShapeDtypeStruct((N_KV_HEADS, HEAD_DIM, t), BF16),
    )
    out_specs = (
        pl.BlockSpec((tm, LRU_WIDTH), row),
        pl.BlockSpec((tm, LRU_WIDTH), row),
        pl.BlockSpec((tm, ATTN_KV), row),
        pl.BlockSpec((tm, 2 * D_MODEL), row),
        pl.BlockSpec((1, N_KV_HEADS, HEAD_DIM, GROUP * tm), lambda i: (i, 0, 0, 0)),
        pl.BlockSpec((N_KV_HEADS, HEAD_DIM, tm), lambda i: (0, 0, i)),
    )
    return pl.pallas_call(
        _inproj_kernel, grid=grid, in_specs=in_specs, out_specs=out_specs, out_shape=out_shape,
        compiler_params=_cparams(("arbitrary",)), name="inproj",
    )(x2, p["g_mix"], p["w_nat"], p["w_tr"], p["b_gate"], p["k_gain"], p["q_gain_t"],
      p["cos"], p["sin"], p["cos_t"], p["sin_t"])


def _lru_kernel(*refs, reverse, tm, nt):
    if reverse:
        (up_ref, uc_ref, un_ref, cw_ref, cb_ref, w_ref, ba_ref, bx_ref, lam_ref, hf_ref, ggr_ref, ga_ref,
         out_ref, ubuf, a_s, g_s, carry) = refs
    else:
        (up_ref, uc_ref, un_ref, cw_ref, cb_ref, w_ref, ba_ref, bx_ref, lam_ref,
         out_ref, ubuf, a_s, g_s, carry) = refs
    i = pl.program_id(1)
    ti = (nt - 1 - i) if reverse else i

    @pl.when(i == 0)
    def _():
        carry[...] = jnp.zeros_like(carry)

    ubuf[0:SUBLANES] = jnp.where(ti == 0, 0.0, up_ref[...])
    ubuf[SUBLANES:SUBLANES + tm] = uc_ref[...]
    ubuf[SUBLANES + tm:2 * SUBLANES + tm] = jnp.where(ti == nt - 1, 0.0, un_ref[...])
    cw = cw_ref[...]
    base = SUBLANES - CONV_PAD_LEFT
    xc = cb_ref[...] + sum(cw[j:j + 1] * ubuf[base + j:base + j + tm] for j in range(CONV_W))

    xcb = xc.astype(BF16)
    lam = lam_ref[...]
    c_log = LRU_C * (jnp.minimum(lam, 0.0) - jnp.log1p(jnp.exp(-jnp.abs(lam))))
    for hb in range(LRU_BLOCKS):
        sl = slice(hb * LRU_BLOCK_W, (hb + 1) * LRU_BLOCK_W)
        gz = jnp.dot(xcb[:, sl], w_ref[hb], preferred_element_type=F32)
        r = jax.nn.sigmoid(gz[:, :LRU_BLOCK_W] + ba_ref[:, sl])
        ig = jax.nn.sigmoid(gz[:, LRU_BLOCK_W:] + bx_ref[:, sl])
        log_a = r * c_log[:, sl]
        a = jnp.exp(log_a)
        a_s[:, sl] = a
        g_s[:, sl] = jnp.sqrt(-jnp.tanh(log_a) * (1.0 + a * a)) * (ig * xc[:, sl])

    n_chunk = tm // SUBLANES
    srow = lax.broadcasted_iota(jnp.int32, (SUBLANES, LRU_WIDTH), 0)

    def chunk(c, h_prev):
        ci = (n_chunk - 1 - c) if reverse else c
        off = pl.multiple_of(ci * SUBLANES, SUBLANES)
        a = a_s[pl.ds(off, SUBLANES), :]
        b = g_s[pl.ds(off, SUBLANES), :]
        for d in (1, 2, 4):
            shift = (SUBLANES - d) if reverse else d
            valid = (srow < SUBLANES - d) if reverse else (srow >= d)
            a_sh = jnp.where(valid, pltpu.roll(a, shift, 0), 1.0)
            b_sh = jnp.where(valid, pltpu.roll(b, shift, 0), 0.0)
            b = a * b_sh + b
            a = a * a_sh
        h = a * h_prev + b
        if reverse:
            rows = pl.ds(off, SUBLANES)
            out_ref[rows, :] = ga_ref[rows, :] * (ggr_ref[rows, :] * (hf_ref[rows, :] + h))
            return h[0:1]
        out_ref[pl.ds(off, SUBLANES), :] = h
        return h[SUBLANES - 1:SUBLANES]

    carry[...] = lax.fori_loop(0, n_chunk, chunk, carry[...])


def _lru_scan(u, gate, ggr, h_fwd, p, d, batch, seq_len, tm, reverse):
    t = u.shape[0]
    nt = seq_len // tm
    per8 = tm // SUBLANES
    n8 = t // SUBLANES

    def tile(b, i):
        return b * nt + ((nt - 1 - i) if reverse else i)

    cur = lambda b, i: (tile(b, i), 0)
    prev = lambda b, i: (jnp.maximum(tile(b, i) * per8 - 1, 0), 0)
    nxt = lambda b, i: (jnp.minimum((tile(b, i) + 1) * per8, n8 - 1), 0)
    const2 = lambda b, i: (0, 0)
    in_specs = [
        pl.BlockSpec((SUBLANES, LRU_WIDTH), prev),
        pl.BlockSpec((tm, LRU_WIDTH), cur),
        pl.BlockSpec((SUBLANES, LRU_WIDTH), nxt),
        pl.BlockSpec((CONV_W, LRU_WIDTH), const2),
        pl.BlockSpec((1, LRU_WIDTH), const2),
        pl.BlockSpec((LRU_BLOCKS, LRU_BLOCK_W, 2 * LRU_BLOCK_W), lambda b, i: (0, 0, 0)),
        pl.BlockSpec((1, LRU_WIDTH), const2),
        pl.BlockSpec((1, LRU_WIDTH), const2),
        pl.BlockSpec((1, LRU_WIDTH), const2),
    ]
    args = [u, u, u, p["conv_w"], p["conv_b"], p["lru_w"][d], p["lru_ba"][d], p["lru_bx"][d], p["lru_lam"][d]]
    if reverse:
        in_specs += [pl.BlockSpec((tm, LRU_WIDTH), cur)] * 3
        args += [h_fwd, ggr, gate]
    return pl.pallas_call(
        functools.partial(_lru_kernel, reverse=reverse, tm=tm, nt=nt),
        grid=(batch, nt), in_specs=in_specs,
        out_specs=pl.BlockSpec((tm, LRU_WIDTH), cur),
        out_shape=jax.ShapeDtypeStruct((t, LRU_WIDTH), F32),
        scratch_shapes=[pltpu.VMEM((tm + 2 * SUBLANES, LRU_WIDTH), F32), pltpu.VMEM((tm, LRU_WIDTH), F32),
                        pltpu.VMEM((tm, LRU_WIDTH), F32), pltpu.VMEM((1, LRU_WIDTH), F32)],
        compiler_params=_cparams(("arbitrary", "arbitrary")), name="lru_bwd" if reverse else "lru_fwd",
    )(*args)


def _attn_kernel(bound_ref, qt_ref, k_ref, vt_ref, gb_ref, o_ref, m_s, l_s, acc_s, *, tq, tk, n_chunks):
    nq_cols = GROUP * tq
    bound = bound_ref[0]
    acc_s[...] = jnp.zeros_like(acc_s)
    l_s[...] = jnp.zeros_like(l_s)

    def kv_chunk(c):
        off = pl.multiple_of(c * tk, tk)
        return k_ref[pl.ds(off, tk), :], vt_ref[0, :, pl.ds(off, tk)]

    @pl.when(2.0 * bound <= SAFE_EXP2_RANGE)
    def _():
        qt = qt_ref[0, 0]

        def chunk(c, _):
            kc, vc = kv_chunk(c)
            pr = jnp.exp2(jnp.dot(kc, qt, preferred_element_type=F32) - bound)
            l_s[...] += jnp.sum(pr.reshape(tk // SUBLANES, SUBLANES, nq_cols), axis=0)
            acc_s[...] += jnp.dot(vc, pr.astype(BF16), preferred_element_type=F32)
            return 0

        lax.fori_loop(0, n_chunks, chunk, 0, unroll=4)

    @pl.when(2.0 * bound > SAFE_EXP2_RANGE)
    def _():
        qt = qt_ref[0, 0]
        m_s[...] = jnp.full_like(m_s, -jnp.inf)

        def chunk(c, _):
            kc, vc = kv_chunk(c)
            s = jnp.dot(kc, qt, preferred_element_type=F32)
            m_old = m_s[...]
            m_new = jnp.maximum(m_old, jnp.max(s, axis=0, keepdims=True))
            alpha = jnp.exp2(m_old - m_new)
            pr = jnp.exp2(s - m_new)
            l_s[...] = alpha * l_s[...] + jnp.sum(pr.reshape(tk // SUBLANES, SUBLANES, nq_cols), axis=0)
            acc_s[...] = alpha * acc_s[...] + jnp.dot(vc, pr.astype(BF16), preferred_element_type=F32)
            m_s[...] = m_new
            return 0

        lax.fori_loop(0, n_chunks, chunk, 0)

    out_t = acc_s[...] / jnp.sum(l_s[...], axis=0, keepdims=True)
    for g in range(GROUP):
        cols = slice(g * HEAD_DIM, (g + 1) * HEAD_DIM)
        o_ref[:, cols] = gb_ref[:, cols] * out_t[:, g * tq:(g + 1) * tq].T


def _attention(bound, qt, k, vt, gate, batch, seq_len, tq, tk):
    t = k.shape[0]
    nq = seq_len // tq
    half = GROUP * HEAD_DIM
    gate_col0 = D_MODEL // half
    return pl.pallas_call(
        functools.partial(_attn_kernel, tq=tq, tk=tk, n_chunks=seq_len // tk),
        grid=(batch, N_KV_HEADS, nq),
        in_specs=[
            pl.BlockSpec(memory_space=pltpu.SMEM),
            pl.BlockSpec((1, 1, HEAD_DIM, GROUP * tq), lambda b, j, i: (b * nq + i, j, 0, 0)),
            pl.BlockSpec((seq_len, HEAD_DIM), lambda b, j, i: (b, j)),
            pl.BlockSpec((1, HEAD_DIM, seq_len), lambda b, j, i: (j, 0, b)),
            pl.BlockSpec((tq, half), lambda b, j, i: (b * nq + i, gate_col0 + j)),
        ],
        out_specs=pl.BlockSpec((tq, half), lambda b, j, i: (b * nq + i, j)),
        out_shape=jax.ShapeDtypeStruct((t, D_MODEL), F32),
        scratch_shapes=[pltpu.VMEM((1, GROUP * tq), F32), pltpu.VMEM((SUBLANES, GROUP * tq), F32),
                        pltpu.VMEM((HEAD_DIM, GROUP * tq), F32)],
        compiler_params=_cparams(("arbitrary", "arbitrary", "arbitrary")), name="attention",
    )(bound, qt, k, vt, gate)


def _outproj_router_kernel(x_ref, ma_ref, mb_ref, wout_ref, gffn_ref, wr_ref, br_ref, cnt0_ref,
                           x1_ref, h2_ref, route_ref, cnt_ref, cnt_s):
    tm = x_ref.shape[0]
    i = pl.program_id(0)

    @pl.when(i == 0)
    def _():
        cnt_s[...] = cnt0_ref[...]

    merged = (ma_ref[...] + mb_ref[...]).astype(BF16)
    x1 = x_ref[...] + jnp.dot(merged, wout_ref[...], preferred_element_type=F32)
    x1_ref[...] = x1
    h2 = x1 * lax.rsqrt(jnp.mean(x1 * x1, axis=-1, keepdims=True) + EPS) * gffn_ref[...]
    h2_ref[...] = h2

    z = jnp.dot(h2, wr_ref[...], preferred_element_type=F32, precision=lax.Precision.HIGHEST) + br_ref[...]
    lane = lax.broadcasted_iota(jnp.int32, (tm, ROUTE_LANES), 1)
    neg = -jnp.inf

    def first_argmax(v):
        m = jnp.max(v, axis=-1, keepdims=True)
        return m, jnp.min(jnp.where(v == m, lane, ROUTE_LANES), axis=-1, keepdims=True)

    zg = jnp.where(lane < N_GROUPS, z, neg)
    mg, grp = first_argmax(zg)
    p_sel = 1.0 / jnp.sum(jnp.exp(zg - mg), axis=-1, keepdims=True)
    e_lo = EXPERT_LANE0 + EXPERTS_PER_GROUP * grp
    ze = jnp.where((lane >= e_lo) & (lane < e_lo + EXPERTS_PER_GROUP), z, neg)
    m1, i1 = first_argmax(ze)
    m2, i2 = first_argmax(jnp.where(lane == i1, neg, ze))
    e2 = jnp.exp(m2 - m1)
    w0 = p_sel / (1.0 + e2)
    w1 = p_sel * e2 / (1.0 + e2)

    hot0 = lane == i1
    hot1 = lane == i2
    hot = jnp.where(hot0 | hot1, 1.0, 0.0)
    rr = lax.broadcasted_iota(jnp.int32, (tm, tm), 0)
    cc = lax.broadcasted_iota(jnp.int32, (tm, tm), 1)
    strict_lower = jnp.where(rr > cc, 1.0, 0.0).astype(BF16)
    before = jnp.dot(strict_lower, hot.astype(BF16), preferred_element_type=F32) + cnt_s[...]
    rank0 = jnp.sum(jnp.where(hot0, before, 0.0), axis=-1, keepdims=True)
    rank1 = jnp.sum(jnp.where(hot1, before, 0.0), axis=-1, keepdims=True)
    cnt_s[...] = cnt_s[...] + jnp.sum(hot, axis=0, keepdims=True)
    cnt_ref[...] = cnt_s[...]

    eid0 = (i1 - EXPERT_LANE0).astype(F32)
    eid1 = (i2 - EXPERT_LANE0).astype(F32)
    rec = jnp.zeros((tm, ROUTE_LANES), F32)
    for col, val in enumerate((eid0, eid1, w0, w1, rank0, rank1)):
        rec = jnp.where(lane == col, val, rec)
    route_ref[...] = rec


def _outproj_router(x2, m_a, m_b, cnt0, p, tm):
    t = x2.shape[0]
    row = lambda i: (i, 0)
    return pl.pallas_call(
        _outproj_router_kernel, grid=(t // tm,),
        in_specs=[pl.BlockSpec((tm, D_MODEL), row), pl.BlockSpec((tm, D_MODEL), row), pl.BlockSpec((tm, D_MODEL), row),
                  _const_spec((D_MODEL, D_MODEL)), _const_spec((1, D_MODEL)), _const_spec((D_MODEL, ROUTE_LANES)),
                  _const_spec((1, ROUTE_LANES)), _const_spec((1, ROUTE_LANES))],
        out_specs=(pl.BlockSpec((tm, D_MODEL), row), pl.BlockSpec((tm, D_MODEL), row),
                   pl.BlockSpec((tm, ROUTE_LANES), row), _const_spec((1, ROUTE_LANES))),
        out_shape=(jax.ShapeDtypeStruct((t, D_MODEL), F32), jax.ShapeDtypeStruct((t, D_MODEL), F32),
                   jax.ShapeDtypeStruct((t, ROUTE_LANES), F32), jax.ShapeDtypeStruct((1, ROUTE_LANES), F32)),
        scratch_shapes=[pltpu.VMEM((1, ROUTE_LANES), F32)],
        compiler_params=_cparams(("arbitrary",)), name="outproj_router",
    )(x2, m_a, m_b, p["w_out"], p["g_ffn"], p["w_route"], p["b_route"], cnt0)


def _dispatch_kernel(dest_ref, h_ref, xs_in_ref, xs_ref, sem, *, tm):
    del xs_in_ref
    i = pl.program_id(0)

    base = i * (tm * TOP_K)

    def issue(r, _):
        for k in range(TOP_K):
            d = dest_ref[base + r * TOP_K + k]
            pltpu.make_async_copy(h_ref.at[pl.ds(r, 1), :], xs_ref.at[pl.ds(d, 1), :], sem).start()
        return 0

    lax.fori_loop(0, tm, issue, 0, unroll=ROW_DMA_UNROLL)
    for k in range(TOP_K):
        pltpu.make_async_copy(h_ref, xs_ref.at[pl.ds(0, tm), :], sem).wait()


def _dispatch(dest_flat, h2, xs, tm):
    t = h2.shape[0]
    return pl.pallas_call(
        functools.partial(_dispatch_kernel, tm=tm),
        grid_spec=pltpu.PrefetchScalarGridSpec(
            num_scalar_prefetch=1, grid=(t // tm,),
            in_specs=[pl.BlockSpec((tm, D_MODEL), lambda i, d: (i, 0)), pl.BlockSpec(memory_space=pl.ANY)],
            out_specs=pl.BlockSpec(memory_space=pl.ANY),
            scratch_shapes=[pltpu.SemaphoreType.DMA]),
        out_shape=jax.ShapeDtypeStruct(xs.shape, xs.dtype),
        input_output_aliases={2: 0},
        compiler_params=_cparams(("arbitrary",)), name="moe_dispatch",
    )(dest_flat, h2, xs)


def _experts_kernel(blk_e_ref, n_used_ref, x_ref, wg_ref, wu_ref, wd_ref, y_ref, wg_s, wu_s, wd_s):
    i = pl.program_id(0)
    changed = jnp.logical_or(i == 0, blk_e_ref[i] != blk_e_ref[jnp.maximum(i - 1, 0)])

    @pl.when(changed)
    def _():
        wg_s[...] = wg_ref[0].astype(BF16)
        wu_s[...] = wu_ref[0].astype(BF16)
        wd_s[...] = wd_ref[0].astype(BF16)

    @pl.when(i < n_used_ref[0])
    def _():
        xb = x_ref[...].astype(BF16)
        gate = jnp.dot(xb, wg_s[...], preferred_element_type=F32)
        up = jnp.dot(xb, wu_s[...], preferred_element_type=F32)
        act = (jax.nn.silu(gate) * up).astype(BF16)
        y_ref[...] = jnp.dot(act, wd_s[...], preferred_element_type=F32)

    @pl.when(i >= n_used_ref[0])
    def _():
        y_ref[...] = jnp.zeros_like(y_ref)


def _experts(blk_e, n_used, xs, w_gate, w_up, w_down):
    n_rows = xs.shape[0]
    rb = MOE_ROWS
    wmap = lambda i, be, nu: (be[i], 0, 0)
    return pl.pallas_call(
        _experts_kernel,
        grid_spec=pltpu.PrefetchScalarGridSpec(
            num_scalar_prefetch=2, grid=(n_rows // rb,),
            in_specs=[pl.BlockSpec((rb, D_MODEL), lambda i, be, nu: (i, 0)),
                      pl.BlockSpec((1, D_MODEL, D_EXPERT), wmap),
                      pl.BlockSpec((1, D_MODEL, D_EXPERT), wmap),
                      pl.BlockSpec((1, D_EXPERT, D_MODEL), wmap)],
            out_specs=pl.BlockSpec((rb, D_MODEL), lambda i, be, nu: (i, 0)),
            scratch_shapes=[pltpu.VMEM((D_MODEL, D_EXPERT), BF16), pltpu.VMEM((D_MODEL, D_EXPERT), BF16),
                            pltpu.VMEM((D_EXPERT, D_MODEL), BF16)]),
        out_shape=jax.ShapeDtypeStruct((n_rows, D_MODEL), F32),
        compiler_params=_cparams(("arbitrary",)), name="moe_experts",
    )(blk_e, n_used, xs, w_gate, w_up, w_down)


def _combine_kernel(dest_ref, x1_ref, route_ref, ys_ref, o_ref, gbuf, sem, *, tm):
    i = pl.program_id(0)

    base = i * (tm * TOP_K)

    def issue(r, _):
        for k in range(TOP_K):
            d = dest_ref[base + r * TOP_K + k]
            pltpu.make_async_copy(ys_ref.at[pl.ds(d, 1), :], gbuf.at[k, pl.ds(r, 1), :], sem).start()
        return 0

    lax.fori_loop(0, tm, issue, 0, unroll=ROW_DMA_UNROLL)
    for k in range(TOP_K):
        pltpu.make_async_copy(ys_ref.at[pl.ds(0, tm), :], gbuf.at[k], sem).wait()
    w0 = route_ref[:, 2:3]
    w1 = route_ref[:, 3:4]
    o_ref[...] = x1_ref[...] + (gbuf[0] * w0 + gbuf[1] * w1)


def _combine(dest_flat, x1, route, ys, tm):
    t = x1.shape[0]
    return pl.pallas_call(
        functools.partial(_combine_kernel, tm=tm),
        grid_spec=pltpu.PrefetchScalarGridSpec(
            num_scalar_prefetch=1, grid=(t // tm,),
            in_specs=[pl.BlockSpec((tm, D_MODEL), lambda i, d: (i, 0)),
                      pl.BlockSpec((tm, ROUTE_LANES), lambda i, d: (i, 0)),
                      pl.BlockSpec(memory_space=pl.ANY)],
            out_specs=pl.BlockSpec((tm, D_MODEL), lambda i, d: (i, 0)),
            scratch_shapes=[pltpu.VMEM((TOP_K, tm, D_MODEL), F32), pltpu.SemaphoreType.DMA]),
        out_shape=jax.ShapeDtypeStruct((t, D_MODEL), F32),
        compiler_params=_cparams(("arbitrary",)), name="moe_combine",
    )(dest_flat, x1, route, ys)


def _rope_tables(seq_len):
    inv = ROPE_THETA ** (-jnp.arange(0, AXIS_DIM, 2, dtype=F32) / AXIS_DIM)
    n_grid_rows = seq_len // GRID_W
    ang_r = jnp.arange(n_grid_rows, dtype=F32)[:, None] * inv
    ang_c = jnp.arange(GRID_W, dtype=F32)[:, None] * inv
    by_row = lambda a: jnp.repeat(a, GRID_W, axis=0)
    by_col = lambda a: jnp.tile(a, (n_grid_rows, 1))
    cos_r, sin_r, cos_c, sin_c = by_row(jnp.cos(ang_r)), by_row(jnp.sin(ang_r)), by_col(jnp.cos(ang_c)), by_col(jnp.sin(ang_c))
    cos = jnp.concatenate([cos_r, cos_r, cos_c, cos_c], axis=-1)
    sin = jnp.concatenate([-sin_r, sin_r, -sin_c, sin_c], axis=-1)
    return cos, sin


def _prepare_params(l, max_seq, tm, g_mix, w_in, b_gate, conv_w, conv_b, lru_wa, lru_ba, lru_wx, lru_bx, lru_lam,
                    q_gain, k_gain, w_out, g_ffn, w_rg, b_rg, w_re, b_re):
    c_u, c_gr, c_q, c_k, c_v = (LRU_WIDTH, 2 * LRU_WIDTH, 2 * LRU_WIDTH + D_MODEL,
                                2 * LRU_WIDTH + D_MODEL + ATTN_KV, 2 * LRU_WIDTH + D_MODEL + 2 * ATTN_KV)
    w = w_in[l]
    w_nat = jnp.concatenate([w[:, :c_gr], w[:, c_q:c_k], w[:, c_v:]], axis=1).astype(BF16)
    w_tr = jnp.concatenate([w[:, c_gr:c_q], w[:, c_k:c_v]], axis=1).T.astype(BF16)
    cos, sin = _rope_tables(max_seq)
    pad = ROUTE_LANES - N_GROUPS - N_EXPERTS
    w_route = jnp.concatenate([w_rg[l], w_re[l], jnp.zeros((D_MODEL, pad), F32)], axis=1)
    b_route = jnp.concatenate([b_rg[l], b_re[l], jnp.zeros((pad,), F32)])[None, :]
    lru_w = jnp.concatenate([lru_wa[l], lru_wx[l]], axis=-1).astype(BF16)
    score_bound = (SCORE_BOUND_SLACK * LOG2E * math.sqrt(HEAD_DIM)
                   * jnp.max(jnp.abs(q_gain[l])) * jnp.max(jnp.abs(k_gain[l]))).reshape(1)
    return dict(
        score_bound=score_bound,
        g_mix=g_mix[l][None, :], w_nat=w_nat, w_tr=w_tr, b_gate=b_gate[l][None, :],
        k_gain=k_gain[l][None, :], q_gain_t=jnp.broadcast_to(q_gain[l][:, None], (HEAD_DIM, tm)),
        cos=cos, sin=sin, cos_t=cos.T, sin_t=sin.T,
        conv_w=conv_w[l], conv_b=conv_b[l][None, :], lru_w=lru_w,
        lru_ba=lru_ba[l].reshape(2, 1, LRU_WIDTH), lru_bx=lru_bx[l].reshape(2, 1, LRU_WIDTH),
        lru_lam=lru_lam[l].reshape(2, 1, LRU_WIDTH),
        w_out=w_out[l].astype(BF16), g_ffn=g_ffn[l][None, :], w_route=w_route, b_route=b_route,
    )


def _mixer(x, p, cnt0, tm, tk):
    batch, seq_len, _ = x.shape
    x2 = x.reshape(batch * seq_len, D_MODEL)
    u, ggr, k, gate, qt, vt = _inproj(x2, seq_len, p, tm)
    h_fwd = _lru_scan(u, gate, ggr, None, p, 0, batch, seq_len, tm, reverse=False)
    m_a = _lru_scan(u, gate, ggr, h_fwd, p, 1, batch, seq_len, tm, reverse=True)
    m_b = _attention(p["score_bound"], qt, k, vt, gate, batch, seq_len, tm, min(tk, seq_len))
    return _outproj_router(x2, m_a, m_b, cnt0, p, tm)


def _layer(xs_in, l, weights):
    (g_mix, w_in, b_gate, conv_w, conv_b, lru_wa, lru_ba, lru_wx, lru_bx, lru_lam,
     q_gain, k_gain, w_out, g_ffn, w_rg, b_rg, w_re, b_re, w_gate, w_up, w_down) = weights
    tm = ROW_TILE
    max_seq = max(x.shape[1] for x in xs_in)
    p = _prepare_params(l, max_seq, tm, g_mix, w_in, b_gate, conv_w, conv_b, lru_wa, lru_ba, lru_wx, lru_bx,
                        lru_lam, q_gain, k_gain, w_out, g_ffn, w_rg, b_rg, w_re, b_re)

    cnt = jnp.zeros((1, ROUTE_LANES), F32)
    x1s, h2s, routes = [], [], []
    for x in xs_in:
        x1, h2, route, cnt = _mixer(x, p, cnt, tm, KV_CHUNK)
        x1s.append(x1)
        h2s.append(h2)
        routes.append(route)

    n_tok = sum(x1.shape[0] for x1 in x1s)
    n_rows = n_tok * TOP_K + N_EXPERTS * MOE_ROWS
    n_blk = n_rows // MOE_ROWS
    counts = cnt[0, EXPERT_LANE0:EXPERT_LANE0 + N_EXPERTS].astype(jnp.int32)
    padded = (counts + MOE_ROWS - 1) // MOE_ROWS * MOE_ROWS
    pend = jnp.cumsum(padded)
    pstart = pend - padded
    blk_row0 = jnp.arange(n_blk, dtype=jnp.int32) * MOE_ROWS
    blk_e = jnp.minimum(jnp.sum((pend[None, :] <= blk_row0[:, None]).astype(jnp.int32), axis=1), N_EXPERTS - 1)
    n_used = (pend[-1:] // MOE_ROWS).astype(jnp.int32)

    dests = []
    xs = jnp.zeros((n_rows, D_MODEL), F32)
    for h2, route in zip(h2s, routes):
        eid = route[:, 0:TOP_K].astype(jnp.int32)
        rank = route[:, 4:4 + TOP_K].astype(jnp.int32)
        dest = (pstart[eid] + rank).reshape(-1)
        dests.append(dest)
        xs = _dispatch(dest, h2, xs, tm)
    ys = _experts(blk_e, n_used, xs, w_gate[l], w_up[l], w_down[l])
    outs = []
    for x, x1, route, dest in zip(xs_in, x1s, routes, dests):
        outs.append(_combine(dest, x1, route, ys, tm).reshape(x.shape))
    return outs


def kernel(x_prompt, x_sample, g_mix, w_in, b_gate, conv_w, conv_b, lru_wa, lru_ba, lru_wx, lru_bx, lru_lam,
           q_gain, k_gain, w_out, g_ffn, w_rg, b_rg, w_re, b_re, w_gate, w_up, w_down):
    weights = (g_mix, w_in, b_gate, conv_w, conv_b, lru_wa, lru_ba, lru_wx, lru_bx, lru_lam,
               q_gain, k_gain, w_out, g_ffn, w_rg, b_rg, w_re, b_re, w_gate, w_up, w_down)
    xs = [x_prompt, x_sample]
    for l in range(g_mix.shape[0]):
        xs = _layer(xs, l, weights)
    return tuple(xs)
```

```python
import functools
import math

import jax
import jax.numpy as jnp
from jax import lax
from jax.experimental import pallas as pl
from jax.experimental.pallas import tpu as pltpu

F32 = jnp.float32
BF16 = jnp.bfloat16

D_MODEL = 1024
N_HEADS = 8
N_KV_HEADS = 2
GROUP = N_HEADS // N_KV_HEADS
HEAD_DIM = D_MODEL // N_HEADS
AXIS_DIM = HEAD_DIM // 2
N_FREQ = AXIS_DIM // 2
GRID_W = 64
ROPE_THETA = 10000.0
LRU_WIDTH = D_MODEL
LRU_BLOCKS = 8
LRU_BLOCK_W = LRU_WIDTH // LRU_BLOCKS
CONV_W = 4
CONV_PAD_LEFT = 2
LRU_C = 8.0
N_GROUPS = 4
EXPERTS_PER_GROUP = 8
N_EXPERTS = N_GROUPS * EXPERTS_PER_GROUP
TOP_K = 2
D_EXPERT = D_MODEL // 2
ATTN_KV = N_KV_HEADS * HEAD_DIM
EPS = 1e-6

LANES = 128
SUBLANES = 8
VMEM_LIMIT_BYTES = 56 * 1024 * 1024

ROW_TILE = 256
KV_CHUNK = 512
MOE_ROWS = 256
ROW_DMA_UNROLL = 8
ROUTE_LANES = LANES
PACKED_COLS = D_MODEL // 2
EXPERT_LANE0 = N_GROUPS

NAT_COLS = 2 * LRU_WIDTH + ATTN_KV + 2 * D_MODEL
TR_ROWS = D_MODEL + ATTN_KV
LOG2E = math.log2(math.e)
SAFE_EXP2_RANGE = 100.0
SCORE_BOUND_SLACK = 1.01


def _cparams(semantics):
    return pltpu.CompilerParams(dimension_semantics=semantics, vmem_limit_bytes=VMEM_LIMIT_BYTES)


def _const_spec(shape):
    nd = len(shape)
    return pl.BlockSpec(shape, lambda *_: (0,) * nd)


def _sigmoid(x):
    return 0.5 * jnp.tanh(0.5 * x) + 0.5


def _pack_bf16_pairs(x):
    n = x.shape[1] // 2
    lo = lax.bitcast_convert_type(x[:, :n].astype(BF16).astype(F32), jnp.uint32)
    hi = lax.bitcast_convert_type(x[:, n:].astype(BF16).astype(F32), jnp.uint32)
    return (hi & jnp.uint32(0xFFFF0000)) | (lo >> 16)


def _unpack_bf16_pairs(w):
    lo = lax.bitcast_convert_type(w << 16, F32)
    hi = lax.bitcast_convert_type(w & jnp.uint32(0xFFFF0000), F32)
    return jnp.concatenate([lo, hi], axis=1)


def _swap_halves_rows(x):
    return jnp.concatenate([x[N_FREQ:AXIS_DIM], x[0:N_FREQ], x[AXIS_DIM + N_FREQ:], x[AXIS_DIM:AXIS_DIM + N_FREQ]], axis=0)


def _inproj_kernel(x_ref, gmix_ref, wnat_ref, wtr_ref, bgate_ref, kgain_ref, qgain_ref,
                   cos_ref, sin_ref, cost_ref, sint_ref,
                   u_ref, ggr_ref, k_ref, gate_ref, qt_ref, vt_ref):
    tm = x_ref.shape[0]
    x = x_ref[...]
    h = (x * lax.rsqrt(jnp.mean(x * x, axis=-1, keepdims=True) + EPS) * gmix_ref[...]).astype(BF16)

    def nat(lo, hi):
        return jnp.dot(h, wnat_ref[:, lo:hi], preferred_element_type=F32)

    u_ref[...] = nat(0, LRU_WIDTH)
    ggr_ref[...] = jax.nn.gelu(nat(LRU_WIDTH, 2 * LRU_WIDTH))
    k0 = 2 * LRU_WIDTH
    kraw = nat(k0, k0 + ATTN_KV)
    g0 = k0 + ATTN_KV
    gate_ref[...] = _sigmoid(nat(g0, g0 + 2 * D_MODEL) + bgate_ref[...])

    cos = cos_ref[...]
    sin = sin_ref[...]
    lane = lax.broadcasted_iota(jnp.int32, (tm, HEAD_DIM), 1)
    first_half = (lane % AXIS_DIM) < N_FREQ
    for j in range(N_KV_HEADS):
        kj = kraw[:, j * HEAD_DIM:(j + 1) * HEAD_DIM]
        kn = kj * lax.rsqrt(jnp.mean(kj * kj, axis=-1, keepdims=True) + EPS) * kgain_ref[...]
        partner = jnp.where(first_half, pltpu.roll(kn, HEAD_DIM - N_FREQ, 1), pltpu.roll(kn, N_FREQ, 1))
        k_ref[:, j * HEAD_DIM:(j + 1) * HEAD_DIM] = (kn * cos + partner * sin).astype(BF16)

    zt = lax.dot_general(wtr_ref[...], h, (((1,), (1,)), ((), ())), preferred_element_type=F32)
    cost = cost_ref[...]
    sint = sint_ref[...]
    qgain = qgain_ref[...]
    qscale = (HEAD_DIM ** -0.5) * LOG2E
    for hd in range(N_HEADS):
        xq = zt[hd * HEAD_DIM:(hd + 1) * HEAD_DIM, :]
        xn = xq * lax.rsqrt(jnp.mean(xq * xq, axis=0, keepdims=True) + EPS) * qgain
        rot = (xn * cost + _swap_halves_rows(xn) * sint) * qscale
        g = hd % GROUP
        qt_ref[0, hd // GROUP, :, g * tm:(g + 1) * tm] = rot.astype(BF16)
    for j in range(N_KV_HEADS):
        r0 = D_MODEL + j * HEAD_DIM
        vt_ref[j] = zt[r0:r0 + HEAD_DIM, :].astype(BF16)


def _inproj(x2, seq_len, p, tm):
    t = x2.shape[0]
    n_pos = seq_len // tm
    grid = (t // tm,)
    row = lambda i: (i, 0)
    in_specs = [
        pl.BlockSpec((tm, D_MODEL), row),
        _const_spec((1, D_MODEL)),
        _const_spec((D_MODEL, NAT_COLS)),
        _const_spec((TR_ROWS, D_MODEL)),
        _const_spec((1, 2 * D_MODEL)),
        _const_spec((1, HEAD_DIM)),
        _const_spec((HEAD_DIM, tm)),
        pl.BlockSpec((tm, HEAD_DIM), lambda i: (i % n_pos, 0)),
        pl.BlockSpec((tm, HEAD_DIM), lambda i: (i % n_pos, 0)),
        pl.BlockSpec((HEAD_DIM, tm), lambda i: (0, i % n_pos)),
        pl.BlockSpec((HEAD_DIM, tm), lambda i: (0, i % n_pos)),
    ]
    out_shape = (
        jax.ShapeDtypeStruct((t, LRU_WIDTH), F32),
        jax.ShapeDtypeStruct((t, LRU_WIDTH), F32),
        jax.ShapeDtypeStruct((t, ATTN_KV), BF16),
        jax.ShapeDtypeStruct((t, 2 * D_MODEL), F32),
        jax.ShapeDtypeStruct((t // tm, N_KV_HEADS, HEAD_DIM, GROUP * tm), BF16),
        jax.ShapeDtypeStruct((N_KV_HEADS, HEAD_DIM, t), BF16),
    )
    out_specs = (
        pl.BlockSpec((tm, LRU_WIDTH), row),
        pl.BlockSpec((tm, LRU_WIDTH), row),
        pl.BlockSpec((tm, ATTN_KV), row),
        pl.BlockSpec((tm, 2 * D_MODEL), row),
        pl.BlockSpec((1, N_KV_HEADS, HEAD_DIM, GROUP * tm), lambda i: (i, 0, 0, 0)),
        pl.BlockSpec((N_KV_HEADS, HEAD_DIM, tm), lambda i: (0, 0, i)),
    )
    return pl.pallas_call(
        _inproj_kernel, grid=grid, in_specs=in_specs, out_specs=out_specs, out_shape=out_shape,
        compiler_params=_cparams(("arbitrary",)), name="inproj",
    )(x2, p["g_mix"], p["w_nat"], p["w_tr"], p["b_gate"], p["k_gain"], p["q_gain_t"],
      p["cos"], p["sin"], p["cos_t"], p["sin_t"])


def _lru_kernel(*refs, reverse, tm, nt):
    if reverse:
        (up_ref, uc_ref, un_ref, cw_ref, cb_ref, w_ref, ba_ref, bx_ref, lam_ref, hf_ref, ggr_ref, ga_ref,
         out_ref, ubuf, a_s, g_s, carry) = refs
    else:
        (up_ref, uc_ref, un_ref, cw_ref, cb_ref, w_ref, ba_ref, bx_ref, lam_ref,
         out_ref, ubuf, a_s, g_s, carry) = refs
    i = pl.program_id(1)
    ti = (nt - 1 - i) if reverse else i

    @pl.when(i == 0)
    def _():
        carry[...] = jnp.zeros_like(carry)

    ubuf[0:SUBLANES] = jnp.where(ti == 0, 0.0, up_ref[...])
    ubuf[SUBLANES:SUBLANES + tm] = uc_ref[...]
    ubuf[SUBLANES + tm:2 * SUBLANES + tm] = jnp.where(ti == nt - 1, 0.0, un_ref[...])
    cw = cw_ref[...]
    ub = ubuf[...]
    n_buf = tm + 2 * SUBLANES
    xc = cb_ref[...]
    for j in range(CONV_W):
        back = CONV_PAD_LEFT - j
        tap = ub if back == 0 else pltpu.roll(ub, back % n_buf, 0)
        xc = xc + cw[j:j + 1] * tap[SUBLANES:SUBLANES + tm]

    xcb = xc.astype(BF16)
    lam = lam_ref[...]
    half_c = (0.5 * LRU_C) * (jnp.minimum(lam, 0.0) - jnp.log1p(jnp.exp(-jnp.abs(lam))))
    half_ba = 0.5 * ba_ref[...]
    half_bx = 0.5 * bx_ref[...]
    half_xc = 0.5 * xc
    for hb in range(LRU_BLOCKS):
        sl = slice(hb * LRU_BLOCK_W, (hb + 1) * LRU_BLOCK_W)
        gz = jnp.dot(xcb[:, sl], w_ref[hb], preferred_element_type=F32)
        tr = jnp.tanh(gz[:, :LRU_BLOCK_W] + half_ba[:, sl])
        ti = jnp.tanh(gz[:, LRU_BLOCK_W:] + half_bx[:, sl])
        log_a = tr * half_c[:, sl] + half_c[:, sl]
        a = jnp.exp(log_a)
        a_s[:, sl] = a
        y = jnp.tanh(log_a) * (-1.0 - a * a)
        root = jnp.where(y > 0.0, y * lax.rsqrt(y), 0.0)
        g_s[:, sl] = root * (ti * half_xc[:, sl] + half_xc[:, sl])

    n_chunk = tm // SUBLANES
    srow = lax.broadcasted_iota(jnp.int32, (SUBLANES, LRU_WIDTH), 0)

    def chunk(c, h_prev):
        ci = (n_chunk - 1 - c) if reverse else c
        off = pl.multiple_of(ci * SUBLANES, SUBLANES)
        a = a_s[pl.ds(off, SUBLANES), :]
        b = g_s[pl.ds(off, SUBLANES), :]
        for d in (1, 2, 4):
            shift = (SUBLANES - d) if reverse else d
            valid = (srow < SUBLANES - d) if reverse else (srow >= d)
            a_sh = jnp.where(valid, pltpu.roll(a, shift, 0), 1.0)
            b_sh = jnp.where(valid, pltpu.roll(b, shift, 0), 0.0)
            b = a * b_sh + b
            a = a * a_sh
        h = a * h_prev + b
        if reverse:
            rows = pl.ds(off, SUBLANES)
            out_ref[rows, :] = ga_ref[rows, :] * (ggr_ref[rows, :] * (hf_ref[rows, :] + h))
            return h[0:1]
        out_ref[pl.ds(off, SUBLANES), :] = h
        return h[SUBLANES - 1:SUBLANES]

    carry[...] = lax.fori_loop(0, n_chunk, chunk, carry[...])


def _lru_scan(u, gate, ggr, h_fwd, p, d, batch, seq_len, tm, reverse):
    t = u.shape[0]
    nt = seq_len // tm
    per8 = tm // SUBLANES
    n8 = t // SUBLANES

    def tile(b, i):
        return b * nt + ((nt - 1 - i) if reverse else i)

    cur = lambda b, i: (tile(b, i), 0)
    prev = lambda b, i: (jnp.maximum(tile(b, i) * per8 - 1, 0), 0)
    nxt = lambda b, i: (jnp.minimum((tile(b, i) + 1) * per8, n8 - 1), 0)
    const2 = lambda b, i: (0, 0)
    in_specs = [
        pl.BlockSpec((SUBLANES, LRU_WIDTH), prev),
        pl.BlockSpec((tm, LRU_WIDTH), cur),
        pl.BlockSpec((SUBLANES, LRU_WIDTH), nxt),
        pl.BlockSpec((CONV_W, LRU_WIDTH), const2),
        pl.BlockSpec((1, LRU_WIDTH), const2),
        pl.BlockSpec((LRU_BLOCKS, LRU_BLOCK_W, 2 * LRU_BLOCK_W), lambda b, i: (0, 0, 0)),
        pl.BlockSpec((1, LRU_WIDTH), const2),
        pl.BlockSpec((1, LRU_WIDTH), const2),
        pl.BlockSpec((1, LRU_WIDTH), const2),
    ]
    args = [u, u, u, p["conv_w"], p["conv_b"], p["lru_w"][d], p["lru_ba"][d], p["lru_bx"][d], p["lru_lam"][d]]
    if reverse:
        in_specs += [pl.BlockSpec((tm, LRU_WIDTH), cur)] * 3
        args += [h_fwd, ggr, gate]
    return pl.pallas_call(
        functools.partial(_lru_kernel, reverse=reverse, tm=tm, nt=nt),
        grid=(batch, nt), in_specs=in_specs,
        out_specs=pl.BlockSpec((tm, LRU_WIDTH), cur),
        out_shape=jax.ShapeDtypeStruct((t, LRU_WIDTH), F32),
        scratch_shapes=[pltpu.VMEM((tm + 2 * SUBLANES, LRU_WIDTH), F32), pltpu.VMEM((tm, LRU_WIDTH), F32),
                        pltpu.VMEM((tm, LRU_WIDTH), F32), pltpu.VMEM((1, LRU_WIDTH), F32)],
        compiler_params=_cparams(("arbitrary", "arbitrary")), name="lru_bwd" if reverse else "lru_fwd",
    )(*args)


def _attn_kernel(bound_ref, qt_ref, k_ref, vt_ref, gb_ref, o_ref, m_s, l_s, acc_s, *, tq, tk, n_chunks):
    nq_cols = GROUP * tq
    bound = bound_ref[0]
    acc_s[...] = jnp.zeros_like(acc_s)
    l_s[...] = jnp.zeros_like(l_s)

    def kv_chunk(c):
        off = pl.multiple_of(c * tk, tk)
        return k_ref[pl.ds(off, tk), :], vt_ref[0, :, pl.ds(off, tk)]

    @pl.when(2.0 * bound <= SAFE_EXP2_RANGE)
    def _():
        qt = qt_ref[0, 0]

        def chunk(c, _):
            kc, vc = kv_chunk(c)
            pr = jnp.exp2(jnp.dot(kc, qt, preferred_element_type=F32) - bound)
            l_s[...] += jnp.sum(pr.reshape(tk // SUBLANES, SUBLANES, nq_cols), axis=0)
            acc_s[...] += jnp.dot(vc, pr.astype(BF16), preferred_element_type=F32)
            return 0

        lax.fori_loop(0, n_chunks, chunk, 0, unroll=8)

    @pl.when(2.0 * bound > SAFE_EXP2_RANGE)
    def _():
        qt = qt_ref[0, 0]
        m_s[...] = jnp.full_like(m_s, -jnp.inf)

        def chunk(c, _):
            kc, vc = kv_chunk(c)
            s = jnp.dot(kc, qt, preferred_element_type=F32)
            m_old = m_s[...]
            m_new = jnp.maximum(m_old, jnp.max(s, axis=0, keepdims=True))
            alpha = jnp.exp2(m_old - m_new)
            pr = jnp.exp2(s - m_new)
            l_s[...] = alpha * l_s[...] + jnp.sum(pr.reshape(tk // SUBLANES, SUBLANES, nq_cols), axis=0)
            acc_s[...] = alpha * acc_s[...] + jnp.dot(vc, pr.astype(BF16), preferred_element_type=F32)
            m_s[...] = m_new
            return 0

        lax.fori_loop(0, n_chunks, chunk, 0)

    out_t = acc_s[...] / jnp.sum(l_s[...], axis=0, keepdims=True)
    for g in range(GROUP):
        cols = slice(g * HEAD_DIM, (g + 1) * HEAD_DIM)
        o_ref[:, cols] = gb_ref[:, cols] * out_t[:, g * tq:(g + 1) * tq].T


def _attention(bound, qt, k, vt, gate, batch, seq_len, tq, tk):
    t = k.shape[0]
    nq = seq_len // tq
    half = GROUP * HEAD_DIM
    gate_col0 = D_MODEL // half
    return pl.pallas_call(
        functools.partial(_attn_kernel, tq=tq, tk=tk, n_chunks=seq_len // tk),
        grid=(batch, N_KV_HEADS, nq),
        in_specs=[
            pl.BlockSpec(memory_space=pltpu.SMEM),
            pl.BlockSpec((1, 1, HEAD_DIM, GROUP * tq), lambda b, j, i: (b * nq + i, j, 0, 0)),
            pl.BlockSpec((seq_len, HEAD_DIM), lambda b, j, i: (b, j)),
            pl.BlockSpec((1, HEAD_DIM, seq_len), lambda b, j, i: (j, 0, b)),
            pl.BlockSpec((tq, half), lambda b, j, i: (b * nq + i, gate_col0 + j)),
        ],
        out_specs=pl.BlockSpec((tq, half), lambda b, j, i: (b * nq + i, j)),
        out_shape=jax.ShapeDtypeStruct((t, D_MODEL), F32),
        scratch_shapes=[pltpu.VMEM((1, GROUP * tq), F32), pltpu.VMEM((SUBLANES, GROUP * tq), F32),
                        pltpu.VMEM((HEAD_DIM, GROUP * tq), F32)],
        compiler_params=_cparams(("arbitrary", "arbitrary", "arbitrary")), name="attention",
    )(bound, qt, k, vt, gate)


def _outproj_router_kernel(x_ref, ma_ref, mb_ref, wout_ref, gffn_ref, wr_ref, br_ref, cnt0_ref,
                           x1_ref, h2_ref, route_ref, route_t_ref, cnt_ref, cnt_s):
    tm = x_ref.shape[0]
    i = pl.program_id(0)

    @pl.when(i == 0)
    def _():
        cnt_s[...] = cnt0_ref[...]

    merged = (ma_ref[...] + mb_ref[...]).astype(BF16)
    x1 = x_ref[...] + jnp.dot(merged, wout_ref[...], preferred_element_type=F32)
    x1_ref[...] = x1
    h2 = x1 * lax.rsqrt(jnp.mean(x1 * x1, axis=-1, keepdims=True) + EPS) * gffn_ref[...]
    h2_ref[...] = _pack_bf16_pairs(h2)

    hi = h2.astype(BF16)
    lo = (h2 - hi.astype(F32)).astype(BF16)
    z = jnp.dot(jnp.concatenate([hi, hi, lo], axis=1), wr_ref[...], preferred_element_type=F32) + br_ref[...]
    lane = lax.broadcasted_iota(jnp.int32, (tm, ROUTE_LANES), 1)
    neg = -jnp.inf

    def first_argmax(v):
        m = jnp.max(v, axis=-1, keepdims=True)
        return m, jnp.min(jnp.where(v == m, lane, ROUTE_LANES), axis=-1, keepdims=True)

    zg = jnp.where(lane < N_GROUPS, z, neg)
    mg, grp = first_argmax(zg)
    p_sel = 1.0 / jnp.sum(jnp.exp(zg - mg), axis=-1, keepdims=True)
    e_lo = EXPERT_LANE0 + EXPERTS_PER_GROUP * grp
    ze = jnp.where((lane >= e_lo) & (lane < e_lo + EXPERTS_PER_GROUP), z, neg)
    m1, i1 = first_argmax(ze)
    m2, i2 = first_argmax(jnp.where(lane == i1, neg, ze))
    e2 = jnp.exp(m2 - m1)
    w0 = p_sel / (1.0 + e2)
    w1 = p_sel * e2 / (1.0 + e2)

    hot0 = lane == i1
    hot1 = lane == i2
    hot = jnp.where(hot0 | hot1, 1.0, 0.0)
    rr = lax.broadcasted_iota(jnp.int32, (tm, tm), 0)
    cc = lax.broadcasted_iota(jnp.int32, (tm, tm), 1)
    strict_lower = jnp.where(rr > cc, 1.0, 0.0).astype(BF16)
    before = jnp.dot(strict_lower, hot.astype(BF16), preferred_element_type=F32) + cnt_s[...]
    rank0 = jnp.sum(jnp.where(hot0, before, 0.0), axis=-1, keepdims=True)
    rank1 = jnp.sum(jnp.where(hot1, before, 0.0), axis=-1, keepdims=True)
    cnt_s[...] = cnt_s[...] + jnp.sum(hot, axis=0, keepdims=True)
    cnt_ref[...] = cnt_s[...]

    eid0 = (i1 - EXPERT_LANE0).astype(F32)
    eid1 = (i2 - EXPERT_LANE0).astype(F32)
    rec = jnp.zeros((tm, ROUTE_LANES), F32)
    for col, val in enumerate((eid0, eid1, w0, w1, rank0, rank1)):
        rec = jnp.where(lane == col, val, rec)
    route_ref[...] = rec
    route_t_ref[...] = rec.T[0:SUBLANES]


def _outproj_router(x2, m_a, m_b, cnt0, p, tm):
    t = x2.shape[0]
    row = lambda i: (i, 0)
    return pl.pallas_call(
        _outproj_router_kernel, grid=(t // tm,),
        in_specs=[pl.BlockSpec((tm, D_MODEL), row), pl.BlockSpec((tm, D_MODEL), row), pl.BlockSpec((tm, D_MODEL), row),
                  _const_spec((D_MODEL, D_MODEL)), _const_spec((1, D_MODEL)), _const_spec((3 * D_MODEL, ROUTE_LANES)),
                  _const_spec((1, ROUTE_LANES)), _const_spec((1, ROUTE_LANES))],
        out_specs=(pl.BlockSpec((tm, D_MODEL), row), pl.BlockSpec((tm, PACKED_COLS), row),
                   pl.BlockSpec((tm, ROUTE_LANES), row), pl.BlockSpec((SUBLANES, tm), lambda i: (0, i)),
                   _const_spec((1, ROUTE_LANES))),
        out_shape=(jax.ShapeDtypeStruct((t, D_MODEL), F32), jax.ShapeDtypeStruct((t, PACKED_COLS), jnp.uint32),
                   jax.ShapeDtypeStruct((t, ROUTE_LANES), F32), jax.ShapeDtypeStruct((SUBLANES, t), F32),
                   jax.ShapeDtypeStruct((1, ROUTE_LANES), F32)),
        scratch_shapes=[pltpu.VMEM((1, ROUTE_LANES), F32)],
        compiler_params=_cparams(("arbitrary",)), name="outproj_router",
    )(x2, m_a, m_b, p["w_out"], p["g_ffn"], p["w_route"], p["b_route"], cnt0)


def _dispatch_kernel(dest_ref, h_ref, xs_in_ref, xs_ref, sem, *, tm, n_tok):
    del xs_in_ref
    base = pl.program_id(0) * tm

    def issue(r, _):
        for k in range(TOP_K):
            d = dest_ref[k * n_tok + base + r]
            pltpu.make_async_copy(h_ref.at[pl.ds(r, 1), :], xs_ref.at[pl.ds(d, 1), :], sem).start()
        return 0

    lax.fori_loop(0, tm, issue, 0, unroll=ROW_DMA_UNROLL)
    for k in range(TOP_K):
        pltpu.make_async_copy(h_ref, xs_ref.at[pl.ds(0, tm), :], sem).wait()


def _dispatch(dest_flat, h2, xs, tm):
    t = h2.shape[0]
    return pl.pallas_call(
        functools.partial(_dispatch_kernel, tm=tm, n_tok=t),
        grid_spec=pltpu.PrefetchScalarGridSpec(
            num_scalar_prefetch=1, grid=(t // tm,),
            in_specs=[pl.BlockSpec((tm, PACKED_COLS), lambda i, d: (i, 0)), pl.BlockSpec(memory_space=pl.ANY)],
            out_specs=pl.BlockSpec(memory_space=pl.ANY),
            scratch_shapes=[pltpu.SemaphoreType.DMA]),
        out_shape=jax.ShapeDtypeStruct(xs.shape, xs.dtype),
        input_output_aliases={2: 0},
        compiler_params=_cparams(("arbitrary",)), name="moe_dispatch",
    )(dest_flat, h2, xs)


def _experts_kernel(blk_e_ref, n_used_ref, x_ref, wg_ref, wu_ref, wd_ref, y_ref, wg_s, wu_s, wd_s):
    i = pl.program_id(0)
    changed = jnp.logical_or(i == 0, blk_e_ref[i] != blk_e_ref[jnp.maximum(i - 1, 0)])

    @pl.when(changed)
    def _():
        wg_s[...] = wg_ref[0].astype(BF16)
        wu_s[...] = wu_ref[0].astype(BF16)
        wd_s[...] = wd_ref[0].astype(BF16)

    @pl.when(i < n_used_ref[0])
    def _():
        xb = _unpack_bf16_pairs(x_ref[...]).astype(BF16)
        gate = jnp.dot(xb, wg_s[...], preferred_element_type=F32)
        up = jnp.dot(xb, wu_s[...], preferred_element_type=F32)
        act = (jax.nn.silu(gate) * up).astype(BF16)
        y_ref[...] = _pack_bf16_pairs(jnp.dot(act, wd_s[...], preferred_element_type=F32))

    @pl.when(i >= n_used_ref[0])
    def _():
        y_ref[...] = jnp.zeros_like(y_ref)


def _experts(blk_e, n_used, xs, w_gate, w_up, w_down):
    n_rows = xs.shape[0]
    rb = MOE_ROWS
    wmap = lambda i, be, nu: (be[i], 0, 0)
    return pl.pallas_call(
        _experts_kernel,
        grid_spec=pltpu.PrefetchScalarGridSpec(
            num_scalar_prefetch=2, grid=(n_rows // rb,),
            in_specs=[pl.BlockSpec((rb, PACKED_COLS), lambda i, be, nu: (i, 0)),
                      pl.BlockSpec((1, D_MODEL, D_EXPERT), wmap),
                      pl.BlockSpec((1, D_MODEL, D_EXPERT), wmap),
                      pl.BlockSpec((1, D_EXPERT, D_MODEL), wmap)],
            out_specs=pl.BlockSpec((rb, PACKED_COLS), lambda i, be, nu: (i, 0)),
            scratch_shapes=[pltpu.VMEM((D_MODEL, D_EXPERT), BF16), pltpu.VMEM((D_MODEL, D_EXPERT), BF16),
                            pltpu.VMEM((D_EXPERT, D_MODEL), BF16)]),
        out_shape=jax.ShapeDtypeStruct((n_rows, PACKED_COLS), jnp.uint32),
        compiler_params=_cparams(("arbitrary",)), name="moe_experts",
    )(blk_e, n_used, xs, w_gate, w_up, w_down)


def _combine_kernel(dest_ref, x1_ref, route_ref, ys_ref, o_ref, gbuf, sem, *, tm, n_tok):
    base = pl.program_id(0) * tm

    def issue(r, _):
        for k in range(TOP_K):
            d = dest_ref[k * n_tok + base + r]
            pltpu.make_async_copy(ys_ref.at[pl.ds(d, 1), :], gbuf.at[k, pl.ds(r, 1), :], sem).start()
        return 0

    lax.fori_loop(0, tm, issue, 0, unroll=ROW_DMA_UNROLL)
    for k in range(TOP_K):
        pltpu.make_async_copy(ys_ref.at[pl.ds(0, tm), :], gbuf.at[k], sem).wait()
    w0 = route_ref[:, 2:3]
    w1 = route_ref[:, 3:4]
    o_ref[...] = x1_ref[...] + (_unpack_bf16_pairs(gbuf[0]) * w0 + _unpack_bf16_pairs(gbuf[1]) * w1)


def _combine(dest_flat, x1, route, ys, tm):
    t = x1.shape[0]
    return pl.pallas_call(
        functools.partial(_combine_kernel, tm=tm, n_tok=t),
        grid_spec=pltpu.PrefetchScalarGridSpec(
            num_scalar_prefetch=1, grid=(t // tm,),
            in_specs=[pl.BlockSpec((tm, D_MODEL), lambda i, d: (i, 0)),
                      pl.BlockSpec((tm, ROUTE_LANES), lambda i, d: (i, 0)),
                      pl.BlockSpec(memory_space=pl.ANY)],
            out_specs=pl.BlockSpec((tm, D_MODEL), lambda i, d: (i, 0)),
            scratch_shapes=[pltpu.VMEM((TOP_K, tm, PACKED_COLS), jnp.uint32), pltpu.SemaphoreType.DMA]),
        out_shape=jax.ShapeDtypeStruct((t, D_MODEL), F32),
        compiler_params=_cparams(("arbitrary",)), name="moe_combine",
    )(dest_flat, x1, route, ys)


def _rope_tables(seq_len):
    inv = ROPE_THETA ** (-jnp.arange(0, AXIS_DIM, 2, dtype=F32) / AXIS_DIM)
    n_grid_rows = seq_len // GRID_W
    ang_r = jnp.arange(n_grid_rows, dtype=F32)[:, None] * inv
    ang_c = jnp.arange(GRID_W, dtype=F32)[:, None] * inv
    by_row = lambda a: jnp.repeat(a, GRID_W, axis=0)
    by_col = lambda a: jnp.tile(a, (n_grid_rows, 1))
    cos_r, sin_r, cos_c, sin_c = by_row(jnp.cos(ang_r)), by_row(jnp.sin(ang_r)), by_col(jnp.cos(ang_c)), by_col(jnp.sin(ang_c))
    cos = jnp.concatenate([cos_r, cos_r, cos_c, cos_c], axis=-1)
    sin = jnp.concatenate([-sin_r, sin_r, -sin_c, sin_c], axis=-1)
    return cos, sin


def _prepare_params(l, max_seq, tm, g_mix, w_in, b_gate, conv_w, conv_b, lru_wa, lru_ba, lru_wx, lru_bx, lru_lam,
                    q_gain, k_gain, w_out, g_ffn, w_rg, b_rg, w_re, b_re):
    c_u, c_gr, c_q, c_k, c_v = (LRU_WIDTH, 2 * LRU_WIDTH, 2 * LRU_WIDTH + D_MODEL,
                                2 * LRU_WIDTH + D_MODEL + ATTN_KV, 2 * LRU_WIDTH + D_MODEL + 2 * ATTN_KV)
    w = w_in[l]
    w_nat = jnp.concatenate([w[:, :c_gr], w[:, c_q:c_k], w[:, c_v:]], axis=1).astype(BF16)
    w_tr = jnp.concatenate([w[:, c_gr:c_q], w[:, c_k:c_v]], axis=1).T.astype(BF16)
    cos, sin = _rope_tables(max_seq)
    pad = ROUTE_LANES - N_GROUPS - N_EXPERTS
    w_route = jnp.concatenate([w_rg[l], w_re[l], jnp.zeros((D_MODEL, pad), F32)], axis=1)
    w_route_hi = w_route.astype(BF16)
    w_route_lo = (w_route - w_route_hi.astype(F32)).astype(BF16)
    w_route = jnp.concatenate([w_route_hi, w_route_lo, w_route_hi], axis=0)
    b_route = jnp.concatenate([b_rg[l], b_re[l], jnp.zeros((pad,), F32)])[None, :]
    lru_w = (0.5 * jnp.concatenate([lru_wa[l], lru_wx[l]], axis=-1)).astype(BF16)
    score_bound = (SCORE_BOUND_SLACK * LOG2E * math.sqrt(HEAD_DIM)
                   * jnp.max(jnp.abs(q_gain[l])) * jnp.max(jnp.abs(k_gain[l]))).reshape(1)
    return dict(
        score_bound=score_bound,
        g_mix=g_mix[l][None, :], w_nat=w_nat, w_tr=w_tr, b_gate=b_gate[l][None, :],
        k_gain=k_gain[l][None, :], q_gain_t=jnp.broadcast_to(q_gain[l][:, None], (HEAD_DIM, tm)),
        cos=cos, sin=sin, cos_t=cos.T, sin_t=sin.T,
        conv_w=conv_w[l], conv_b=conv_b[l][None, :], lru_w=lru_w,
        lru_ba=lru_ba[l].reshape(2, 1, LRU_WIDTH), lru_bx=lru_bx[l].reshape(2, 1, LRU_WIDTH),
        lru_lam=lru_lam[l].reshape(2, 1, LRU_WIDTH),
        w_out=w_out[l].astype(BF16), g_ffn=g_ffn[l][None, :], w_route=w_route, b_route=b_route,
    )


def _mixer(x, p, cnt0, tm, tk):
    batch, seq_len, _ = x.shape
    x2 = x.reshape(batch * seq_len, D_MODEL)
    u, ggr, k, gate, qt, vt = _inproj(x2, seq_len, p, tm)
    h_fwd = _lru_scan(u, gate, ggr, None, p, 0, batch, seq_len, tm, reverse=False)
    m_a = _lru_scan(u, gate, ggr, h_fwd, p, 1, batch, seq_len, tm, reverse=True)
    m_b = _attention(p["score_bound"], qt, k, vt, gate, batch, seq_len, tm, min(tk, seq_len))
    return _outproj_router(x2, m_a, m_b, cnt0, p, tm)


def _layer(xs_in, l, weights):
    (g_mix, w_in, b_gate, conv_w, conv_b, lru_wa, lru_ba, lru_wx, lru_bx, lru_lam,
     q_gain, k_gain, w_out, g_ffn, w_rg, b_rg, w_re, b_re, w_gate, w_up, w_down) = weights
    tm = ROW_TILE
    max_seq = max(x.shape[1] for x in xs_in)
    p = _prepare_params(l, max_seq, tm, g_mix, w_in, b_gate, conv_w, conv_b, lru_wa, lru_ba, lru_wx, lru_bx,
                        lru_lam, q_gain, k_gain, w_out, g_ffn, w_rg, b_rg, w_re, b_re)

    cnt = jnp.zeros((1, ROUTE_LANES), F32)
    x1s, h2s, routes, routes_t = [], [], [], []
    for x in xs_in:
        x1, h2, route, route_t, cnt = _mixer(x, p, cnt, tm, KV_CHUNK)
        x1s.append(x1)
        h2s.append(h2)
        routes.append(route)
        routes_t.append(route_t)

    n_tok = sum(x1.shape[0] for x1 in x1s)
    n_rows = n_tok * TOP_K + N_EXPERTS * MOE_ROWS
    n_blk = n_rows // MOE_ROWS
    counts = cnt[0, EXPERT_LANE0:EXPERT_LANE0 + N_EXPERTS].astype(jnp.int32)
    padded = (counts + MOE_ROWS - 1) // MOE_ROWS * MOE_ROWS
    pend = jnp.cumsum(padded)
    pstart = pend - padded
    blk_row0 = jnp.arange(n_blk, dtype=jnp.int32) * MOE_ROWS
    blk_e = jnp.minimum(jnp.sum((pend[None, :] <= blk_row0[:, None]).astype(jnp.int32), axis=1), N_EXPERTS - 1)
    n_used = (pend[-1:] // MOE_ROWS).astype(jnp.int32)

    dests = []
    xs = jnp.zeros((n_rows, PACKED_COLS), jnp.uint32)
    for h2, route_t in zip(h2s, routes_t):
        eid = route_t[0:TOP_K].astype(jnp.int32)
        rank = route_t[4:4 + TOP_K].astype(jnp.int32)
        dest = (pstart[eid] + rank).reshape(-1)
        dests.append(dest)
        xs = _dispatch(dest, h2, xs, tm)
    ys = _experts(blk_e, n_used, xs, w_gate[l], w_up[l], w_down[l])
    outs = []
    for x, x1, route, dest in zip(xs_in, x1s, routes, dests):
        outs.append(_combine(dest, x1, route, ys, tm).reshape(x.shape))
    return outs


def kernel(x_prompt, x_sample, g_mix, w_in, b_gate, conv_w, conv_b, lru_wa, lru_ba, lru_wx, lru_bx, lru_lam,
           q_gain, k_gain, w_out, g_ffn, w_rg, b_rg, w_re, b_re, w_gate, w_up, w_down):
    weights = (g_mix, w_in, b_gate, conv_w, conv_b, lru_wa, lru_ba, lru_wx, lru_bx, lru_lam,
               q_gain, k_gain, w_out, g_ffn, w_rg, b_rg, w_re, b_re, w_gate, w_up, w_down)
    xs = [x_prompt, x_sample]
    for l in range(g_mix.shape[0]):
        xs = _layer(xs, l, weights)
    return tuple(xs)
```

```python
import functools
import math

import jax
import jax.numpy as jnp
from jax import lax
from jax.experimental import pallas as pl
from jax.experimental.pallas import tpu as pltpu

F32 = jnp.float32
BF16 = jnp.bfloat16

D_MODEL = 1024
N_HEADS = 8
N_KV_HEADS = 2
GROUP = N_HEADS // N_KV_HEADS
HEAD_DIM = D_MODEL // N_HEADS
AXIS_DIM = HEAD_DIM // 2
N_FREQ = AXIS_DIM // 2
GRID_W = 64
ROPE_THETA = 10000.0
LRU_WIDTH = D_MODEL
LRU_BLOCKS = 8
LRU_BLOCK_W = LRU_WIDTH // LRU_BLOCKS
CONV_W = 4
CONV_PAD_LEFT = 2
LRU_C = 8.0
N_GROUPS = 4
EXPERTS_PER_GROUP = 8
N_EXPERTS = N_GROUPS * EXPERTS_PER_GROUP
TOP_K = 2
D_EXPERT = D_MODEL // 2
ATTN_KV = N_KV_HEADS * HEAD_DIM
EPS = 1e-6

LANES = 128
SUBLANES = 8
VMEM_LIMIT_BYTES = 56 * 1024 * 1024

ROW_TILE = 256
KV_CHUNK = 512
MOE_ROWS = 256
ROW_DMA_UNROLL = 8
ROUTE_LANES = LANES
PACKED_COLS = D_MODEL // 2
EXPERT_LANE0 = N_GROUPS

NAT_COLS = 2 * LRU_WIDTH + ATTN_KV + 2 * D_MODEL
TR_ROWS = D_MODEL + ATTN_KV
LOG2E = math.log2(math.e)
SAFE_EXP2_RANGE = 100.0
SCORE_BOUND_SLACK = 1.01


def _cparams(semantics):
    return pltpu.CompilerParams(dimension_semantics=semantics, vmem_limit_bytes=VMEM_LIMIT_BYTES)


def _const_spec(shape):
    nd = len(shape)
    return pl.BlockSpec(shape, lambda *_: (0,) * nd)


def _sigmoid(x):
    return 0.5 * jnp.tanh(0.5 * x) + 0.5


def _pack_bf16_pairs(x):
    n = x.shape[1] // 2
    lo = lax.bitcast_convert_type(x[:, :n].astype(BF16).astype(F32), jnp.uint32)
    hi = lax.bitcast_convert_type(x[:, n:].astype(BF16).astype(F32), jnp.uint32)
    return (hi & jnp.uint32(0xFFFF0000)) | (lo >> 16)


def _unpack_bf16_pairs(w):
    lo = lax.bitcast_convert_type(w << 16, F32)
    hi = lax.bitcast_convert_type(w & jnp.uint32(0xFFFF0000), F32)
    return jnp.concatenate([lo, hi], axis=1)


def _swap_halves_rows(x):
    return jnp.concatenate([x[N_FREQ:AXIS_DIM], x[0:N_FREQ], x[AXIS_DIM + N_FREQ:], x[AXIS_DIM:AXIS_DIM + N_FREQ]], axis=0)


def _inproj_kernel(x_ref, gmix_ref, wnat_ref, wtr_ref, bgate_ref, kgain_ref, qgain_ref,
                   cos_ref, sin_ref, cost_ref, sint_ref,
                   u_ref, ggr_ref, k_ref, gate_ref, qt_ref, vt_ref):
    tm = x_ref.shape[0]
    x = x_ref[...]
    h = (x * lax.rsqrt(jnp.mean(x * x, axis=-1, keepdims=True) + EPS) * gmix_ref[...]).astype(BF16)

    def nat(lo, hi):
        return jnp.dot(h, wnat_ref[:, lo:hi], preferred_element_type=F32)

    u_ref[...] = nat(0, LRU_WIDTH)
    ggr_ref[...] = jax.nn.gelu(nat(LRU_WIDTH, 2 * LRU_WIDTH))
    k0 = 2 * LRU_WIDTH
    kraw = nat(k0, k0 + ATTN_KV)
    g0 = k0 + ATTN_KV
    gate_ref[...] = _sigmoid(nat(g0, g0 + 2 * D_MODEL) + bgate_ref[...])

    cos = cos_ref[...]
    sin = sin_ref[...]
    lane = lax.broadcasted_iota(jnp.int32, (tm, HEAD_DIM), 1)
    first_half = (lane % AXIS_DIM) < N_FREQ
    for j in range(N_KV_HEADS):
        kj = kraw[:, j * HEAD_DIM:(j + 1) * HEAD_DIM]
        kn = kj * lax.rsqrt(jnp.mean(kj * kj, axis=-1, keepdims=True) + EPS) * kgain_ref[...]
        partner = jnp.where(first_half, pltpu.roll(kn, HEAD_DIM - N_FREQ, 1), pltpu.roll(kn, N_FREQ, 1))
        k_ref[:, j * HEAD_DIM:(j + 1) * HEAD_DIM] = (kn * cos + partner * sin).astype(BF16)

    zt = lax.dot_general(wtr_ref[...], h, (((1,), (1,)), ((), ())), preferred_element_type=F32)
    cost = cost_ref[...]
    sint = sint_ref[...]
    qgain = qgain_ref[...]
    qscale = (HEAD_DIM ** -0.5) * LOG2E
    for hd in range(N_HEADS):
        xq = zt[hd * HEAD_DIM:(hd + 1) * HEAD_DIM, :]
        xn = xq * lax.rsqrt(jnp.mean(xq * xq, axis=0, keepdims=True) + EPS) * qgain
        rot = (xn * cost + _swap_halves_rows(xn) * sint) * qscale
        g = hd % GROUP
        qt_ref[0, hd // GROUP, :, g * tm:(g + 1) * tm] = rot.astype(BF16)
    for j in range(N_KV_HEADS):
        r0 = D_MODEL + j * HEAD_DIM
        vt_ref[j] = zt[r0:r0 + HEAD_DIM, :].astype(BF16)


def _inproj(x2, seq_len, p, tm):
    t = x2.shape[0]
    n_pos = seq_len // tm
    grid = (t // tm,)
    row = lambda i: (i, 0)
    in_specs = [
        pl.BlockSpec((tm, D_MODEL), row),
        _const_spec((1, D_MODEL)),
        _const_spec((D_MODEL, NAT_COLS)),
        _const_spec((TR_ROWS, D_MODEL)),
        _const_spec((1, 2 * D_MODEL)),
        _const_spec((1, HEAD_DIM)),
        _const_spec((HEAD_DIM, tm)),
        pl.BlockSpec((tm, HEAD_DIM), lambda i: (i % n_pos, 0)),
        pl.BlockSpec((tm, HEAD_DIM), lambda i: (i % n_pos, 0)),
        pl.BlockSpec((HEAD_DIM, tm), lambda i: (0, i % n_pos)),
        pl.BlockSpec((HEAD_DIM, tm), lambda i: (0, i % n_pos)),
    ]
    out_shape = (
        jax.ShapeDtypeStruct((t, LRU_WIDTH), F32),
        jax.ShapeDtypeStruct((t, LRU_WIDTH), F32),
        jax.ShapeDtypeStruct((t, ATTN_KV), BF16),
        jax.ShapeDtypeStruct((t, 2 * D_MODEL), F32),
        jax.ShapeDtypeStruct((t // tm, N_KV_HEADS, HEAD_DIM, GROUP * tm), BF16),
        jax.ShapeDtypeStruct((N_KV_HEADS, HEAD_DIM, t), BF16),
    )
    out_specs = (
        pl.BlockSpec((tm, LRU_WIDTH), row),
        pl.BlockSpec((tm, LRU_WIDTH), row),
        pl.BlockSpec((tm, ATTN_KV), row),
        pl.BlockSpec((tm, 2 * D_MODEL), row),
        pl.BlockSpec((1, N_KV_HEADS, HEAD_DIM, GROUP * tm), lambda i: (i, 0, 0, 0)),
        pl.BlockSpec((N_KV_HEADS, HEAD_DIM, tm), lambda i: (0, 0, i)),
    )
    return pl.pallas_call(
        _inproj_kernel, grid=grid, in_specs=in_specs, out_specs=out_specs, out_shape=out_shape,
        compiler_params=_cparams(("arbitrary",)), name="inproj",
    )(x2, p["g_mix"], p["w_nat"], p["w_tr"], p["b_gate"], p["k_gain"], p["q_gain_t"],
      p["cos"], p["sin"], p["cos_t"], p["sin_t"])


def _lru_kernel(*refs, reverse, tm, nt):
    if reverse:
        (up_ref, uc_ref, un_ref, cw_ref, cb_ref, w_ref, ba_ref, bx_ref, lam_ref, hf_ref, ggr_ref, ga_ref,
         out_ref, ubuf, a_s, g_s, carry) = refs
    else:
        (up_ref, uc_ref, un_ref, cw_ref, cb_ref, w_ref, ba_ref, bx_ref, lam_ref,
         out_ref, ubuf, a_s, g_s, carry) = refs
    i = pl.program_id(1)
    ti = (nt - 1 - i) if reverse else i

    @pl.when(i == 0)
    def _():
        carry[...] = jnp.zeros_like(carry)

    ubuf[0:SUBLANES] = jnp.where(ti == 0, 0.0, up_ref[...])
    ubuf[SUBLANES:SUBLANES + tm] = uc_ref[...]
    ubuf[SUBLANES + tm:2 * SUBLANES + tm] = jnp.where(ti == nt - 1, 0.0, un_ref[...])
    cw = cw_ref[...]
    ub = ubuf[...]
    n_buf = tm + 2 * SUBLANES
    xc = cb_ref[...]
    for j in range(CONV_W):
        back = CONV_PAD_LEFT - j
        tap = ub if back == 0 else pltpu.roll(ub, back % n_buf, 0)
        xc = xc + cw[j:j + 1] * tap[SUBLANES:SUBLANES + tm]

    xcb = xc.astype(BF16)
    lam = lam_ref[...]
    half_c = (0.5 * LRU_C) * (jnp.minimum(lam, 0.0) - jnp.log1p(jnp.exp(-jnp.abs(lam))))
    half_ba = 0.5 * ba_ref[...]
    half_bx = 0.5 * bx_ref[...]
    half_xc = 0.5 * xc
    for hb in range(LRU_BLOCKS):
        sl = slice(hb * LRU_BLOCK_W, (hb + 1) * LRU_BLOCK_W)
        gz = jnp.dot(xcb[:, sl], w_ref[hb], preferred_element_type=F32)
        tr = jnp.tanh(gz[:, :LRU_BLOCK_W] + half_ba[:, sl])
        ti = jnp.tanh(gz[:, LRU_BLOCK_W:] + half_bx[:, sl])
        log_a = tr * half_c[:, sl] + half_c[:, sl]
        a = jnp.exp(log_a)
        a_s[:, sl] = a
        y = jnp.tanh(log_a) * (-1.0 - a * a)
        root = jnp.where(y > 0.0, y * lax.rsqrt(y), 0.0)
        g_s[:, sl] = root * (ti * half_xc[:, sl] + half_xc[:, sl])

    n_chunk = tm // SUBLANES
    srow = lax.broadcasted_iota(jnp.int32, (SUBLANES, LRU_WIDTH), 0)

    def chunk(c, h_prev):
        ci = (n_chunk - 1 - c) if reverse else c
        off = pl.multiple_of(ci * SUBLANES, SUBLANES)
        a = a_s[pl.ds(off, SUBLANES), :]
        b = g_s[pl.ds(off, SUBLANES), :]
        for d in (1, 2, 4):
            shift = (SUBLANES - d) if reverse else d
            valid = (srow < SUBLANES - d) if reverse else (srow >= d)
            a_sh = jnp.where(valid, pltpu.roll(a, shift, 0), 1.0)
            b_sh = jnp.where(valid, pltpu.roll(b, shift, 0), 0.0)
            b = a * b_sh + b
            a = a * a_sh
        h = a * h_prev + b
        if reverse:
            rows = pl.ds(off, SUBLANES)
            out_ref[rows, :] = ga_ref[rows, :] * (ggr_ref[rows, :] * (hf_ref[rows, :] + h))
            return h[0:1]
        out_ref[pl.ds(off, SUBLANES), :] = h
        return h[SUBLANES - 1:SUBLANES]

    carry[...] = lax.fori_loop(0, n_chunk, chunk, carry[...])


def _lru_scan(u, gate, ggr, h_fwd, p, d, batch, seq_len, tm, reverse):
    t = u.shape[0]
    nt = seq_len // tm
    per8 = tm // SUBLANES
    n8 = t // SUBLANES

    def tile(b, i):
        return b * nt + ((nt - 1 - i) if reverse else i)

    cur = lambda b, i: (tile(b, i), 0)
    prev = lambda b, i: (jnp.maximum(tile(b, i) * per8 - 1, 0), 0)
    nxt = lambda b, i: (jnp.minimum((tile(b, i) + 1) * per8, n8 - 1), 0)
    const2 = lambda b, i: (0, 0)
    in_specs = [
        pl.BlockSpec((SUBLANES, LRU_WIDTH), prev),
        pl.BlockSpec((tm, LRU_WIDTH), cur),
        pl.BlockSpec((SUBLANES, LRU_WIDTH), nxt),
        pl.BlockSpec((CONV_W, LRU_WIDTH), const2),
        pl.BlockSpec((1, LRU_WIDTH), const2),
        pl.BlockSpec((LRU_BLOCKS, LRU_BLOCK_W, 2 * LRU_BLOCK_W), lambda b, i: (0, 0, 0)),
        pl.BlockSpec((1, LRU_WIDTH), const2),
        pl.BlockSpec((1, LRU_WIDTH), const2),
        pl.BlockSpec((1, LRU_WIDTH), const2),
    ]
    args = [u, u, u, p["conv_w"], p["conv_b"], p["lru_w"][d], p["lru_ba"][d], p["lru_bx"][d], p["lru_lam"][d]]
    if reverse:
        in_specs += [pl.BlockSpec((tm, LRU_WIDTH), cur)] * 3
        args += [h_fwd, ggr, gate]
    return pl.pallas_call(
        functools.partial(_lru_kernel, reverse=reverse, tm=tm, nt=nt),
        grid=(batch, nt), in_specs=in_specs,
        out_specs=pl.BlockSpec((tm, LRU_WIDTH), cur),
        out_shape=jax.ShapeDtypeStruct((t, LRU_WIDTH), F32),
        scratch_shapes=[pltpu.VMEM((tm + 2 * SUBLANES, LRU_WIDTH), F32), pltpu.VMEM((tm, LRU_WIDTH), F32),
                        pltpu.VMEM((tm, LRU_WIDTH), F32), pltpu.VMEM((1, LRU_WIDTH), F32)],
        compiler_params=_cparams(("arbitrary", "arbitrary")), name="lru_bwd" if reverse else "lru_fwd",
    )(*args)


def _attn_kernel(bound_ref, qt_ref, k_ref, vt_ref, gb_ref, o_ref, m_s, l_s, acc_s, *, tq, tk, n_chunks):
    nq_cols = GROUP * tq
    bound = bound_ref[0]
    acc_s[...] = jnp.zeros_like(acc_s)
    l_s[...] = jnp.zeros_like(l_s)

    def kv_chunk(c):
        off = pl.multiple_of(c * tk, tk)
        return k_ref[pl.ds(off, tk), :], vt_ref[0, :, pl.ds(off, tk)]

    @pl.when(2.0 * bound <= SAFE_EXP2_RANGE)
    def _():
        qt = qt_ref[0, 0]

        def chunk(c, _):
            kc, vc = kv_chunk(c)
            pr = jnp.exp2(jnp.dot(kc, qt, preferred_element_type=F32) - bound)
            l_s[...] += jnp.sum(pr.reshape(tk // SUBLANES, SUBLANES, nq_cols), axis=0)
            acc_s[...] += jnp.dot(vc, pr.astype(BF16), preferred_element_type=F32)
            return 0

        lax.fori_loop(0, n_chunks, chunk, 0, unroll=8)

    @pl.when(2.0 * bound > SAFE_EXP2_RANGE)
    def _():
        qt = qt_ref[0, 0]
        m_s[...] = jnp.full_like(m_s, -jnp.inf)

        def chunk(c, _):
            kc, vc = kv_chunk(c)
            s = jnp.dot(kc, qt, preferred_element_type=F32)
            m_old = m_s[...]
            m_new = jnp.maximum(m_old, jnp.max(s, axis=0, keepdims=True))
            alpha = jnp.exp2(m_old - m_new)
            pr = jnp.exp2(s - m_new)
            l_s[...] = alpha * l_s[...] + jnp.sum(pr.reshape(tk // SUBLANES, SUBLANES, nq_cols), axis=0)
            acc_s[...] = alpha * acc_s[...] + jnp.dot(vc, pr.astype(BF16), preferred_element_type=F32)
            m_s[...] = m_new
            return 0

        lax.fori_loop(0, n_chunks, chunk, 0)

    out_t = acc_s[...] / jnp.sum(l_s[...], axis=0, keepdims=True)
    for g in range(GROUP):
        cols = slice(g * HEAD_DIM, (g + 1) * HEAD_DIM)
        o_ref[:, cols] = gb_ref[:, cols] * out_t[:, g * tq:(g + 1) * tq].T


def _attention(bound, qt, k, vt, gate, batch, seq_len, tq, tk):
    t = k.shape[0]
    nq = seq_len // tq
    half = GROUP * HEAD_DIM
    gate_col0 = D_MODEL // half
    return pl.pallas_call(
        functools.partial(_attn_kernel, tq=tq, tk=tk, n_chunks=seq_len // tk),
        grid=(batch, N_KV_HEADS, nq),
        in_specs=[
            pl.BlockSpec(memory_space=pltpu.SMEM),
            pl.BlockSpec((1, 1, HEAD_DIM, GROUP * tq), lambda b, j, i: (b * nq + i, j, 0, 0)),
            pl.BlockSpec((seq_len, HEAD_DIM), lambda b, j, i: (b, j)),
            pl.BlockSpec((1, HEAD_DIM, seq_len), lambda b, j, i: (j, 0, b)),
            pl.BlockSpec((tq, half), lambda b, j, i: (b * nq + i, gate_col0 + j)),
        ],
        out_specs=pl.BlockSpec((tq, half), lambda b, j, i: (b * nq + i, j)),
        out_shape=jax.ShapeDtypeStruct((t, D_MODEL), F32),
        scratch_shapes=[pltpu.VMEM((1, GROUP * tq), F32), pltpu.VMEM((SUBLANES, GROUP * tq), F32),
                        pltpu.VMEM((HEAD_DIM, GROUP * tq), F32)],
        compiler_params=_cparams(("arbitrary", "arbitrary", "arbitrary")), name="attention",
    )(bound, qt, k, vt, gate)


def _outproj_router_kernel(x_ref, ma_ref, mb_ref, wout_ref, gffn_ref, wr_ref, br_ref, cnt0_ref,
                           x1_ref, h2_ref, route_ref, route_t_ref, cnt_ref, cnt_s):
    tm = x_ref.shape[0]
    i = pl.program_id(0)

    @pl.when(i == 0)
    def _():
        cnt_s[...] = cnt0_ref[...]

    merged = (ma_ref[...] + mb_ref[...]).astype(BF16)
    x1 = x_ref[...] + jnp.dot(merged, wout_ref[...], preferred_element_type=F32)
    x1_ref[...] = x1
    h2 = x1 * lax.rsqrt(jnp.mean(x1 * x1, axis=-1, keepdims=True) + EPS) * gffn_ref[...]
    h2_ref[...] = _pack_bf16_pairs(h2)

    hi = h2.astype(BF16)
    lo = (h2 - hi.astype(F32)).astype(BF16)
    z = jnp.dot(jnp.concatenate([hi, hi, lo], axis=1), wr_ref[...], preferred_element_type=F32) + br_ref[...]
    lane = lax.broadcasted_iota(jnp.int32, (tm, ROUTE_LANES), 1)
    neg = -jnp.inf

    def first_argmax(v):
        m = jnp.max(v, axis=-1, keepdims=True)
        return m, jnp.min(jnp.where(v == m, lane, ROUTE_LANES), axis=-1, keepdims=True)

    zg = jnp.where(lane < N_GROUPS, z, neg)
    mg, grp = first_argmax(zg)
    p_sel = 1.0 / jnp.sum(jnp.exp(zg - mg), axis=-1, keepdims=True)
    e_lo = EXPERT_LANE0 + EXPERTS_PER_GROUP * grp
    ze = jnp.where((lane >= e_lo) & (lane < e_lo + EXPERTS_PER_GROUP), z, neg)
    m1, i1 = first_argmax(ze)
    m2, i2 = first_argmax(jnp.where(lane == i1, neg, ze))
    e2 = jnp.exp(m2 - m1)
    w0 = p_sel / (1.0 + e2)
    w1 = p_sel * e2 / (1.0 + e2)

    hot0 = lane == i1
    hot1 = lane == i2
    hot = jnp.where(hot0 | hot1, 1.0, 0.0)
    rr = lax.broadcasted_iota(jnp.int32, (tm, tm), 0)
    cc = lax.broadcasted_iota(jnp.int32, (tm, tm), 1)
    strict_lower = jnp.where(rr > cc, 1.0, 0.0).astype(BF16)
    before = jnp.dot(strict_lower, hot.astype(BF16), preferred_element_type=F32) + cnt_s[...]
    rank0 = jnp.sum(jnp.where(hot0, before, 0.0), axis=-1, keepdims=True)
    rank1 = jnp.sum(jnp.where(hot1, before, 0.0), axis=-1, keepdims=True)
    cnt_s[...] = cnt_s[...] + jnp.sum(hot, axis=0, keepdims=True)
    cnt_ref[...] = cnt_s[...]

    eid0 = (i1 - EXPERT_LANE0).astype(F32)
    eid1 = (i2 - EXPERT_LANE0).astype(F32)
    rec = jnp.zeros((tm, ROUTE_LANES), F32)
    for col, val in enumerate((eid0, eid1, w0, w1, rank0, rank1)):
        rec = jnp.where(lane == col, val, rec)
    route_ref[...] = rec
    route_t_ref[...] = rec.T[0:SUBLANES]


def _outproj_router(x2, m_a, m_b, cnt0, p, tm):
    t = x2.shape[0]
    row = lambda i: (i, 0)
    return pl.pallas_call(
        _outproj_router_kernel, grid=(t // tm,),
        in_specs=[pl.BlockSpec((tm, D_MODEL), row), pl.BlockSpec((tm, D_MODEL), row), pl.BlockSpec((tm, D_MODEL), row),
                  _const_spec((D_MODEL, D_MODEL)), _const_spec((1, D_MODEL)), _const_spec((3 * D_MODEL, ROUTE_LANES)),
                  _const_spec((1, ROUTE_LANES)), _const_spec((1, ROUTE_LANES))],
        out_specs=(pl.BlockSpec((tm, D_MODEL), row), pl.BlockSpec((tm, PACKED_COLS), row),
                   pl.BlockSpec((tm, ROUTE_LANES), row), pl.BlockSpec((SUBLANES, tm), lambda i: (0, i)),
                   _const_spec((1, ROUTE_LANES))),
        out_shape=(jax.ShapeDtypeStruct((t, D_MODEL), F32), jax.ShapeDtypeStruct((t, PACKED_COLS), jnp.uint32),
                   jax.ShapeDtypeStruct((t, ROUTE_LANES), F32), jax.ShapeDtypeStruct((SUBLANES, t), F32),
                   jax.ShapeDtypeStruct((1, ROUTE_LANES), F32)),
        scratch_shapes=[pltpu.VMEM((1, ROUTE_LANES), F32)],
        compiler_params=_cparams(("arbitrary",)), name="outproj_router",
    )(x2, m_a, m_b, p["w_out"], p["g_ffn"], p["w_route"], p["b_route"], cnt0)


def _dispatch_kernel(dest_ref, h_ref, xs_in_ref, xs_ref, sem, *, tm, n_tok):
    del xs_in_ref
    base = pl.program_id(0) * tm

    def issue(r, _):
        for k in range(TOP_K):
            d = dest_ref[k * n_tok + base + r]
            pltpu.make_async_copy(h_ref.at[pl.ds(r, 1), :], xs_ref.at[pl.ds(d, 1), :], sem).start()
        return 0

    lax.fori_loop(0, tm, issue, 0, unroll=ROW_DMA_UNROLL)
    for k in range(TOP_K):
        pltpu.make_async_copy(h_ref, xs_ref.at[pl.ds(0, tm), :], sem).wait()


def _dispatch(dest_flat, h2, xs, tm):
    t = h2.shape[0]
    return pl.pallas_call(
        functools.partial(_dispatch_kernel, tm=tm, n_tok=t),
        grid_spec=pltpu.PrefetchScalarGridSpec(
            num_scalar_prefetch=1, grid=(t // tm,),
            in_specs=[pl.BlockSpec((tm, PACKED_COLS), lambda i, d: (i, 0)), pl.BlockSpec(memory_space=pl.ANY)],
            out_specs=pl.BlockSpec(memory_space=pl.ANY),
            scratch_shapes=[pltpu.SemaphoreType.DMA]),
        out_shape=jax.ShapeDtypeStruct(xs.shape, xs.dtype),
        input_output_aliases={2: 0},
        compiler_params=_cparams(("arbitrary",)), name="moe_dispatch",
    )(dest_flat, h2, xs)


def _experts_kernel(blk_e_ref, nxt_e_ref, slot_ref, n_used_ref, x_ref, wg_hbm, wu_hbm, wd_hbm, y_ref,
                    wg_f, wu_f, wd_f, wg_s, wu_s, wd_s, sems):
    i = pl.program_id(0)
    e = blk_e_ref[i]
    slot = slot_ref[i]
    first = jnp.logical_or(i == 0, e != blk_e_ref[jnp.maximum(i - 1, 0)])

    def weight_copies(expert, s):
        return (pltpu.make_async_copy(wg_hbm.at[expert], wg_f.at[s], sems.at[s, 0]),
                pltpu.make_async_copy(wu_hbm.at[expert], wu_f.at[s], sems.at[s, 1]),
                pltpu.make_async_copy(wd_hbm.at[expert], wd_f.at[s], sems.at[s, 2]))

    @pl.when(i == 0)
    def _():
        for cp in weight_copies(e, slot):
            cp.start()

    @pl.when(first)
    def _():
        for cp in weight_copies(e, slot):
            cp.wait()
        wg_s[...] = wg_f[slot].astype(BF16)
        wu_s[...] = wu_f[slot].astype(BF16)
        wd_s[...] = wd_f[slot].astype(BF16)
        nxt = nxt_e_ref[i]

        @pl.when(nxt != e)
        def _():
            for cp in weight_copies(nxt, 1 - slot):
                cp.start()

    @pl.when(i < n_used_ref[0])
    def _():
        xb = _unpack_bf16_pairs(x_ref[...]).astype(BF16)
        gate = jnp.dot(xb, wg_s[...], preferred_element_type=F32)
        up = jnp.dot(xb, wu_s[...], preferred_element_type=F32)
        act = (jax.nn.silu(gate) * up).astype(BF16)
        y_ref[...] = _pack_bf16_pairs(jnp.dot(act, wd_s[...], preferred_element_type=F32))

    @pl.when(i >= n_used_ref[0])
    def _():
        y_ref[...] = jnp.zeros_like(y_ref)


def _experts(blk_e, nxt_e, slot, n_used, xs, w_gate, w_up, w_down):
    n_rows = xs.shape[0]
    rb = MOE_ROWS
    row_blk = lambda i, *_: (i, 0)
    return pl.pallas_call(
        _experts_kernel,
        grid_spec=pltpu.PrefetchScalarGridSpec(
            num_scalar_prefetch=4, grid=(n_rows // rb,),
            in_specs=[pl.BlockSpec((rb, PACKED_COLS), row_blk),
                      pl.BlockSpec(memory_space=pl.ANY), pl.BlockSpec(memory_space=pl.ANY),
                      pl.BlockSpec(memory_space=pl.ANY)],
            out_specs=pl.BlockSpec((rb, PACKED_COLS), row_blk),
            scratch_shapes=[pltpu.VMEM((2, D_MODEL, D_EXPERT), F32), pltpu.VMEM((2, D_MODEL, D_EXPERT), F32),
                            pltpu.VMEM((2, D_EXPERT, D_MODEL), F32),
                            pltpu.VMEM((D_MODEL, D_EXPERT), BF16), pltpu.VMEM((D_MODEL, D_EXPERT), BF16),
                            pltpu.VMEM((D_EXPERT, D_MODEL), BF16), pltpu.SemaphoreType.DMA((2, 3))]),
        out_shape=jax.ShapeDtypeStruct((n_rows, PACKED_COLS), jnp.uint32),
        compiler_params=_cparams(("arbitrary",)), name="moe_experts",
    )(blk_e, nxt_e, slot, n_used, xs, w_gate, w_up, w_down)


def _combine_kernel(dest_ref, x1_ref, route_ref, ys_ref, o_ref, gbuf, sem, *, tm, n_tok):
    base = pl.program_id(0) * tm

    def issue(r, _):
        for k in range(TOP_K):
            d = dest_ref[k * n_tok + base + r]
            pltpu.make_async_copy(ys_ref.at[pl.ds(d, 1), :], gbuf.at[k, pl.ds(r, 1), :], sem).start()
        return 0

    lax.fori_loop(0, tm, issue, 0, unroll=ROW_DMA_UNROLL)
    for k in range(TOP_K):
        pltpu.make_async_copy(ys_ref.at[pl.ds(0, tm), :], gbuf.at[k], sem).wait()
    w0 = route_ref[:, 2:3]
    w1 = route_ref[:, 3:4]
    o_ref[...] = x1_ref[...] + (_unpack_bf16_pairs(gbuf[0]) * w0 + _unpack_bf16_pairs(gbuf[1]) * w1)


def _combine(dest_flat, x1, route, ys, tm):
    t = x1.shape[0]
    return pl.pallas_call(
        functools.partial(_combine_kernel, tm=tm, n_tok=t),
        grid_spec=pltpu.PrefetchScalarGridSpec(
            num_scalar_prefetch=1, grid=(t // tm,),
            in_specs=[pl.BlockSpec((tm, D_MODEL), lambda i, d: (i, 0)),
                      pl.BlockSpec((tm, ROUTE_LANES), lambda i, d: (i, 0)),
                      pl.BlockSpec(memory_space=pl.ANY)],
            out_specs=pl.BlockSpec((tm, D_MODEL), lambda i, d: (i, 0)),
            scratch_shapes=[pltpu.VMEM((TOP_K, tm, PACKED_COLS), jnp.uint32), pltpu.SemaphoreType.DMA]),
        out_shape=jax.ShapeDtypeStruct((t, D_MODEL), F32),
        compiler_params=_cparams(("arbitrary",)), name="moe_combine",
    )(dest_flat, x1, route, ys)


def _rope_tables(seq_len):
    inv = ROPE_THETA ** (-jnp.arange(0, AXIS_DIM, 2, dtype=F32) / AXIS_DIM)
    n_grid_rows = seq_len // GRID_W
    ang_r = jnp.arange(n_grid_rows, dtype=F32)[:, None] * inv
    ang_c = jnp.arange(GRID_W, dtype=F32)[:, None] * inv
    by_row = lambda a: jnp.repeat(a, GRID_W, axis=0)
    by_col = lambda a: jnp.tile(a, (n_grid_rows, 1))
    cos_r, sin_r, cos_c, sin_c = by_row(jnp.cos(ang_r)), by_row(jnp.sin(ang_r)), by_col(jnp.cos(ang_c)), by_col(jnp.sin(ang_c))
    cos = jnp.concatenate([cos_r, cos_r, cos_c, cos_c], axis=-1)
    sin = jnp.concatenate([-sin_r, sin_r, -sin_c, sin_c], axis=-1)
    return cos, sin


def _prepare_params(l, max_seq, tm, g_mix, w_in, b_gate, conv_w, conv_b, lru_wa, lru_ba, lru_wx, lru_bx, lru_lam,
                    q_gain, k_gain, w_out, g_ffn, w_rg, b_rg, w_re, b_re):
    c_u, c_gr, c_q, c_k, c_v = (LRU_WIDTH, 2 * LRU_WIDTH, 2 * LRU_WIDTH + D_MODEL,
                                2 * LRU_WIDTH + D_MODEL + ATTN_KV, 2 * LRU_WIDTH + D_MODEL + 2 * ATTN_KV)
    w = w_in[l]
    w_nat = jnp.concatenate([w[:, :c_gr], w[:, c_q:c_k], w[:, c_v:]], axis=1).astype(BF16)
    w_tr = jnp.concatenate([w[:, c_gr:c_q], w[:, c_k:c_v]], axis=1).T.astype(BF16)
    cos, sin = _rope_tables(max_seq)
    pad = ROUTE_LANES - N_GROUPS - N_EXPERTS
    w_route = jnp.concatenate([w_rg[l], w_re[l], jnp.zeros((D_MODEL, pad), F32)], axis=1)
    w_route_hi = w_route.astype(BF16)
    w_route_lo = (w_route - w_route_hi.astype(F32)).astype(BF16)
    w_route = jnp.concatenate([w_route_hi, w_route_lo, w_route_hi], axis=0)
    b_route = jnp.concatenate([b_rg[l], b_re[l], jnp.zeros((pad,), F32)])[None, :]
    lru_w = (0.5 * jnp.concatenate([lru_wa[l], lru_wx[l]], axis=-1)).astype(BF16)
    score_bound = (SCORE_BOUND_SLACK * LOG2E * math.sqrt(HEAD_DIM)
                   * jnp.max(jnp.abs(q_gain[l])) * jnp.max(jnp.abs(k_gain[l]))).reshape(1)
    return dict(
        score_bound=score_bound,
        g_mix=g_mix[l][None, :], w_nat=w_nat, w_tr=w_tr, b_gate=b_gate[l][None, :],
        k_gain=k_gain[l][None, :], q_gain_t=jnp.broadcast_to(q_gain[l][:, None], (HEAD_DIM, tm)),
        cos=cos, sin=sin, cos_t=cos.T, sin_t=sin.T,
        conv_w=conv_w[l], conv_b=conv_b[l][None, :], lru_w=lru_w,
        lru_ba=lru_ba[l].reshape(2, 1, LRU_WIDTH), lru_bx=lru_bx[l].reshape(2, 1, LRU_WIDTH),
        lru_lam=lru_lam[l].reshape(2, 1, LRU_WIDTH),
        w_out=w_out[l].astype(BF16), g_ffn=g_ffn[l][None, :], w_route=w_route, b_route=b_route,
    )


def _mixer(x, p, cnt0, tm, tk):
    batch, seq_len, _ = x.shape
    x2 = x.reshape(batch * seq_len, D_MODEL)
    u, ggr, k, gate, qt, vt = _inproj(x2, seq_len, p, tm)
    h_fwd = _lru_scan(u, gate, ggr, None, p, 0, batch, seq_len, tm, reverse=False)
    m_a = _lru_scan(u, gate, ggr, h_fwd, p, 1, batch, seq_len, tm, reverse=True)
    m_b = _attention(p["score_bound"], qt, k, vt, gate, batch, seq_len, tm, min(tk, seq_len))
    return _outproj_router(x2, m_a, m_b, cnt0, p, tm)


def _layer(xs_in, l, weights):
    (g_mix, w_in, b_gate, conv_w, conv_b, lru_wa, lru_ba, lru_wx, lru_bx, lru_lam,
     q_gain, k_gain, w_out, g_ffn, w_rg, b_rg, w_re, b_re, w_gate, w_up, w_down) = weights
    tm = ROW_TILE
    max_seq = max(x.shape[1] for x in xs_in)
    p = _prepare_params(l, max_seq, tm, g_mix, w_in, b_gate, conv_w, conv_b, lru_wa, lru_ba, lru_wx, lru_bx,
                        lru_lam, q_gain, k_gain, w_out, g_ffn, w_rg, b_rg, w_re, b_re)

    cnt = jnp.zeros((1, ROUTE_LANES), F32)
    x1s, h2s, routes, routes_t = [], [], [], []
    for x in xs_in:
        x1, h2, route, route_t, cnt = _mixer(x, p, cnt, tm, KV_CHUNK)
        x1s.append(x1)
        h2s.append(h2)
        routes.append(route)
        routes_t.append(route_t)

    n_tok = sum(x1.shape[0] for x1 in x1s)
    n_rows = n_tok * TOP_K + N_EXPERTS * MOE_ROWS
    n_blk = n_rows // MOE_ROWS
    counts = cnt[0, EXPERT_LANE0:EXPERT_LANE0 + N_EXPERTS].astype(jnp.int32)
    padded = (counts + MOE_ROWS - 1) // MOE_ROWS * MOE_ROWS
    pend = jnp.cumsum(padded)
    pstart = pend - padded
    blk_idx = jnp.arange(n_blk, dtype=jnp.int32)
    n_used = (pend[-1:] // MOE_ROWS).astype(jnp.int32)
    blk_e = jnp.minimum(jnp.sum((pend[None, :] <= (blk_idx * MOE_ROWS)[:, None]).astype(jnp.int32), axis=1),
                        N_EXPERTS - 1)
    blk_e = jnp.where(blk_idx < n_used, blk_e, blk_e[jnp.maximum(n_used[0] - 1, 0)])
    starts = jnp.concatenate([jnp.ones((1,), jnp.int32), (blk_e[1:] != blk_e[:-1]).astype(jnp.int32)])
    slot = (jnp.cumsum(starts) - 1) % 2
    later_start = lax.cummin(jnp.where(starts == 1, blk_idx, n_blk), reverse=True)
    next_start = jnp.concatenate([later_start[1:], jnp.full((1,), n_blk, jnp.int32)])
    nxt_e = jnp.where(next_start < n_blk, blk_e[jnp.minimum(next_start, n_blk - 1)], blk_e)

    dests = []
    xs = jnp.zeros((n_rows, PACKED_COLS), jnp.uint32)
    for h2, route_t in zip(h2s, routes_t):
        eid = route_t[0:TOP_K].astype(jnp.int32)
        rank = route_t[4:4 + TOP_K].astype(jnp.int32)
        experts = jnp.arange(N_EXPERTS, dtype=jnp.int32)[:, None, None]
        dest = (rank + jnp.sum(jnp.where(eid[None] == experts, pstart[:, None, None], 0), axis=0)).reshape(-1)
        dests.append(dest)
        xs = _dispatch(dest, h2, xs, tm)
    ys = _experts(blk_e, nxt_e, slot.astype(jnp.int32), n_used, xs, w_gate[l], w_up[l], w_down[l])
    outs = []
    for x, x1, route, dest in zip(xs_in, x1s, routes, dests):
        outs.append(_combine(dest, x1, route, ys, tm).reshape(x.shape))
    return outs


def kernel(x_prompt, x_sample, g_mix, w_in, b_gate, conv_w, conv_b, lru_wa, lru_ba, lru_wx, lru_bx, lru_lam,
           q_gain, k_gain, w_out, g_ffn, w_rg, b_rg, w_re, b_re, w_gate, w_up, w_down):
    weights = (g_mix, w_in, b_gate, conv_w, conv_b, lru_wa, lru_ba, lru_wx, lru_bx, lru_lam,
               q_gain, k_gain, w_out, g_ffn, w_rg, b_rg, w_re, b_re, w_gate, w_up, w_down)
    xs = [x_prompt, x_sample]
    for l in range(g_mix.shape[0]):
        xs = _layer(xs, l, weights)
    return tuple(xs)
```

```python
import functools
import math

import jax
import jax.numpy as jnp
from jax import lax
from jax.experimental import pallas as pl
from jax.experimental.pallas import tpu as pltpu

F32 = jnp.float32
BF16 = jnp.bfloat16

D_MODEL = 1024
N_HEADS = 8
N_KV_HEADS = 2
GROUP = N_HEADS // N_KV_HEADS
HEAD_DIM = D_MODEL // N_HEADS
AXIS_DIM = HEAD_DIM // 2
N_FREQ = AXIS_DIM // 2
GRID_W = 64
ROPE_THETA = 10000.0
LRU_WIDTH = D_MODEL
LRU_BLOCKS = 8
LRU_BLOCK_W = LRU_WIDTH // LRU_BLOCKS
CONV_W = 4
CONV_PAD_LEFT = 2
LRU_C = 8.0
N_GROUPS = 4
EXPERTS_PER_GROUP = 8
N_EXPERTS = N_GROUPS * EXPERTS_PER_GROUP
TOP_K = 2
D_EXPERT = D_MODEL // 2
ATTN_KV = N_KV_HEADS * HEAD_DIM
EPS = 1e-6

LANES = 128
SUBLANES = 8
VMEM_LIMIT_BYTES = 56 * 1024 * 1024

ROW_TILE = 256
WIDE_TILE = 512
KV_CHUNK = 512
KV_BLOCK = 8192
MOE_ROWS = 512
ROW_DMA_UNROLL = 8
ROUTE_LANES = LANES
PACKED_COLS = D_MODEL // 2
EXPERT_LANE0 = N_GROUPS

NAT_COLS = 2 * LRU_WIDTH + ATTN_KV + 2 * D_MODEL
TR_ROWS = D_MODEL + ATTN_KV
LOG2E = math.log2(math.e)
SAFE_EXP2_RANGE = 100.0
SCORE_BOUND_SLACK = 1.01


def _cparams(semantics):
    return pltpu.CompilerParams(dimension_semantics=semantics, vmem_limit_bytes=VMEM_LIMIT_BYTES)


def _const_spec(shape):
    nd = len(shape)
    return pl.BlockSpec(shape, lambda *_: (0,) * nd)


def _sigmoid(x):
    return 0.5 * jnp.tanh(0.5 * x) + 0.5


def _pack_bf16_pairs(x):
    n = x.shape[1] // 2
    lo = lax.bitcast_convert_type(x[:, :n].astype(BF16).astype(F32), jnp.uint32)
    hi = lax.bitcast_convert_type(x[:, n:].astype(BF16).astype(F32), jnp.uint32)
    return (hi & jnp.uint32(0xFFFF0000)) | (lo >> 16)


def _unpack_bf16_pairs(w):
    lo = lax.bitcast_convert_type(w << 16, F32)
    hi = lax.bitcast_convert_type(w & jnp.uint32(0xFFFF0000), F32)
    return jnp.concatenate([lo, hi], axis=1)


def _swap_halves_rows(x):
    return jnp.concatenate([x[N_FREQ:AXIS_DIM], x[0:N_FREQ], x[AXIS_DIM + N_FREQ:], x[AXIS_DIM:AXIS_DIM + N_FREQ]], axis=0)


def _inproj_kernel(x_ref, gmix_ref, wnat_ref, wtr_ref, bgate_ref, kgain_ref, qgain_ref,
                   cos_ref, sin_ref, cost_ref, sint_ref,
                   u_ref, ggr_ref, k_ref, gate_ref, qt_ref, vt_ref):
    tm = x_ref.shape[0]
    x = x_ref[...]
    h = (x * lax.rsqrt(jnp.mean(x * x, axis=-1, keepdims=True) + EPS) * gmix_ref[...]).astype(BF16)

    def nat(lo, hi):
        return jnp.dot(h, wnat_ref[:, lo:hi], preferred_element_type=F32)

    u_ref[...] = nat(0, LRU_WIDTH)
    ggr_ref[...] = jax.nn.gelu(nat(LRU_WIDTH, 2 * LRU_WIDTH))
    k0 = 2 * LRU_WIDTH
    kraw = nat(k0, k0 + ATTN_KV)
    g0 = k0 + ATTN_KV
    gate_ref[...] = _sigmoid(nat(g0, g0 + 2 * D_MODEL) + bgate_ref[...])

    cos = cos_ref[...]
    sin = sin_ref[...]
    lane = lax.broadcasted_iota(jnp.int32, (tm, HEAD_DIM), 1)
    first_half = (lane % AXIS_DIM) < N_FREQ
    for j in range(N_KV_HEADS):
        kj = kraw[:, j * HEAD_DIM:(j + 1) * HEAD_DIM]
        kn = kj * lax.rsqrt(jnp.mean(kj * kj, axis=-1, keepdims=True) + EPS) * kgain_ref[...]
        partner = jnp.where(first_half, pltpu.roll(kn, HEAD_DIM - N_FREQ, 1), pltpu.roll(kn, N_FREQ, 1))
        k_ref[:, j * HEAD_DIM:(j + 1) * HEAD_DIM] = (kn * cos + partner * sin).astype(BF16)

    zt = lax.dot_general(wtr_ref[...], h, (((1,), (1,)), ((), ())), preferred_element_type=F32)
    cost = cost_ref[...]
    sint = sint_ref[...]
    qgain = qgain_ref[...]
    qscale = (HEAD_DIM ** -0.5) * LOG2E
    for hd in range(N_HEADS):
        xq = zt[hd * HEAD_DIM:(hd + 1) * HEAD_DIM, :]
        xn = xq * lax.rsqrt(jnp.mean(xq * xq, axis=0, keepdims=True) + EPS) * qgain
        rot = (xn * cost + _swap_halves_rows(xn) * sint) * qscale
        g = hd % GROUP
        qt_ref[0, hd // GROUP, :, g * tm:(g + 1) * tm] = rot.astype(BF16)
    for j in range(N_KV_HEADS):
        r0 = D_MODEL + j * HEAD_DIM
        vt_ref[j] = zt[r0:r0 + HEAD_DIM, :].astype(BF16)


def _inproj(x2, seq_len, p, tm):
    t = x2.shape[0]
    n_pos = seq_len // tm
    grid = (t // tm,)
    row = lambda i: (i, 0)
    in_specs = [
        pl.BlockSpec((tm, D_MODEL), row),
        _const_spec((1, D_MODEL)),
        _const_spec((D_MODEL, NAT_COLS)),
        _const_spec((TR_ROWS, D_MODEL)),
        _const_spec((1, 2 * D_MODEL)),
        _const_spec((1, HEAD_DIM)),
        _const_spec((HEAD_DIM, tm)),
        pl.BlockSpec((tm, HEAD_DIM), lambda i: (i % n_pos, 0)),
        pl.BlockSpec((tm, HEAD_DIM), lambda i: (i % n_pos, 0)),
        pl.BlockSpec((HEAD_DIM, tm), lambda i: (0, i % n_pos)),
        pl.BlockSpec((HEAD_DIM, tm), lambda i: (0, i % n_pos)),
    ]
    out_shape = (
        jax.ShapeDtypeStruct((t, LRU_WIDTH), F32),
        jax.ShapeDtypeStruct((t, LRU_WIDTH), F32),
        jax.ShapeDtypeStruct((t, ATTN_KV), BF16),
        jax.ShapeDtypeStruct((t, 2 * D_MODEL), F32),
        jax.ShapeDtypeStruct((t // tm, N_KV_HEADS, HEAD_DIM, GROUP * tm), BF16),
        jax.ShapeDtypeStruct((N_KV_HEADS, HEAD_DIM, t), BF16),
    )
    out_specs = (
        pl.BlockSpec((tm, LRU_WIDTH), row),
        pl.BlockSpec((tm, LRU_WIDTH), row),
        pl.BlockSpec((tm, ATTN_KV), row),
        pl.BlockSpec((tm, 2 * D_MODEL), row),
        pl.BlockSpec((1, N_KV_HEADS, HEAD_DIM, GROUP * tm), lambda i: (i, 0, 0, 0)),
        pl.BlockSpec((N_KV_HEADS, HEAD_DIM, tm), lambda i: (0, 0, i)),
    )
    return pl.pallas_call(
        _inproj_kernel, grid=grid, in_specs=in_specs, out_specs=out_specs, out_shape=out_shape,
        compiler_params=_cparams(("arbitrary",)), name="inproj",
    )(x2, p["g_mix"], p["w_nat"], p["w_tr"], p["b_gate"], p["k_gain"], p["q_gain_t"],
      p["cos"], p["sin"], p["cos_t"], p["sin_t"])


def _lru_kernel(*refs, reverse, tm, nt):
    if reverse:
        (up_ref, uc_ref, un_ref, cw_ref, cb_ref, w_ref, ba_ref, bx_ref, lam_ref, hf_ref, ggr_ref, ga_ref,
         out_ref, ubuf, a_s, g_s, carry) = refs
    else:
        (up_ref, uc_ref, un_ref, cw_ref, cb_ref, w_ref, ba_ref, bx_ref, lam_ref,
         out_ref, ubuf, a_s, g_s, carry) = refs
    i = pl.program_id(1)
    ti = (nt - 1 - i) if reverse else i

    @pl.when(i == 0)
    def _():
        carry[...] = jnp.zeros_like(carry)

    ubuf[0:SUBLANES] = jnp.where(ti == 0, 0.0, up_ref[...])
    ubuf[SUBLANES:SUBLANES + tm] = uc_ref[...]
    ubuf[SUBLANES + tm:2 * SUBLANES + tm] = jnp.where(ti == nt - 1, 0.0, un_ref[...])
    cw = cw_ref[...]
    ub = ubuf[...]
    n_buf = tm + 2 * SUBLANES
    xc = cb_ref[...]
    for j in range(CONV_W):
        back = CONV_PAD_LEFT - j
        tap = ub if back == 0 else pltpu.roll(ub, back % n_buf, 0)
        xc = xc + cw[j:j + 1] * tap[SUBLANES:SUBLANES + tm]

    xcb = xc.astype(BF16)
    lam = lam_ref[...]
    half_c = (0.5 * LRU_C) * (jnp.minimum(lam, 0.0) - jnp.log1p(jnp.exp(-jnp.abs(lam))))
    half_ba = 0.5 * ba_ref[...]
    half_bx = 0.5 * bx_ref[...]
    half_xc = 0.5 * xc
    for hb in range(LRU_BLOCKS):
        sl = slice(hb * LRU_BLOCK_W, (hb + 1) * LRU_BLOCK_W)
        gz = jnp.dot(xcb[:, sl], w_ref[hb], preferred_element_type=F32)
        tr = jnp.tanh(gz[:, :LRU_BLOCK_W] + half_ba[:, sl])
        ti = jnp.tanh(gz[:, LRU_BLOCK_W:] + half_bx[:, sl])
        log_a = tr * half_c[:, sl] + half_c[:, sl]
        a = jnp.exp(log_a)
        a_s[:, sl] = a
        y = jnp.tanh(log_a) * (-1.0 - a * a)
        root = jnp.where(y > 0.0, y * lax.rsqrt(y), 0.0)
        g_s[:, sl] = root * (ti * half_xc[:, sl] + half_xc[:, sl])

    n_chunk = tm // SUBLANES
    srow = lax.broadcasted_iota(jnp.int32, (SUBLANES, LRU_WIDTH), 0)

    def chunk(c, h_prev):
        ci = (n_chunk - 1 - c) if reverse else c
        off = pl.multiple_of(ci * SUBLANES, SUBLANES)
        a = a_s[pl.ds(off, SUBLANES), :]
        b = g_s[pl.ds(off, SUBLANES), :]
        for d in (1, 2, 4):
            shift = (SUBLANES - d) if reverse else d
            valid = (srow < SUBLANES - d) if reverse else (srow >= d)
            a_sh = jnp.where(valid, pltpu.roll(a, shift, 0), 1.0)
            b_sh = jnp.where(valid, pltpu.roll(b, shift, 0), 0.0)
            b = a * b_sh + b
            a = a * a_sh
        h = a * h_prev + b
        if reverse:
            rows = pl.ds(off, SUBLANES)
            out_ref[rows, :] = ga_ref[rows, :] * (ggr_ref[rows, :] * (hf_ref[rows, :] + h))
            return h[0:1]
        out_ref[pl.ds(off, SUBLANES), :] = h
        return h[SUBLANES - 1:SUBLANES]

    carry[...] = lax.fori_loop(0, n_chunk, chunk, carry[...])


def _lru_scan(u, gate, ggr, h_fwd, p, d, batch, seq_len, tm, reverse):
    t = u.shape[0]
    nt = seq_len // tm
    per8 = tm // SUBLANES
    n8 = t // SUBLANES

    def tile(b, i):
        return b * nt + ((nt - 1 - i) if reverse else i)

    cur = lambda b, i: (tile(b, i), 0)
    prev = lambda b, i: (jnp.maximum(tile(b, i) * per8 - 1, 0), 0)
    nxt = lambda b, i: (jnp.minimum((tile(b, i) + 1) * per8, n8 - 1), 0)
    const2 = lambda b, i: (0, 0)
    in_specs = [
        pl.BlockSpec((SUBLANES, LRU_WIDTH), prev),
        pl.BlockSpec((tm, LRU_WIDTH), cur),
        pl.BlockSpec((SUBLANES, LRU_WIDTH), nxt),
        pl.BlockSpec((CONV_W, LRU_WIDTH), const2),
        pl.BlockSpec((1, LRU_WIDTH), const2),
        pl.BlockSpec((LRU_BLOCKS, LRU_BLOCK_W, 2 * LRU_BLOCK_W), lambda b, i: (0, 0, 0)),
        pl.BlockSpec((1, LRU_WIDTH), const2),
        pl.BlockSpec((1, LRU_WIDTH), const2),
        pl.BlockSpec((1, LRU_WIDTH), const2),
    ]
    args = [u, u, u, p["conv_w"], p["conv_b"], p["lru_w"][d], p["lru_ba"][d], p["lru_bx"][d], p["lru_lam"][d]]
    if reverse:
        in_specs += [pl.BlockSpec((tm, LRU_WIDTH), cur)] * 3
        args += [h_fwd, ggr, gate]
    return pl.pallas_call(
        functools.partial(_lru_kernel, reverse=reverse, tm=tm, nt=nt),
        grid=(batch, nt), in_specs=in_specs,
        out_specs=pl.BlockSpec((tm, LRU_WIDTH), cur),
        out_shape=jax.ShapeDtypeStruct((t, LRU_WIDTH), F32),
        scratch_shapes=[pltpu.VMEM((tm + 2 * SUBLANES, LRU_WIDTH), F32), pltpu.VMEM((tm, LRU_WIDTH), F32),
                        pltpu.VMEM((tm, LRU_WIDTH), F32), pltpu.VMEM((1, LRU_WIDTH), F32)],
        compiler_params=_cparams(("arbitrary", "arbitrary")), name="lru_bwd" if reverse else "lru_fwd",
    )(*args)


def _attn_kernel(bound_ref, qt_ref, k_ref, vt_ref, gb_ref, o_ref, m_s, l_s, acc_s, *, tq, tk, n_chunks, n_kv_blocks):
    nq_cols = GROUP * tq
    kv_blk = pl.program_id(3)
    bound = bound_ref[0]
    bounded = 2.0 * bound <= SAFE_EXP2_RANGE

    @pl.when(kv_blk == 0)
    def _():
        acc_s[...] = jnp.zeros_like(acc_s)
        l_s[...] = jnp.zeros_like(l_s)
        m_s[...] = jnp.full_like(m_s, -jnp.inf)

    @pl.when(bounded)
    def _():
        qt = qt_ref[0, 0]
        for c in range(n_chunks):
            rows = slice(c * tk, (c + 1) * tk)
            pr = jnp.exp2(jnp.dot(k_ref[rows, :], qt, preferred_element_type=F32) - bound)
            l_s[...] += jnp.sum(pr.reshape(tk // SUBLANES, SUBLANES, nq_cols), axis=0)
            acc_s[...] += jnp.dot(vt_ref[0, :, rows], pr.astype(BF16), preferred_element_type=F32)

    @pl.when(jnp.logical_not(bounded))
    def _():
        qt = qt_ref[0, 0]

        def chunk(c, _):
            off = pl.multiple_of(c * tk, tk)
            s = jnp.dot(k_ref[pl.ds(off, tk), :], qt, preferred_element_type=F32)
            m_old = m_s[...]
            m_new = jnp.maximum(m_old, jnp.max(s, axis=0, keepdims=True))
            alpha = jnp.exp2(m_old - m_new)
            pr = jnp.exp2(s - m_new)
            l_s[...] = alpha * l_s[...] + jnp.sum(pr.reshape(tk // SUBLANES, SUBLANES, nq_cols), axis=0)
            acc_s[...] = alpha * acc_s[...] + jnp.dot(vt_ref[0, :, pl.ds(off, tk)], pr.astype(BF16),
                                                      preferred_element_type=F32)
            m_s[...] = m_new
            return 0

        lax.fori_loop(0, n_chunks, chunk, 0)

    @pl.when(kv_blk == n_kv_blocks - 1)
    def _():
        out_t = acc_s[...] / jnp.sum(l_s[...], axis=0, keepdims=True)
        for g in range(GROUP):
            cols = slice(g * HEAD_DIM, (g + 1) * HEAD_DIM)
            o_ref[:, cols] = gb_ref[:, cols] * out_t[:, g * tq:(g + 1) * tq].T


def _attention(bound, qt, k, vt, gate, batch, seq_len, tq, tk, kv_block):
    t = k.shape[0]
    nq = seq_len // tq
    nkv = seq_len // kv_block
    half = GROUP * HEAD_DIM
    gate_col0 = D_MODEL // half
    return pl.pallas_call(
        functools.partial(_attn_kernel, tq=tq, tk=tk, n_chunks=kv_block // tk, n_kv_blocks=nkv),
        grid=(batch, N_KV_HEADS, nq, nkv),
        in_specs=[
            pl.BlockSpec(memory_space=pltpu.SMEM),
            pl.BlockSpec((1, 1, HEAD_DIM, GROUP * tq), lambda b, j, i, c: (b * nq + i, j, 0, 0)),
            pl.BlockSpec((kv_block, HEAD_DIM), lambda b, j, i, c: (b * nkv + c, j)),
            pl.BlockSpec((1, HEAD_DIM, kv_block), lambda b, j, i, c: (j, 0, b * nkv + c)),
            pl.BlockSpec((tq, half), lambda b, j, i, c: (b * nq + i, gate_col0 + j)),
        ],
        out_specs=pl.BlockSpec((tq, half), lambda b, j, i, c: (b * nq + i, j)),
        out_shape=jax.ShapeDtypeStruct((t, D_MODEL), F32),
        scratch_shapes=[pltpu.VMEM((1, GROUP * tq), F32), pltpu.VMEM((SUBLANES, GROUP * tq), F32),
                        pltpu.VMEM((HEAD_DIM, GROUP * tq), F32)],
        compiler_params=_cparams(("arbitrary", "arbitrary", "arbitrary", "arbitrary")), name="attention",
    )(bound, qt, k, vt, gate)


def _outproj_router_kernel(x_ref, ma_ref, mb_ref, wout_ref, gffn_ref, wr_ref, br_ref, cnt0_ref,
                           x1_ref, h2_ref, route_ref, route_t_ref, cnt_ref, cnt_s):
    tm = x_ref.shape[0]
    i = pl.program_id(0)

    @pl.when(i == 0)
    def _():
        cnt_s[...] = cnt0_ref[...]

    merged = (ma_ref[...] + mb_ref[...]).astype(BF16)
    x1 = x_ref[...] + jnp.dot(merged, wout_ref[...], preferred_element_type=F32)
    x1_ref[...] = x1
    h2 = x1 * lax.rsqrt(jnp.mean(x1 * x1, axis=-1, keepdims=True) + EPS) * gffn_ref[...]
    h2_ref[...] = _pack_bf16_pairs(h2)

    hi = h2.astype(BF16)
    lo = (h2 - hi.astype(F32)).astype(BF16)
    z = jnp.dot(jnp.concatenate([hi, hi, lo], axis=1), wr_ref[...], preferred_element_type=F32) + br_ref[...]
    lane = lax.broadcasted_iota(jnp.int32, (tm, ROUTE_LANES), 1)
    neg = -jnp.inf

    def first_argmax(v):
        m = jnp.max(v, axis=-1, keepdims=True)
        return m, jnp.min(jnp.where(v == m, lane, ROUTE_LANES), axis=-1, keepdims=True)

    zg = jnp.where(lane < N_GROUPS, z, neg)
    mg, grp = first_argmax(zg)
    p_sel = 1.0 / jnp.sum(jnp.exp(zg - mg), axis=-1, keepdims=True)
    e_lo = EXPERT_LANE0 + EXPERTS_PER_GROUP * grp
    ze = jnp.where((lane >= e_lo) & (lane < e_lo + EXPERTS_PER_GROUP), z, neg)
    m1, i1 = first_argmax(ze)
    m2, i2 = first_argmax(jnp.where(lane == i1, neg, ze))
    e2 = jnp.exp(m2 - m1)
    w0 = p_sel / (1.0 + e2)
    w1 = p_sel * e2 / (1.0 + e2)

    hot0 = lane == i1
    hot1 = lane == i2
    hot = jnp.where(hot0 | hot1, 1.0, 0.0)
    rr = lax.broadcasted_iota(jnp.int32, (tm, tm), 0)
    cc = lax.broadcasted_iota(jnp.int32, (tm, tm), 1)
    strict_lower = jnp.where(rr > cc, 1.0, 0.0).astype(BF16)
    before = jnp.dot(strict_lower, hot.astype(BF16), preferred_element_type=F32) + cnt_s[...]
    rank0 = jnp.sum(jnp.where(hot0, before, 0.0), axis=-1, keepdims=True)
    rank1 = jnp.sum(jnp.where(hot1, before, 0.0), axis=-1, keepdims=True)
    cnt_s[...] = cnt_s[...] + jnp.sum(hot, axis=0, keepdims=True)
    cnt_ref[...] = cnt_s[...]

    eid0 = (i1 - EXPERT_LANE0).astype(F32)
    eid1 = (i2 - EXPERT_LANE0).astype(F32)
    rec = jnp.zeros((tm, ROUTE_LANES), F32)
    for col, val in enumerate((eid0, eid1, w0, w1, rank0, rank1)):
        rec = jnp.where(lane == col, val, rec)
    route_ref[...] = rec
    route_t_ref[...] = rec.T[0:SUBLANES]


def _outproj_router(x2, m_a, m_b, cnt0, p, tm):
    t = x2.shape[0]
    row = lambda i: (i, 0)
    return pl.pallas_call(
        _outproj_router_kernel, grid=(t // tm,),
        in_specs=[pl.BlockSpec((tm, D_MODEL), row), pl.BlockSpec((tm, D_MODEL), row), pl.BlockSpec((tm, D_MODEL), row),
                  _const_spec((D_MODEL, D_MODEL)), _const_spec((1, D_MODEL)), _const_spec((3 * D_MODEL, ROUTE_LANES)),
                  _const_spec((1, ROUTE_LANES)), _const_spec((1, ROUTE_LANES))],
        out_specs=(pl.BlockSpec((tm, D_MODEL), row), pl.BlockSpec((tm, PACKED_COLS), row),
                   pl.BlockSpec((tm, ROUTE_LANES), row), pl.BlockSpec((SUBLANES, tm), lambda i: (0, i)),
                   _const_spec((1, ROUTE_LANES))),
        out_shape=(jax.ShapeDtypeStruct((t, D_MODEL), F32), jax.ShapeDtypeStruct((t, PACKED_COLS), jnp.uint32),
                   jax.ShapeDtypeStruct((t, ROUTE_LANES), F32), jax.ShapeDtypeStruct((SUBLANES, t), F32),
                   jax.ShapeDtypeStruct((1, ROUTE_LANES), F32)),
        scratch_shapes=[pltpu.VMEM((1, ROUTE_LANES), F32)],
        compiler_params=_cparams(("arbitrary",)), name="outproj_router",
    )(x2, m_a, m_b, p["w_out"], p["g_ffn"], p["w_route"], p["b_route"], cnt0)


def _dispatch_kernel(dest_ref, h_ref, xs_in_ref, xs_ref, sem, *, tm, n_tok):
    del xs_in_ref
    base = pl.program_id(0) * tm

    def issue(r, _):
        for k in range(TOP_K):
            d = dest_ref[k * n_tok + base + r]
            pltpu.make_async_copy(h_ref.at[pl.ds(r, 1), :], xs_ref.at[pl.ds(d, 1), :], sem).start()
        return 0

    lax.fori_loop(0, tm, issue, 0, unroll=ROW_DMA_UNROLL)
    for k in range(TOP_K):
        pltpu.make_async_copy(h_ref, xs_ref.at[pl.ds(0, tm), :], sem).wait()


def _dispatch(dest_flat, h2, xs, tm):
    t = h2.shape[0]
    return pl.pallas_call(
        functools.partial(_dispatch_kernel, tm=tm, n_tok=t),
        grid_spec=pltpu.PrefetchScalarGridSpec(
            num_scalar_prefetch=1, grid=(t // tm,),
            in_specs=[pl.BlockSpec((tm, PACKED_COLS), lambda i, d: (i, 0)), pl.BlockSpec(memory_space=pl.ANY)],
            out_specs=pl.BlockSpec(memory_space=pl.ANY),
            scratch_shapes=[pltpu.SemaphoreType.DMA]),
        out_shape=jax.ShapeDtypeStruct(xs.shape, xs.dtype),
        input_output_aliases={2: 0},
        compiler_params=_cparams(("arbitrary",)), name="moe_dispatch",
    )(dest_flat, h2, xs)


def _experts_kernel(blk_e_ref, nxt_e_ref, slot_ref, n_used_ref, x_ref, wg_hbm, wu_hbm, wd_hbm, y_ref,
                    wg_f, wu_f, wd_f, wg_s, wu_s, wd_s, sems):
    i = pl.program_id(0)
    e = blk_e_ref[i]
    slot = slot_ref[i]
    first = jnp.logical_or(i == 0, e != blk_e_ref[jnp.maximum(i - 1, 0)])

    def weight_copies(expert, s):
        return (pltpu.make_async_copy(wg_hbm.at[expert], wg_f.at[s], sems.at[s, 0]),
                pltpu.make_async_copy(wu_hbm.at[expert], wu_f.at[s], sems.at[s, 1]),
                pltpu.make_async_copy(wd_hbm.at[expert], wd_f.at[s], sems.at[s, 2]))

    @pl.when(i == 0)
    def _():
        for cp in weight_copies(e, slot):
            cp.start()

    @pl.when(first)
    def _():
        for cp in weight_copies(e, slot):
            cp.wait()
        wg_s[...] = wg_f[slot].astype(BF16)
        wu_s[...] = wu_f[slot].astype(BF16)
        wd_s[...] = wd_f[slot].astype(BF16)
        nxt = nxt_e_ref[i]

        @pl.when(nxt != e)
        def _():
            for cp in weight_copies(nxt, 1 - slot):
                cp.start()

    @pl.when(i < n_used_ref[0])
    def _():
        xb = _unpack_bf16_pairs(x_ref[...]).astype(BF16)
        gate = jnp.dot(xb, wg_s[...], preferred_element_type=F32)
        up = jnp.dot(xb, wu_s[...], preferred_element_type=F32)
        act = (jax.nn.silu(gate) * up).astype(BF16)
        y_ref[...] = _pack_bf16_pairs(jnp.dot(act, wd_s[...], preferred_element_type=F32))

    @pl.when(i >= n_used_ref[0])
    def _():
        y_ref[...] = jnp.zeros_like(y_ref)


def _experts(blk_e, nxt_e, slot, n_used, xs, w_gate, w_up, w_down):
    n_rows = xs.shape[0]
    rb = MOE_ROWS
    row_blk = lambda i, *_: (i, 0)
    return pl.pallas_call(
        _experts_kernel,
        grid_spec=pltpu.PrefetchScalarGridSpec(
            num_scalar_prefetch=4, grid=(n_rows // rb,),
            in_specs=[pl.BlockSpec((rb, PACKED_COLS), row_blk),
                      pl.BlockSpec(memory_space=pl.ANY), pl.BlockSpec(memory_space=pl.ANY),
                      pl.BlockSpec(memory_space=pl.ANY)],
            out_specs=pl.BlockSpec((rb, PACKED_COLS), row_blk),
            scratch_shapes=[pltpu.VMEM((2, D_MODEL, D_EXPERT), F32), pltpu.VMEM((2, D_MODEL, D_EXPERT), F32),
                            pltpu.VMEM((2, D_EXPERT, D_MODEL), F32),
                            pltpu.VMEM((D_MODEL, D_EXPERT), BF16), pltpu.VMEM((D_MODEL, D_EXPERT), BF16),
                            pltpu.VMEM((D_EXPERT, D_MODEL), BF16), pltpu.SemaphoreType.DMA((2, 3))]),
        out_shape=jax.ShapeDtypeStruct((n_rows, PACKED_COLS), jnp.uint32),
        compiler_params=_cparams(("arbitrary",)), name="moe_experts",
    )(blk_e, nxt_e, slot, n_used, xs, w_gate, w_up, w_down)


def _combine_kernel(dest_ref, x1_ref, route_ref, ys_ref, o_ref, gbuf, sem, *, tm, n_tok):
    base = pl.program_id(0) * tm

    def issue(r, _):
        for k in range(TOP_K):
            d = dest_ref[k * n_tok + base + r]
            pltpu.make_async_copy(ys_ref.at[pl.ds(d, 1), :], gbuf.at[k, pl.ds(r, 1), :], sem).start()
        return 0

    lax.fori_loop(0, tm, issue, 0, unroll=ROW_DMA_UNROLL)
    for k in range(TOP_K):
        pltpu.make_async_copy(ys_ref.at[pl.ds(0, tm), :], gbuf.at[k], sem).wait()
    w0 = route_ref[:, 2:3]
    w1 = route_ref[:, 3:4]
    o_ref[...] = x1_ref[...] + (_unpack_bf16_pairs(gbuf[0]) * w0 + _unpack_bf16_pairs(gbuf[1]) * w1)


def _combine(dest_flat, x1, route, ys, tm):
    t = x1.shape[0]
    return pl.pallas_call(
        functools.partial(_combine_kernel, tm=tm, n_tok=t),
        grid_spec=pltpu.PrefetchScalarGridSpec(
            num_scalar_prefetch=1, grid=(t // tm,),
            in_specs=[pl.BlockSpec((tm, D_MODEL), lambda i, d: (i, 0)),
                      pl.BlockSpec((tm, ROUTE_LANES), lambda i, d: (i, 0)),
                      pl.BlockSpec(memory_space=pl.ANY)],
            out_specs=pl.BlockSpec((tm, D_MODEL), lambda i, d: (i, 0)),
            scratch_shapes=[pltpu.VMEM((TOP_K, tm, PACKED_COLS), jnp.uint32), pltpu.SemaphoreType.DMA]),
        out_shape=jax.ShapeDtypeStruct((t, D_MODEL), F32),
        compiler_params=_cparams(("arbitrary",)), name="moe_combine",
    )(dest_flat, x1, route, ys)


def _rope_tables(seq_len):
    inv = ROPE_THETA ** (-jnp.arange(0, AXIS_DIM, 2, dtype=F32) / AXIS_DIM)
    n_grid_rows = seq_len // GRID_W
    ang_r = jnp.arange(n_grid_rows, dtype=F32)[:, None] * inv
    ang_c = jnp.arange(GRID_W, dtype=F32)[:, None] * inv
    by_row = lambda a: jnp.repeat(a, GRID_W, axis=0)
    by_col = lambda a: jnp.tile(a, (n_grid_rows, 1))
    cos_r, sin_r, cos_c, sin_c = by_row(jnp.cos(ang_r)), by_row(jnp.sin(ang_r)), by_col(jnp.cos(ang_c)), by_col(jnp.sin(ang_c))
    cos = jnp.concatenate([cos_r, cos_r, cos_c, cos_c], axis=-1)
    sin = jnp.concatenate([-sin_r, sin_r, -sin_c, sin_c], axis=-1)
    return cos, sin


def _prepare_params(l, max_seq, tm, g_mix, w_in, b_gate, conv_w, conv_b, lru_wa, lru_ba, lru_wx, lru_bx, lru_lam,
                    q_gain, k_gain, w_out, g_ffn, w_rg, b_rg, w_re, b_re):
    c_u, c_gr, c_q, c_k, c_v = (LRU_WIDTH, 2 * LRU_WIDTH, 2 * LRU_WIDTH + D_MODEL,
                                2 * LRU_WIDTH + D_MODEL + ATTN_KV, 2 * LRU_WIDTH + D_MODEL + 2 * ATTN_KV)
    w = w_in[l]
    w_nat = jnp.concatenate([w[:, :c_gr], w[:, c_q:c_k], w[:, c_v:]], axis=1).astype(BF16)
    w_tr = jnp.concatenate([w[:, c_gr:c_q], w[:, c_k:c_v]], axis=1).T.astype(BF16)
    cos, sin = _rope_tables(max_seq)
    pad = ROUTE_LANES - N_GROUPS - N_EXPERTS
    w_route = jnp.concatenate([w_rg[l], w_re[l], jnp.zeros((D_MODEL, pad), F32)], axis=1)
    w_route_hi = w_route.astype(BF16)
    w_route_lo = (w_route - w_route_hi.astype(F32)).astype(BF16)
    w_route = jnp.concatenate([w_route_hi, w_route_lo, w_route_hi], axis=0)
    b_route = jnp.concatenate([b_rg[l], b_re[l], jnp.zeros((pad,), F32)])[None, :]
    lru_w = (0.5 * jnp.concatenate([lru_wa[l], lru_wx[l]], axis=-1)).astype(BF16)
    score_bound = (SCORE_BOUND_SLACK * LOG2E * math.sqrt(HEAD_DIM)
                   * jnp.max(jnp.abs(q_gain[l])) * jnp.max(jnp.abs(k_gain[l]))).reshape(1)
    return dict(
        score_bound=score_bound,
        g_mix=g_mix[l][None, :], w_nat=w_nat, w_tr=w_tr, b_gate=b_gate[l][None, :],
        k_gain=k_gain[l][None, :], q_gain_t=jnp.broadcast_to(q_gain[l][:, None], (HEAD_DIM, tm)),
        cos=cos, sin=sin, cos_t=cos.T, sin_t=sin.T,
        conv_w=conv_w[l], conv_b=conv_b[l][None, :], lru_w=lru_w,
        lru_ba=lru_ba[l].reshape(2, 1, LRU_WIDTH), lru_bx=lru_bx[l].reshape(2, 1, LRU_WIDTH),
        lru_lam=lru_lam[l].reshape(2, 1, LRU_WIDTH),
        w_out=w_out[l].astype(BF16), g_ffn=g_ffn[l][None, :], w_route=w_route, b_route=b_route,
    )


def _mixer(x, p, cnt0, tm, tk):
    batch, seq_len, _ = x.shape
    x2 = x.reshape(batch * seq_len, D_MODEL)
    u, ggr, k, gate, qt, vt = _inproj(x2, seq_len, p, tm)
    h_fwd = _lru_scan(u, gate, ggr, None, p, 0, batch, seq_len, WIDE_TILE, reverse=False)
    m_a = _lru_scan(u, gate, ggr, h_fwd, p, 1, batch, seq_len, WIDE_TILE, reverse=True)
    m_b = _attention(p["score_bound"], qt, k, vt, gate, batch, seq_len, tm, min(tk, seq_len),
                     min(KV_BLOCK, seq_len))
    return _outproj_router(x2, m_a, m_b, cnt0, p, tm)


def _layer(xs_in, l, weights):
    (g_mix, w_in, b_gate, conv_w, conv_b, lru_wa, lru_ba, lru_wx, lru_bx, lru_lam,
     q_gain, k_gain, w_out, g_ffn, w_rg, b_rg, w_re, b_re, w_gate, w_up, w_down) = weights
    tm = ROW_TILE
    max_seq = max(x.shape[1] for x in xs_in)
    p = _prepare_params(l, max_seq, tm, g_mix, w_in, b_gate, conv_w, conv_b, lru_wa, lru_ba, lru_wx, lru_bx,
                        lru_lam, q_gain, k_gain, w_out, g_ffn, w_rg, b_rg, w_re, b_re)

    cnt = jnp.zeros((1, ROUTE_LANES), F32)
    x1s, h2s, routes, routes_t = [], [], [], []
    for x in xs_in:
        x1, h2, route, route_t, cnt = _mixer(x, p, cnt, tm, KV_CHUNK)
        x1s.append(x1)
        h2s.append(h2)
        routes.append(route)
        routes_t.append(route_t)

    n_tok = sum(x1.shape[0] for x1 in x1s)
    n_rows = n_tok * TOP_K + N_EXPERTS * MOE_ROWS
    n_blk = n_rows // MOE_ROWS
    counts = cnt[0, EXPERT_LANE0:EXPERT_LANE0 + N_EXPERTS].astype(jnp.int32)
    padded = (counts + MOE_ROWS - 1) // MOE_ROWS * MOE_ROWS
    pend = jnp.cumsum(padded)
    pstart = pend - padded
    blk_idx = jnp.arange(n_blk, dtype=jnp.int32)
    n_used = (pend[-1:] // MOE_ROWS).astype(jnp.int32)
    blk_e = jnp.minimum(jnp.sum((pend[None, :] <= (blk_idx * MOE_ROWS)[:, None]).astype(jnp.int32), axis=1),
                        N_EXPERTS - 1)
    blk_e = jnp.where(blk_idx < n_used, blk_e, blk_e[jnp.maximum(n_used[0] - 1, 0)])
    starts = jnp.concatenate([jnp.ones((1,), jnp.int32), (blk_e[1:] != blk_e[:-1]).astype(jnp.int32)])
    slot = (jnp.cumsum(starts) - 1) % 2
    later_start = lax.cummin(jnp.where(starts == 1, blk_idx, n_blk), reverse=True)
    next_start = jnp.concatenate([later_start[1:], jnp.full((1,), n_blk, jnp.int32)])
    nxt_e = jnp.where(next_start < n_blk, blk_e[jnp.minimum(next_start, n_blk - 1)], blk_e)

    dests = []
    xs = jnp.zeros((n_rows, PACKED_COLS), jnp.uint32)
    for h2, route_t in zip(h2s, routes_t):
        eid = route_t[0:TOP_K].astype(jnp.int32)
        rank = route_t[4:4 + TOP_K].astype(jnp.int32)
        experts = jnp.arange(N_EXPERTS, dtype=jnp.int32)[:, None, None]
        dest = (rank + jnp.sum(jnp.where(eid[None] == experts, pstart[:, None, None], 0), axis=0)).reshape(-1)
        dests.append(dest)
        xs = _dispatch(dest, h2, xs, WIDE_TILE)
    ys = _experts(blk_e, nxt_e, slot.astype(jnp.int32), n_used, xs, w_gate[l], w_up[l], w_down[l])
    outs = []
    for x, x1, route, dest in zip(xs_in, x1s, routes, dests):
        outs.append(_combine(dest, x1, route, ys, WIDE_TILE).reshape(x.shape))
    return outs


def kernel(x_prompt, x_sample, g_mix, w_in, b_gate, conv_w, conv_b, lru_wa, lru_ba, lru_wx, lru_bx, lru_lam,
           q_gain, k_gain, w_out, g_ffn, w_rg, b_rg, w_re, b_re, w_gate, w_up, w_down):
    weights = (g_mix, w_in, b_gate, conv_w, conv_b, lru_wa, lru_ba, lru_wx, lru_bx, lru_lam,
               q_gain, k_gain, w_out, g_ffn, w_rg, b_rg, w_re, b_re, w_gate, w_up, w_down)
    xs = [x_prompt, x_sample]
    for l in range(g_mix.shape[0]):
        xs = _layer(xs, l, weights)
    return tuple(xs)
```

```python
import functools
import math

import jax
import jax.numpy as jnp
from jax import lax
from jax.experimental import pallas as pl
from jax.experimental.pallas import tpu as pltpu

F32 = jnp.float32
BF16 = jnp.bfloat16

D_MODEL = 1024
N_HEADS = 8
N_KV_HEADS = 2
GROUP = N_HEADS // N_KV_HEADS
HEAD_DIM = D_MODEL // N_HEADS
AXIS_DIM = HEAD_DIM // 2
N_FREQ = AXIS_DIM // 2
GRID_W = 64
ROPE_THETA = 10000.0
LRU_WIDTH = D_MODEL
LRU_BLOCKS = 8
LRU_BLOCK_W = LRU_WIDTH // LRU_BLOCKS
CONV_W = 4
CONV_PAD_LEFT = 2
LRU_C = 8.0
N_GROUPS = 4
EXPERTS_PER_GROUP = 8
N_EXPERTS = N_GROUPS * EXPERTS_PER_GROUP
TOP_K = 2
D_EXPERT = D_MODEL // 2
ATTN_KV = N_KV_HEADS * HEAD_DIM
EPS = 1e-6

LANES = 128
SUBLANES = 8
VMEM_LIMIT_BYTES = 56 * 1024 * 1024

ROW_TILE = 256
WIDE_TILE = 512
KV_CHUNK = 512
KV_BLOCK = 16384
MOE_ROWS = 512
ROW_DMA_UNROLL = 8
ROUTE_LANES = LANES
ROUTE_ROWS = 64
PACKED_COLS = D_MODEL // 2
EXPERT_ROW0 = N_GROUPS

NAT_COLS = 2 * LRU_WIDTH + ATTN_KV + 2 * D_MODEL
TR_ROWS = D_MODEL + ATTN_KV
LOG2E = math.log2(math.e)
SAFE_EXP2_RANGE = 100.0
SCORE_BOUND_SLACK = 1.01


def _cparams(semantics):
    return pltpu.CompilerParams(dimension_semantics=semantics, vmem_limit_bytes=VMEM_LIMIT_BYTES)


def _const_spec(shape, single_buffer=False):
    nd = len(shape)
    mode = pl.Buffered(1) if single_buffer else None
    return pl.BlockSpec(shape, lambda *_: (0,) * nd, pipeline_mode=mode)


def _sigmoid(x):
    return 0.5 * jnp.tanh(0.5 * x) + 0.5


def _pack_bf16_pairs(x):
    n = x.shape[1] // 2
    lo = lax.bitcast_convert_type(x[:, :n].astype(BF16).astype(F32), jnp.uint32)
    hi = lax.bitcast_convert_type(x[:, n:].astype(BF16).astype(F32), jnp.uint32)
    return (hi & jnp.uint32(0xFFFF0000)) | (lo >> 16)


def _unpack_bf16_pairs(w):
    lo = lax.bitcast_convert_type(w << 16, F32)
    hi = lax.bitcast_convert_type(w & jnp.uint32(0xFFFF0000), F32)
    return jnp.concatenate([lo, hi], axis=1)


def _swap_halves_rows(x):
    return jnp.concatenate([x[N_FREQ:AXIS_DIM], x[0:N_FREQ], x[AXIS_DIM + N_FREQ:], x[AXIS_DIM:AXIS_DIM + N_FREQ]], axis=0)


def _inproj_kernel(x_ref, gmix_ref, wnat_ref, wtr_ref, bgate_ref, kgain_ref, qgain_ref,
                   cos_ref, sin_ref, cost_ref, sint_ref,
                   u_ref, ggr_ref, k_ref, gate_ref, qt_ref, vt_ref):
    tm = x_ref.shape[0]
    x = x_ref[...]
    h = (x * lax.rsqrt(jnp.mean(x * x, axis=-1, keepdims=True) + EPS) * gmix_ref[...]).astype(BF16)

    def nat(lo, hi):
        return jnp.dot(h, wnat_ref[:, lo:hi], preferred_element_type=F32)

    u_ref[...] = nat(0, LRU_WIDTH)
    ggr_ref[...] = jax.nn.gelu(nat(LRU_WIDTH, 2 * LRU_WIDTH))
    k0 = 2 * LRU_WIDTH
    kraw = nat(k0, k0 + ATTN_KV)
    g0 = k0 + ATTN_KV
    gate_ref[...] = _sigmoid(nat(g0, g0 + 2 * D_MODEL) + bgate_ref[...])

    cos = cos_ref[...]
    sin = sin_ref[...]
    lane = lax.broadcasted_iota(jnp.int32, (tm, HEAD_DIM), 1)
    first_half = (lane % AXIS_DIM) < N_FREQ
    for j in range(N_KV_HEADS):
        kj = kraw[:, j * HEAD_DIM:(j + 1) * HEAD_DIM]
        kn = kj * lax.rsqrt(jnp.mean(kj * kj, axis=-1, keepdims=True) + EPS) * kgain_ref[...]
        partner = jnp.where(first_half, pltpu.roll(kn, HEAD_DIM - N_FREQ, 1), pltpu.roll(kn, N_FREQ, 1))
        k_ref[:, j * HEAD_DIM:(j + 1) * HEAD_DIM] = (kn * cos + partner * sin).astype(BF16)

    zt = lax.dot_general(wtr_ref[...], h, (((1,), (1,)), ((), ())), preferred_element_type=F32)
    cost = cost_ref[...]
    sint = sint_ref[...]
    qgain = qgain_ref[...]
    qscale = (HEAD_DIM ** -0.5) * LOG2E
    for hd in range(N_HEADS):
        xq = zt[hd * HEAD_DIM:(hd + 1) * HEAD_DIM, :]
        xn = xq * lax.rsqrt(jnp.mean(xq * xq, axis=0, keepdims=True) + EPS) * qgain
        rot = (xn * cost + _swap_halves_rows(xn) * sint) * qscale
        g = hd % GROUP
        tq = qt_ref.shape[3] // GROUP
        for qb in range(tm // tq):
            qt_ref[qb, hd // GROUP, :, g * tq:(g + 1) * tq] = rot[:, qb * tq:(qb + 1) * tq].astype(BF16)
    for j in range(N_KV_HEADS):
        r0 = D_MODEL + j * HEAD_DIM
        vt_ref[j] = zt[r0:r0 + HEAD_DIM, :].astype(BF16)


def _inproj(x2, seq_len, p, tm, tq):
    t = x2.shape[0]
    n_pos = seq_len // tm
    grid = (t // tm,)
    row = lambda i: (i, 0)
    in_specs = [
        pl.BlockSpec((tm, D_MODEL), row),
        _const_spec((1, D_MODEL)),
        _const_spec((D_MODEL, NAT_COLS), single_buffer=True),
        _const_spec((TR_ROWS, D_MODEL), single_buffer=True),
        _const_spec((1, 2 * D_MODEL)),
        _const_spec((1, HEAD_DIM)),
        _const_spec((HEAD_DIM, tm)),
        pl.BlockSpec((tm, HEAD_DIM), lambda i: (i % n_pos, 0)),
        pl.BlockSpec((tm, HEAD_DIM), lambda i: (i % n_pos, 0)),
        pl.BlockSpec((HEAD_DIM, tm), lambda i: (0, i % n_pos)),
        pl.BlockSpec((HEAD_DIM, tm), lambda i: (0, i % n_pos)),
    ]
    out_shape = (
        jax.ShapeDtypeStruct((t, LRU_WIDTH), F32),
        jax.ShapeDtypeStruct((t, LRU_WIDTH), F32),
        jax.ShapeDtypeStruct((t, ATTN_KV), BF16),
        jax.ShapeDtypeStruct((t, 2 * D_MODEL), F32),
        jax.ShapeDtypeStruct((t // tq, N_KV_HEADS, HEAD_DIM, GROUP * tq), BF16),
        jax.ShapeDtypeStruct((N_KV_HEADS, HEAD_DIM, t), BF16),
    )
    out_specs = (
        pl.BlockSpec((tm, LRU_WIDTH), row),
        pl.BlockSpec((tm, LRU_WIDTH), row),
        pl.BlockSpec((tm, ATTN_KV), row),
        pl.BlockSpec((tm, 2 * D_MODEL), row),
        pl.BlockSpec((tm // tq, N_KV_HEADS, HEAD_DIM, GROUP * tq), lambda i: (i, 0, 0, 0)),
        pl.BlockSpec((N_KV_HEADS, HEAD_DIM, tm), lambda i: (0, 0, i)),
    )
    return pl.pallas_call(
        _inproj_kernel, grid=grid, in_specs=in_specs, out_specs=out_specs, out_shape=out_shape,
        compiler_params=_cparams(("arbitrary",)), name="inproj",
    )(x2, p["g_mix"], p["w_nat"], p["w_tr"], p["b_gate"], p["k_gain"], p["q_gain_t"],
      p["cos"], p["sin"], p["cos_t"], p["sin_t"])


def _lru_kernel(*refs, reverse, tm, nt):
    if reverse:
        (up_ref, uc_ref, un_ref, cw_ref, cb_ref, w_ref, ba_ref, bx_ref, lam_ref, hf_ref, ggr_ref, ga_ref,
         out_ref, ubuf, a_s, g_s, carry) = refs
    else:
        (up_ref, uc_ref, un_ref, cw_ref, cb_ref, w_ref, ba_ref, bx_ref, lam_ref,
         out_ref, ubuf, a_s, g_s, carry) = refs
    i = pl.program_id(1)
    ti = (nt - 1 - i) if reverse else i

    @pl.when(i == 0)
    def _():
        carry[...] = jnp.zeros_like(carry)

    ubuf[0:SUBLANES] = jnp.where(ti == 0, 0.0, up_ref[...])
    ubuf[SUBLANES:SUBLANES + tm] = uc_ref[...]
    ubuf[SUBLANES + tm:2 * SUBLANES + tm] = jnp.where(ti == nt - 1, 0.0, un_ref[...])
    cw = cw_ref[...]
    ub = ubuf[...]
    n_buf = tm + 2 * SUBLANES
    xc = cb_ref[...]
    for j in range(CONV_W):
        back = CONV_PAD_LEFT - j
        tap = ub if back == 0 else pltpu.roll(ub, back % n_buf, 0)
        xc = xc + cw[j:j + 1] * tap[SUBLANES:SUBLANES + tm]

    xcb = xc.astype(BF16)
    lam = lam_ref[...]
    half_c = (0.5 * LRU_C) * (jnp.minimum(lam, 0.0) - jnp.log1p(jnp.exp(-jnp.abs(lam))))
    half_ba = 0.5 * ba_ref[...]
    half_bx = 0.5 * bx_ref[...]
    half_xc = 0.5 * xc
    for hb in range(LRU_BLOCKS):
        sl = slice(hb * LRU_BLOCK_W, (hb + 1) * LRU_BLOCK_W)
        gz = jnp.dot(xcb[:, sl], w_ref[hb], preferred_element_type=F32)
        tr = jnp.tanh(gz[:, :LRU_BLOCK_W] + half_ba[:, sl])
        ti = jnp.tanh(gz[:, LRU_BLOCK_W:] + half_bx[:, sl])
        log_a = tr * half_c[:, sl] + half_c[:, sl]
        a = jnp.exp(log_a)
        a_s[:, sl] = a
        y = jnp.tanh(log_a) * (-1.0 - a * a)
        root = jnp.where(y > 0.0, y * lax.rsqrt(y), 0.0)
        g_s[:, sl] = root * (ti * half_xc[:, sl] + half_xc[:, sl])

    n_chunk = tm // SUBLANES
    srow = lax.broadcasted_iota(jnp.int32, (SUBLANES, LRU_WIDTH), 0)

    def chunk(c, h_prev):
        ci = (n_chunk - 1 - c) if reverse else c
        off = pl.multiple_of(ci * SUBLANES, SUBLANES)
        a = a_s[pl.ds(off, SUBLANES), :]
        b = g_s[pl.ds(off, SUBLANES), :]
        edge = (SUBLANES - 1) if reverse else 0
        b = b + jnp.where(srow == edge, a * h_prev, 0.0)
        for d in (1, 2, 4):
            shift = (SUBLANES - d) if reverse else d
            valid = (srow < SUBLANES - d) if reverse else (srow >= d)
            b = a * jnp.where(valid, pltpu.roll(b, shift, 0), 0.0) + b
            if d != 4:
                a = a * pltpu.roll(a, shift, 0)
        h = b
        if reverse:
            rows = pl.ds(off, SUBLANES)
            out_ref[rows, :] = ga_ref[rows, :] * (ggr_ref[rows, :] * (hf_ref[rows, :] + h))
            return h[0:1]
        out_ref[pl.ds(off, SUBLANES), :] = h
        return h[SUBLANES - 1:SUBLANES]

    carry[...] = lax.fori_loop(0, n_chunk, chunk, carry[...])


def _lru_scan(u, gate, ggr, h_fwd, p, d, batch, seq_len, tm, reverse):
    t = u.shape[0]
    nt = seq_len // tm
    per8 = tm // SUBLANES
    n8 = t // SUBLANES

    def tile(b, i):
        return b * nt + ((nt - 1 - i) if reverse else i)

    cur = lambda b, i: (tile(b, i), 0)
    prev = lambda b, i: (jnp.maximum(tile(b, i) * per8 - 1, 0), 0)
    nxt = lambda b, i: (jnp.minimum((tile(b, i) + 1) * per8, n8 - 1), 0)
    const2 = lambda b, i: (0, 0)
    in_specs = [
        pl.BlockSpec((SUBLANES, LRU_WIDTH), prev),
        pl.BlockSpec((tm, LRU_WIDTH), cur),
        pl.BlockSpec((SUBLANES, LRU_WIDTH), nxt),
        pl.BlockSpec((CONV_W, LRU_WIDTH), const2),
        pl.BlockSpec((1, LRU_WIDTH), const2),
        pl.BlockSpec((LRU_BLOCKS, LRU_BLOCK_W, 2 * LRU_BLOCK_W), lambda b, i: (0, 0, 0)),
        pl.BlockSpec((1, LRU_WIDTH), const2),
        pl.BlockSpec((1, LRU_WIDTH), const2),
        pl.BlockSpec((1, LRU_WIDTH), const2),
    ]
    args = [u, u, u, p["conv_w"], p["conv_b"], p["lru_w"][d], p["lru_ba"][d], p["lru_bx"][d], p["lru_lam"][d]]
    if reverse:
        in_specs += [pl.BlockSpec((tm, LRU_WIDTH), cur)] * 3
        args += [h_fwd, ggr, gate]
    return pl.pallas_call(
        functools.partial(_lru_kernel, reverse=reverse, tm=tm, nt=nt),
        grid=(batch, nt), in_specs=in_specs,
        out_specs=pl.BlockSpec((tm, LRU_WIDTH), cur),
        out_shape=jax.ShapeDtypeStruct((t, LRU_WIDTH), F32),
        scratch_shapes=[pltpu.VMEM((tm + 2 * SUBLANES, LRU_WIDTH), F32), pltpu.VMEM((tm, LRU_WIDTH), F32),
                        pltpu.VMEM((tm, LRU_WIDTH), F32), pltpu.VMEM((1, LRU_WIDTH), F32)],
        compiler_params=_cparams(("arbitrary", "arbitrary")), name="lru_bwd" if reverse else "lru_fwd",
    )(*args)


def _attn_kernel(bound_ref, qt_ref, k_ref, vt_ref, gb_ref, o_ref, m_s, l_s, acc_s, *, tq, tk, n_chunks, n_kv_blocks):
    nq_cols = GROUP * tq
    kv_blk = pl.program_id(3)
    bound = bound_ref[0]
    bounded = 2.0 * bound <= SAFE_EXP2_RANGE

    @pl.when(kv_blk == 0)
    def _():
        acc_s[...] = jnp.zeros_like(acc_s)
        l_s[...] = jnp.zeros_like(l_s)
        m_s[...] = jnp.full_like(m_s, -jnp.inf)

    @pl.when(bounded)
    def _():
        qt = qt_ref[0, 0]
        for c in range(n_chunks):
            rows = slice(c * tk, (c + 1) * tk)
            pr = jnp.exp2(jnp.dot(k_ref[rows, :], qt, preferred_element_type=F32) - bound)
            l_s[...] += jnp.sum(pr.reshape(tk // SUBLANES, SUBLANES, nq_cols), axis=0)
            acc_s[...] += jnp.dot(vt_ref[0, :, rows], pr.astype(BF16), preferred_element_type=F32)

    @pl.when(jnp.logical_not(bounded))
    def _():
        qt = qt_ref[0, 0]

        def chunk(c, _):
            off = pl.multiple_of(c * tk, tk)
            s = jnp.dot(k_ref[pl.ds(off, tk), :], qt, preferred_element_type=F32)
            m_old = m_s[...]
            m_new = jnp.maximum(m_old, jnp.max(s, axis=0, keepdims=True))
            alpha = jnp.exp2(m_old - m_new)
            pr = jnp.exp2(s - m_new)
            l_s[...] = alpha * l_s[...] + jnp.sum(pr.reshape(tk // SUBLANES, SUBLANES, nq_cols), axis=0)
            acc_s[...] = alpha * acc_s[...] + jnp.dot(vt_ref[0, :, pl.ds(off, tk)], pr.astype(BF16),
                                                      preferred_element_type=F32)
            m_s[...] = m_new
            return 0

        lax.fori_loop(0, n_chunks, chunk, 0)

    @pl.when(kv_blk == n_kv_blocks - 1)
    def _():
        out_t = acc_s[...] / jnp.sum(l_s[...], axis=0, keepdims=True)
        for g in range(GROUP):
            cols = slice(g * HEAD_DIM, (g + 1) * HEAD_DIM)
            o_ref[:, cols] = gb_ref[:, cols] * out_t[:, g * tq:(g + 1) * tq].T


def _attention(bound, qt, k, vt, gate, batch, seq_len, tq, tk, kv_block):
    t = k.shape[0]
    nq = seq_len // tq
    nkv = seq_len // kv_block
    half = GROUP * HEAD_DIM
    gate_col0 = D_MODEL // half
    return pl.pallas_call(
        functools.partial(_attn_kernel, tq=tq, tk=tk, n_chunks=kv_block // tk, n_kv_blocks=nkv),
        grid=(batch, N_KV_HEADS, nq, nkv),
        in_specs=[
            pl.BlockSpec(memory_space=pltpu.SMEM),
            pl.BlockSpec((1, 1, HEAD_DIM, GROUP * tq), lambda b, j, i, c: (b * nq + i, j, 0, 0)),
            pl.BlockSpec((kv_block, HEAD_DIM), lambda b, j, i, c: (b * nkv + c, j)),
            pl.BlockSpec((1, HEAD_DIM, kv_block), lambda b, j, i, c: (j, 0, b * nkv + c)),
            pl.BlockSpec((tq, half), lambda b, j, i, c: (b * nq + i, gate_col0 + j)),
        ],
        out_specs=pl.BlockSpec((tq, half), lambda b, j, i, c: (b * nq + i, j)),
        out_shape=jax.ShapeDtypeStruct((t, D_MODEL), F32),
        scratch_shapes=[pltpu.VMEM((1, GROUP * tq), F32), pltpu.VMEM((SUBLANES, GROUP * tq), F32),
                        pltpu.VMEM((HEAD_DIM, GROUP * tq), F32)],
        compiler_params=_cparams(("arbitrary", "arbitrary", "arbitrary", "arbitrary")), name="attention",
    )(bound, qt, k, vt, gate)


def _outproj_router_kernel(x_ref, ma_ref, mb_ref, wout_ref, gffn_ref, wr_ref, br_ref, cnt0_ref,
                           x1_ref, h2_ref, route_ref, route_t_ref, cnt_ref, cnt_s):
    tm = x_ref.shape[0]
    i = pl.program_id(0)

    @pl.when(i == 0)
    def _():
        cnt_s[...] = cnt0_ref[...]

    merged = (ma_ref[...] + mb_ref[...]).astype(BF16)
    x1 = x_ref[...] + jnp.dot(merged, wout_ref[...], preferred_element_type=F32)
    x1_ref[...] = x1
    h2 = x1 * lax.rsqrt(jnp.mean(x1 * x1, axis=-1, keepdims=True) + EPS) * gffn_ref[...]
    h2_ref[...] = _pack_bf16_pairs(h2)

    hi = h2.astype(BF16)
    lo = (h2 - hi.astype(F32)).astype(BF16)
    zt = lax.dot_general(wr_ref[...], jnp.concatenate([hi, hi, lo], axis=1), (((1,), (1,)), ((), ())),
                         preferred_element_type=F32) + br_ref[...]
    sub = lax.broadcasted_iota(jnp.int32, (ROUTE_ROWS, tm), 0)
    neg = -jnp.inf

    def first_argmax(v):
        m = jnp.max(v, axis=0, keepdims=True)
        return m, jnp.min(jnp.where(v == m, sub, ROUTE_ROWS), axis=0, keepdims=True)

    zg = jnp.where(sub < N_GROUPS, zt, neg)
    mg, grp = first_argmax(zg)
    p_sel = 1.0 / jnp.sum(jnp.exp(zg - mg), axis=0, keepdims=True)
    e_lo = EXPERT_ROW0 + EXPERTS_PER_GROUP * grp
    ze = jnp.where((sub >= e_lo) & (sub < e_lo + EXPERTS_PER_GROUP), zt, neg)
    m1, i1 = first_argmax(ze)
    m2, i2 = first_argmax(jnp.where(sub == i1, neg, ze))
    e2 = jnp.exp(m2 - m1)
    w0 = p_sel / (1.0 + e2)
    w1 = p_sel * e2 / (1.0 + e2)

    hot0 = sub == i1
    hot1 = sub == i2
    hot = jnp.where(hot0 | hot1, 1.0, 0.0)
    rr = lax.broadcasted_iota(jnp.int32, (tm, tm), 0)
    cc = lax.broadcasted_iota(jnp.int32, (tm, tm), 1)
    earlier = jnp.where(rr < cc, 1.0, 0.0).astype(BF16)
    cnt = cnt_s[...]
    before = (jnp.dot(hot.astype(BF16), earlier, preferred_element_type=F32)
              + jnp.concatenate([cnt] * (tm // LANES), axis=1))
    rank0 = jnp.sum(jnp.where(hot0, before, 0.0), axis=0, keepdims=True)
    rank1 = jnp.sum(jnp.where(hot1, before, 0.0), axis=0, keepdims=True)
    cnt_s[...] = cnt + jnp.sum(hot, axis=1, keepdims=True)
    cnt_ref[...] = cnt_s[...]

    eid0 = (i1 - EXPERT_ROW0).astype(F32)
    eid1 = (i2 - EXPERT_ROW0).astype(F32)
    zero = jnp.zeros_like(w0)
    rec_t = jnp.concatenate([eid0, eid1, w0, w1, rank0, rank1, zero, zero], axis=0)
    route_t_ref[...] = rec_t
    route_ref[...] = jnp.concatenate([rec_t, jnp.zeros((ROUTE_LANES - SUBLANES, tm), F32)], axis=0).T


def _outproj_router(x2, m_a, m_b, cnt0, p, tm):
    t = x2.shape[0]
    row = lambda i: (i, 0)
    return pl.pallas_call(
        _outproj_router_kernel, grid=(t // tm,),
        in_specs=[pl.BlockSpec((tm, D_MODEL), row), pl.BlockSpec((tm, D_MODEL), row), pl.BlockSpec((tm, D_MODEL), row),
                  _const_spec((D_MODEL, D_MODEL)), _const_spec((1, D_MODEL)), _const_spec((ROUTE_ROWS, 3 * D_MODEL)),
                  _const_spec((ROUTE_ROWS, tm)), _const_spec((ROUTE_ROWS, LANES))],
        out_specs=(pl.BlockSpec((tm, D_MODEL), row), pl.BlockSpec((tm, PACKED_COLS), row),
                   pl.BlockSpec((tm, ROUTE_LANES), row), pl.BlockSpec((SUBLANES, tm), lambda i: (0, i)),
                   _const_spec((ROUTE_ROWS, LANES))),
        out_shape=(jax.ShapeDtypeStruct((t, D_MODEL), F32), jax.ShapeDtypeStruct((t, PACKED_COLS), jnp.uint32),
                   jax.ShapeDtypeStruct((t, ROUTE_LANES), F32), jax.ShapeDtypeStruct((SUBLANES, t), F32),
                   jax.ShapeDtypeStruct((ROUTE_ROWS, LANES), F32)),
        scratch_shapes=[pltpu.VMEM((ROUTE_ROWS, LANES), F32)],
        compiler_params=_cparams(("arbitrary",)), name="outproj_router",
    )(x2, m_a, m_b, p["w_out"], p["g_ffn"], p["w_route"], p["b_route"], cnt0)


def _dispatch_kernel(dest_ref, h_ref, xs_in_ref, xs_ref, sem, *, tm, n_tok):
    del xs_in_ref
    base = pl.program_id(0) * tm

    def issue(r, _):
        for k in range(TOP_K):
            d = dest_ref[k * n_tok + base + r]
            pltpu.make_async_copy(h_ref.at[pl.ds(r, 1), :], xs_ref.at[pl.ds(d, 1), :], sem).start()
        return 0

    lax.fori_loop(0, tm, issue, 0, unroll=ROW_DMA_UNROLL)
    for k in range(TOP_K):
        pltpu.make_async_copy(h_ref, xs_ref.at[pl.ds(0, tm), :], sem).wait()


def _dispatch(dest_flat, h2, xs, tm):
    t = h2.shape[0]
    return pl.pallas_call(
        functools.partial(_dispatch_kernel, tm=tm, n_tok=t),
        grid_spec=pltpu.PrefetchScalarGridSpec(
            num_scalar_prefetch=1, grid=(t // tm,),
            in_specs=[pl.BlockSpec((tm, PACKED_COLS), lambda i, d: (i, 0)), pl.BlockSpec(memory_space=pl.ANY)],
            out_specs=pl.BlockSpec(memory_space=pl.ANY),
            scratch_shapes=[pltpu.SemaphoreType.DMA]),
        out_shape=jax.ShapeDtypeStruct(xs.shape, xs.dtype),
        input_output_aliases={2: 0},
        compiler_params=_cparams(("arbitrary",)), name="moe_dispatch",
    )(dest_flat, h2, xs)


def _experts_kernel(blk_e_ref, nxt_e_ref, slot_ref, n_used_ref, x_ref, wg_hbm, wu_hbm, wd_hbm, y_ref,
                    wg_f, wu_f, wd_f, wg_s, wu_s, wd_s, sems):
    i = pl.program_id(0)
    e = blk_e_ref[i]
    slot = slot_ref[i]
    first = jnp.logical_or(i == 0, e != blk_e_ref[jnp.maximum(i - 1, 0)])

    def weight_copies(expert, s):
        return (pltpu.make_async_copy(wg_hbm.at[expert], wg_f.at[s], sems.at[s, 0]),
                pltpu.make_async_copy(wu_hbm.at[expert], wu_f.at[s], sems.at[s, 1]),
                pltpu.make_async_copy(wd_hbm.at[expert], wd_f.at[s], sems.at[s, 2]))

    @pl.when(i == 0)
    def _():
        for cp in weight_copies(e, slot):
            cp.start()

    @pl.when(first)
    def _():
        for cp in weight_copies(e, slot):
            cp.wait()
        wg_s[...] = wg_f[slot].astype(BF16)
        wu_s[...] = wu_f[slot].astype(BF16)
        wd_s[...] = wd_f[slot].astype(BF16)
        nxt = nxt_e_ref[i]

        @pl.when(nxt != e)
        def _():
            for cp in weight_copies(nxt, 1 - slot):
                cp.start()

    @pl.when(i < n_used_ref[0])
    def _():
        xb = _unpack_bf16_pairs(x_ref[...]).astype(BF16)
        gate = jnp.dot(xb, wg_s[...], preferred_element_type=F32)
        up = jnp.dot(xb, wu_s[...], preferred_element_type=F32)
        act = (jax.nn.silu(gate) * up).astype(BF16)
        y_ref[...] = _pack_bf16_pairs(jnp.dot(act, wd_s[...], preferred_element_type=F32))

    @pl.when(i >= n_used_ref[0])
    def _():
        y_ref[...] = jnp.zeros_like(y_ref)


def _experts(blk_e, nxt_e, slot, n_used, xs, w_gate, w_up, w_down):
    n_rows = xs.shape[0]
    rb = MOE_ROWS
    row_blk = lambda i, *_: (i, 0)
    return pl.pallas_call(
        _experts_kernel,
        grid_spec=pltpu.PrefetchScalarGridSpec(
            num_scalar_prefetch=4, grid=(n_rows // rb,),
            in_specs=[pl.BlockSpec((rb, PACKED_COLS), row_blk),
                      pl.BlockSpec(memory_space=pl.ANY), pl.BlockSpec(memory_space=pl.ANY),
                      pl.BlockSpec(memory_space=pl.ANY)],
            out_specs=pl.BlockSpec((rb, PACKED_COLS), row_blk),
            scratch_shapes=[pltpu.VMEM((2, D_MODEL, D_EXPERT), F32), pltpu.VMEM((2, D_MODEL, D_EXPERT), F32),
                            pltpu.VMEM((2, D_EXPERT, D_MODEL), F32),
                            pltpu.VMEM((D_MODEL, D_EXPERT), BF16), pltpu.VMEM((D_MODEL, D_EXPERT), BF16),
                            pltpu.VMEM((D_EXPERT, D_MODEL), BF16), pltpu.SemaphoreType.DMA((2, 3))]),
        out_shape=jax.ShapeDtypeStruct((n_rows, PACKED_COLS), jnp.uint32),
        compiler_params=_cparams(("arbitrary",)), name="moe_experts",
    )(blk_e, nxt_e, slot, n_used, xs, w_gate, w_up, w_down)


def _combine_kernel(dest_ref, x1_ref, route_ref, ys_ref, o_ref, gbuf, sem, *, tm, n_tok):
    base = pl.program_id(0) * tm

    def issue(r, _):
        for k in range(TOP_K):
            d = dest_ref[k * n_tok + base + r]
            pltpu.make_async_copy(ys_ref.at[pl.ds(d, 1), :], gbuf.at[k, pl.ds(r, 1), :], sem).start()
        return 0

    lax.fori_loop(0, tm, issue, 0, unroll=ROW_DMA_UNROLL)
    for k in range(TOP_K):
        pltpu.make_async_copy(ys_ref.at[pl.ds(0, tm), :], gbuf.at[k], sem).wait()
    w0 = route_ref[:, 2:3]
    w1 = route_ref[:, 3:4]
    o_ref[...] = x1_ref[...] + (_unpack_bf16_pairs(gbuf[0]) * w0 + _unpack_bf16_pairs(gbuf[1]) * w1)


def _combine(dest_flat, x1, route, ys, tm):
    t = x1.shape[0]
    return pl.pallas_call(
        functools.partial(_combine_kernel, tm=tm, n_tok=t),
        grid_spec=pltpu.PrefetchScalarGridSpec(
            num_scalar_prefetch=1, grid=(t // tm,),
            in_specs=[pl.BlockSpec((tm, D_MODEL), lambda i, d: (i, 0)),
                      pl.BlockSpec((tm, ROUTE_LANES), lambda i, d: (i, 0)),
                      pl.BlockSpec(memory_space=pl.ANY)],
            out_specs=pl.BlockSpec((tm, D_MODEL), lambda i, d: (i, 0)),
            scratch_shapes=[pltpu.VMEM((TOP_K, tm, PACKED_COLS), jnp.uint32), pltpu.SemaphoreType.DMA]),
        out_shape=jax.ShapeDtypeStruct((t, D_MODEL), F32),
        compiler_params=_cparams(("arbitrary",)), name="moe_combine",
    )(dest_flat, x1, route, ys)


def _rope_tables(seq_len):
    inv = ROPE_THETA ** (-jnp.arange(0, AXIS_DIM, 2, dtype=F32) / AXIS_DIM)
    n_grid_rows = seq_len // GRID_W
    ang_r = jnp.arange(n_grid_rows, dtype=F32)[:, None] * inv
    ang_c = jnp.arange(GRID_W, dtype=F32)[:, None] * inv
    by_row = lambda a: jnp.repeat(a, GRID_W, axis=0)
    by_col = lambda a: jnp.tile(a, (n_grid_rows, 1))
    cos_r, sin_r, cos_c, sin_c = by_row(jnp.cos(ang_r)), by_row(jnp.sin(ang_r)), by_col(jnp.cos(ang_c)), by_col(jnp.sin(ang_c))
    cos = jnp.concatenate([cos_r, cos_r, cos_c, cos_c], axis=-1)
    sin = jnp.concatenate([-sin_r, sin_r, -sin_c, sin_c], axis=-1)
    return cos, sin


def _prepare_params(l, max_seq, tm, g_mix, w_in, b_gate, conv_w, conv_b, lru_wa, lru_ba, lru_wx, lru_bx, lru_lam,
                    q_gain, k_gain, w_out, g_ffn, w_rg, b_rg, w_re, b_re):
    c_u, c_gr, c_q, c_k, c_v = (LRU_WIDTH, 2 * LRU_WIDTH, 2 * LRU_WIDTH + D_MODEL,
                                2 * LRU_WIDTH + D_MODEL + ATTN_KV, 2 * LRU_WIDTH + D_MODEL + 2 * ATTN_KV)
    w = w_in[l]
    w_nat = jnp.concatenate([w[:, :c_gr], w[:, c_q:c_k], w[:, c_v:]], axis=1).astype(BF16)
    w_tr = jnp.concatenate([w[:, c_gr:c_q], w[:, c_k:c_v]], axis=1).T.astype(BF16)
    cos, sin = _rope_tables(max_seq)
    pad = ROUTE_ROWS - N_GROUPS - N_EXPERTS
    w_route = jnp.concatenate([w_rg[l], w_re[l], jnp.zeros((D_MODEL, pad), F32)], axis=1).T
    w_route_hi = w_route.astype(BF16)
    w_route_lo = (w_route - w_route_hi.astype(F32)).astype(BF16)
    w_route = jnp.concatenate([w_route_hi, w_route_lo, w_route_hi], axis=1)
    b_route = jnp.broadcast_to(jnp.concatenate([b_rg[l], b_re[l], jnp.zeros((pad,), F32)])[:, None], (ROUTE_ROWS, tm))
    lru_w = (0.5 * jnp.concatenate([lru_wa[l], lru_wx[l]], axis=-1)).astype(BF16)
    score_bound = (SCORE_BOUND_SLACK * LOG2E * math.sqrt(HEAD_DIM)
                   * jnp.max(jnp.abs(q_gain[l])) * jnp.max(jnp.abs(k_gain[l]))).reshape(1)
    return dict(
        score_bound=score_bound,
        g_mix=g_mix[l][None, :], w_nat=w_nat, w_tr=w_tr, b_gate=b_gate[l][None, :],
        k_gain=k_gain[l][None, :], q_gain_t=jnp.broadcast_to(q_gain[l][:, None], (HEAD_DIM, WIDE_TILE)),
        cos=cos, sin=sin, cos_t=cos.T, sin_t=sin.T,
        conv_w=conv_w[l], conv_b=conv_b[l][None, :], lru_w=lru_w,
        lru_ba=lru_ba[l].reshape(2, 1, LRU_WIDTH), lru_bx=lru_bx[l].reshape(2, 1, LRU_WIDTH),
        lru_lam=lru_lam[l].reshape(2, 1, LRU_WIDTH),
        w_out=w_out[l].astype(BF16), g_ffn=g_ffn[l][None, :], w_route=w_route, b_route=b_route,
    )


def _mixer(x, p, cnt0, tm, tk):
    batch, seq_len, _ = x.shape
    x2 = x.reshape(batch * seq_len, D_MODEL)
    u, ggr, k, gate, qt, vt = _inproj(x2, seq_len, p, WIDE_TILE, tm)
    h_fwd = _lru_scan(u, gate, ggr, None, p, 0, batch, seq_len, WIDE_TILE, reverse=False)
    m_a = _lru_scan(u, gate, ggr, h_fwd, p, 1, batch, seq_len, WIDE_TILE, reverse=True)
    m_b = _attention(p["score_bound"], qt, k, vt, gate, batch, seq_len, tm, min(tk, seq_len),
                     min(KV_BLOCK, seq_len))
    return _outproj_router(x2, m_a, m_b, cnt0, p, tm)


def _layer(xs_in, l, weights):
    (g_mix, w_in, b_gate, conv_w, conv_b, lru_wa, lru_ba, lru_wx, lru_bx, lru_lam,
     q_gain, k_gain, w_out, g_ffn, w_rg, b_rg, w_re, b_re, w_gate, w_up, w_down) = weights
    tm = ROW_TILE
    max_seq = max(x.shape[1] for x in xs_in)
    p = _prepare_params(l, max_seq, tm, g_mix, w_in, b_gate, conv_w, conv_b, lru_wa, lru_ba, lru_wx, lru_bx,
                        lru_lam, q_gain, k_gain, w_out, g_ffn, w_rg, b_rg, w_re, b_re)

    cnt = jnp.zeros((ROUTE_ROWS, LANES), F32)
    x1s, h2s, routes, routes_t = [], [], [], []
    for x in xs_in:
        x1, h2, route, route_t, cnt = _mixer(x, p, cnt, tm, KV_CHUNK)
        x1s.append(x1)
        h2s.append(h2)
        routes.append(route)
        routes_t.append(route_t)

    n_tok = sum(x1.shape[0] for x1 in x1s)
    n_rows = n_tok * TOP_K + N_EXPERTS * MOE_ROWS
    n_blk = n_rows // MOE_ROWS
    counts = cnt[EXPERT_ROW0:EXPERT_ROW0 + N_EXPERTS, 0].astype(jnp.int32)
    padded = (counts + MOE_ROWS - 1) // MOE_ROWS * MOE_ROWS
    pend = jnp.cumsum(padded)
    pstart = pend - padded
    blk_idx = jnp.arange(n_blk, dtype=jnp.int32)
    n_used = (pend[-1:] // MOE_ROWS).astype(jnp.int32)
    blk_e = jnp.minimum(jnp.sum((pend[None, :] <= (blk_idx * MOE_ROWS)[:, None]).astype(jnp.int32), axis=1),
                        N_EXPERTS - 1)
    blk_e = jnp.where(blk_idx < n_used, blk_e, blk_e[jnp.maximum(n_used[0] - 1, 0)])
    starts = jnp.concatenate([jnp.ones((1,), jnp.int32), (blk_e[1:] != blk_e[:-1]).astype(jnp.int32)])
    slot = (jnp.cumsum(starts) - 1) % 2
    later_start = lax.cummin(jnp.where(starts == 1, blk_idx, n_blk), reverse=True)
    next_start = jnp.concatenate([later_start[1:], jnp.full((1,), n_blk, jnp.int32)])
    nxt_e = jnp.where(next_start < n_blk, blk_e[jnp.minimum(next_start, n_blk - 1)], blk_e)

    dests = []
    xs = jnp.zeros((n_rows, PACKED_COLS), jnp.uint32)
    for h2, route_t in zip(h2s, routes_t):
        eid = route_t[0:TOP_K].astype(jnp.int32)
        rank = route_t[4:4 + TOP_K].astype(jnp.int32)
        experts = jnp.arange(N_EXPERTS, dtype=jnp.int32)[:, None, None]
        dest = (rank + jnp.sum(jnp.where(eid[None] == experts, pstart[:, None, None], 0), axis=0)).reshape(-1)
        dests.append(dest)
        xs = _dispatch(dest, h2, xs, WIDE_TILE)
    ys = _experts(blk_e, nxt_e, slot.astype(jnp.int32), n_used, xs, w_gate[l], w_up[l], w_down[l])
    outs = []
    for x, x1, route, dest in zip(xs_in, x1s, routes, dests):
        outs.append(_combine(dest, x1, route, ys, WIDE_TILE).reshape(x.shape))
    return outs


def kernel(x_prompt, x_sample, g_mix, w_in, b_gate, conv_w, conv_b, lru_wa, lru_ba, lru_wx, lru_bx, lru_lam,
           q_gain, k_gain, w_out, g_ffn, w_rg, b_rg, w_re, b_re, w_gate, w_up, w_down):
    weights = (g_mix, w_in, b_gate, conv_w, conv_b, lru_wa, lru_ba, lru_wx, lru_bx, lru_lam,
               q_gain, k_gain, w_out, g_ffn, w_rg, b_rg, w_re, b_re, w_gate, w_up, w_down)
    xs = [x_prompt, x_sample]
    for l in range(g_mix.shape[0]):
        xs = _layer(xs, l, weights)
    return tuple(xs)
```

```python
import functools
import math

import jax
import jax.numpy as jnp
from jax import lax
from jax.experimental import pallas as pl
from jax.experimental.pallas import tpu as pltpu

F32 = jnp.float32
BF16 = jnp.bfloat16

D_MODEL = 1024
N_HEADS = 8
N_KV_HEADS = 2
GROUP = N_HEADS // N_KV_HEADS
HEAD_DIM = D_MODEL // N_HEADS
AXIS_DIM = HEAD_DIM // 2
N_FREQ = AXIS_DIM // 2
GRID_W = 64
ROPE_THETA = 10000.0
LRU_WIDTH = D_MODEL
LRU_BLOCKS = 8
LRU_BLOCK_W = LRU_WIDTH // LRU_BLOCKS
CONV_W = 4
CONV_PAD_LEFT = 2
LRU_C = 8.0
N_GROUPS = 4
EXPERTS_PER_GROUP = 8
N_EXPERTS = N_GROUPS * EXPERTS_PER_GROUP
TOP_K = 2
D_EXPERT = D_MODEL // 2
ATTN_KV = N_KV_HEADS * HEAD_DIM
EPS = 1e-6

LANES = 128
SUBLANES = 8
VMEM_LIMIT_BYTES = 56 * 1024 * 1024

ROW_TILE = 256
WIDE_TILE = 512
KV_CHUNK = 1024
KV_BLOCK = 16384
MOE_ROWS = 512
ROW_DMA_UNROLL = 8
ROUTE_LANES = LANES
ROUTE_ROWS = 64
PACKED_COLS = D_MODEL // 2
EXPERT_ROW0 = N_GROUPS

NAT_COLS = 2 * LRU_WIDTH + ATTN_KV + 2 * D_MODEL
TR_ROWS = D_MODEL + ATTN_KV
LOG2E = math.log2(math.e)
SAFE_EXP2_RANGE = 100.0
SCORE_BOUND_SLACK = 1.01


def _cparams(semantics):
    return pltpu.CompilerParams(dimension_semantics=semantics, vmem_limit_bytes=VMEM_LIMIT_BYTES)


def _const_spec(shape, single_buffer=False):
    nd = len(shape)
    mode = pl.Buffered(1) if single_buffer else None
    return pl.BlockSpec(shape, lambda *_: (0,) * nd, pipeline_mode=mode)


def _sigmoid(x):
    return 0.5 * jnp.tanh(0.5 * x) + 0.5


def _pack_bf16_pairs(x):
    n = x.shape[1] // 2
    lo = lax.bitcast_convert_type(x[:, :n].astype(BF16).astype(F32), jnp.uint32)
    hi = lax.bitcast_convert_type(x[:, n:].astype(BF16).astype(F32), jnp.uint32)
    return (hi & jnp.uint32(0xFFFF0000)) | (lo >> 16)


def _unpack_bf16_pairs(w):
    lo = lax.bitcast_convert_type(w << 16, F32)
    hi = lax.bitcast_convert_type(w & jnp.uint32(0xFFFF0000), F32)
    return jnp.concatenate([lo, hi], axis=1)


def _swap_halves_rows(x):
    return jnp.concatenate([x[N_FREQ:AXIS_DIM], x[0:N_FREQ], x[AXIS_DIM + N_FREQ:], x[AXIS_DIM:AXIS_DIM + N_FREQ]], axis=0)


def _inproj_kernel(x_ref, gmix_ref, wnat_ref, wtr_ref, bgate_ref, kgain_ref, qgain_ref,
                   cos_ref, sin_ref, cost_ref, sint_ref,
                   u_ref, ggr_ref, k_ref, gate_ref, qt_ref, vt_ref):
    tm = x_ref.shape[0]
    x = x_ref[...]
    h = (x * lax.rsqrt(jnp.mean(x * x, axis=-1, keepdims=True) + EPS) * gmix_ref[...]).astype(BF16)

    def nat(lo, hi):
        return jnp.dot(h, wnat_ref[:, lo:hi], preferred_element_type=F32)

    u_ref[...] = nat(0, LRU_WIDTH)
    ggr_ref[...] = jax.nn.gelu(nat(LRU_WIDTH, 2 * LRU_WIDTH))
    k0 = 2 * LRU_WIDTH
    kraw = nat(k0, k0 + ATTN_KV)
    g0 = k0 + ATTN_KV
    gate_ref[...] = _sigmoid(nat(g0, g0 + 2 * D_MODEL) + bgate_ref[...])

    cos = cos_ref[...]
    sin = sin_ref[...]
    lane = lax.broadcasted_iota(jnp.int32, (tm, HEAD_DIM), 1)
    first_half = (lane % AXIS_DIM) < N_FREQ
    for j in range(N_KV_HEADS):
        kj = kraw[:, j * HEAD_DIM:(j + 1) * HEAD_DIM]
        kn = kj * lax.rsqrt(jnp.mean(kj * kj, axis=-1, keepdims=True) + EPS) * kgain_ref[...]
        partner = jnp.where(first_half, pltpu.roll(kn, HEAD_DIM - N_FREQ, 1), pltpu.roll(kn, N_FREQ, 1))
        k_ref[:, j * HEAD_DIM:(j + 1) * HEAD_DIM] = (kn * cos + partner * sin).astype(BF16)

    zt = lax.dot_general(wtr_ref[...], h, (((1,), (1,)), ((), ())), preferred_element_type=F32)
    cost = cost_ref[...]
    sint = sint_ref[...]
    qgain = qgain_ref[...]
    qscale = (HEAD_DIM ** -0.5) * LOG2E
    for hd in range(N_HEADS):
        xq = zt[hd * HEAD_DIM:(hd + 1) * HEAD_DIM, :]
        xn = xq * lax.rsqrt(jnp.mean(xq * xq, axis=0, keepdims=True) + EPS) * qgain
        rot = (xn * cost + _swap_halves_rows(xn) * sint) * qscale
        g = hd % GROUP
        tq = qt_ref.shape[3] // GROUP
        for qb in range(tm // tq):
            qt_ref[qb, hd // GROUP, :, g * tq:(g + 1) * tq] = rot[:, qb * tq:(qb + 1) * tq].astype(BF16)
    for j in range(N_KV_HEADS):
        r0 = D_MODEL + j * HEAD_DIM
        vt_ref[j] = zt[r0:r0 + HEAD_DIM, :].astype(BF16)


def _inproj(x2, seq_len, p, tm, tq):
    t = x2.shape[0]
    n_pos = seq_len // tm
    grid = (t // tm,)
    row = lambda i: (i, 0)
    in_specs = [
        pl.BlockSpec((tm, D_MODEL), row),
        _const_spec((1, D_MODEL)),
        _const_spec((D_MODEL, NAT_COLS), single_buffer=True),
        _const_spec((TR_ROWS, D_MODEL), single_buffer=True),
        _const_spec((1, 2 * D_MODEL)),
        _const_spec((1, HEAD_DIM)),
        _const_spec((HEAD_DIM, tm)),
        pl.BlockSpec((tm, HEAD_DIM), lambda i: (i % n_pos, 0)),
        pl.BlockSpec((tm, HEAD_DIM), lambda i: (i % n_pos, 0)),
        pl.BlockSpec((HEAD_DIM, tm), lambda i: (0, i % n_pos)),
        pl.BlockSpec((HEAD_DIM, tm), lambda i: (0, i % n_pos)),
    ]
    out_shape = (
        jax.ShapeDtypeStruct((t, LRU_WIDTH), F32),
        jax.ShapeDtypeStruct((t, LRU_WIDTH), F32),
        jax.ShapeDtypeStruct((t, ATTN_KV), BF16),
        jax.ShapeDtypeStruct((t, 2 * D_MODEL), F32),
        jax.ShapeDtypeStruct((t // tq, N_KV_HEADS, HEAD_DIM, GROUP * tq), BF16),
        jax.ShapeDtypeStruct((N_KV_HEADS, HEAD_DIM, t), BF16),
    )
    out_specs = (
        pl.BlockSpec((tm, LRU_WIDTH), row),
        pl.BlockSpec((tm, LRU_WIDTH), row),
        pl.BlockSpec((tm, ATTN_KV), row),
        pl.BlockSpec((tm, 2 * D_MODEL), row),
        pl.BlockSpec((tm // tq, N_KV_HEADS, HEAD_DIM, GROUP * tq), lambda i: (i, 0, 0, 0)),
        pl.BlockSpec((N_KV_HEADS, HEAD_DIM, tm), lambda i: (0, 0, i)),
    )
    return pl.pallas_call(
        _inproj_kernel, grid=grid, in_specs=in_specs, out_specs=out_specs, out_shape=out_shape,
        compiler_params=_cparams(("arbitrary",)), name="inproj",
    )(x2, p["g_mix"], p["w_nat"], p["w_tr"], p["b_gate"], p["k_gain"], p["q_gain_t"],
      p["cos"], p["sin"], p["cos_t"], p["sin_t"])


def _lru_kernel(*refs, reverse, tm, nt):
    if reverse:
        (up_ref, uc_ref, un_ref, cw_ref, cb_ref, w_ref, ba_ref, bx_ref, lam_ref, hf_ref, ggr_ref, ga_ref,
         out_ref, ubuf, a_s, g_s, carry) = refs
    else:
        (up_ref, uc_ref, un_ref, cw_ref, cb_ref, w_ref, ba_ref, bx_ref, lam_ref,
         out_ref, ubuf, a_s, g_s, carry) = refs
    i = pl.program_id(1)
    ti = (nt - 1 - i) if reverse else i

    @pl.when(i == 0)
    def _():
        carry[...] = jnp.zeros_like(carry)

    ubuf[0:SUBLANES] = jnp.where(ti == 0, 0.0, up_ref[...])
    ubuf[SUBLANES:SUBLANES + tm] = uc_ref[...]
    ubuf[SUBLANES + tm:2 * SUBLANES + tm] = jnp.where(ti == nt - 1, 0.0, un_ref[...])
    cw = cw_ref[...]
    ub = ubuf[...]
    n_buf = tm + 2 * SUBLANES
    xc = cb_ref[...]
    for j in range(CONV_W):
        back = CONV_PAD_LEFT - j
        tap = ub if back == 0 else pltpu.roll(ub, back % n_buf, 0)
        xc = xc + cw[j:j + 1] * tap[SUBLANES:SUBLANES + tm]

    xcb = xc.astype(BF16)
    lam = lam_ref[...]
    half_c = (0.5 * LRU_C) * (jnp.minimum(lam, 0.0) - jnp.log1p(jnp.exp(-jnp.abs(lam))))
    half_ba = 0.5 * ba_ref[...]
    half_bx = 0.5 * bx_ref[...]
    half_xc = 0.5 * xc
    for hb in range(LRU_BLOCKS):
        sl = slice(hb * LRU_BLOCK_W, (hb + 1) * LRU_BLOCK_W)
        gz = jnp.dot(xcb[:, sl], w_ref[hb], preferred_element_type=F32)
        tr = jnp.tanh(gz[:, :LRU_BLOCK_W] + half_ba[:, sl])
        ti = jnp.tanh(gz[:, LRU_BLOCK_W:] + half_bx[:, sl])
        log_a = tr * half_c[:, sl] + half_c[:, sl]
        a = jnp.exp(log_a)
        a_s[:, sl] = a
        y = jnp.tanh(log_a) * (-1.0 - a * a)
        root = jnp.where(y > 0.0, y * lax.rsqrt(y), 0.0)
        g_s[:, sl] = root * (ti * half_xc[:, sl] + half_xc[:, sl])

    n_chunk = tm // SUBLANES
    srow = lax.broadcasted_iota(jnp.int32, (SUBLANES, LRU_WIDTH), 0)

    def chunk(c, h_prev):
        ci = (n_chunk - 1 - c) if reverse else c
        off = pl.multiple_of(ci * SUBLANES, SUBLANES)
        a = a_s[pl.ds(off, SUBLANES), :]
        b = g_s[pl.ds(off, SUBLANES), :]
        edge = (SUBLANES - 1) if reverse else 0
        b = b + jnp.where(srow == edge, a * h_prev, 0.0)
        for d in (1, 2, 4):
            shift = (SUBLANES - d) if reverse else d
            valid = (srow < SUBLANES - d) if reverse else (srow >= d)
            b = a * jnp.where(valid, pltpu.roll(b, shift, 0), 0.0) + b
            if d != 4:
                a = a * pltpu.roll(a, shift, 0)
        h = b
        if reverse:
            rows = pl.ds(off, SUBLANES)
            out_ref[rows, :] = ga_ref[rows, :] * (ggr_ref[rows, :] * (hf_ref[rows, :] + h))
            return h[0:1]
        out_ref[pl.ds(off, SUBLANES), :] = h
        return h[SUBLANES - 1:SUBLANES]

    carry[...] = lax.fori_loop(0, n_chunk, chunk, carry[...])


def _lru_scan(u, gate, ggr, h_fwd, p, d, batch, seq_len, tm, reverse):
    t = u.shape[0]
    nt = seq_len // tm
    per8 = tm // SUBLANES
    n8 = t // SUBLANES

    def tile(b, i):
        return b * nt + ((nt - 1 - i) if reverse else i)

    cur = lambda b, i: (tile(b, i), 0)
    prev = lambda b, i: (jnp.maximum(tile(b, i) * per8 - 1, 0), 0)
    nxt = lambda b, i: (jnp.minimum((tile(b, i) + 1) * per8, n8 - 1), 0)
    const2 = lambda b, i: (0, 0)
    in_specs = [
        pl.BlockSpec((SUBLANES, LRU_WIDTH), prev),
        pl.BlockSpec((tm, LRU_WIDTH), cur),
        pl.BlockSpec((SUBLANES, LRU_WIDTH), nxt),
        pl.BlockSpec((CONV_W, LRU_WIDTH), const2),
        pl.BlockSpec((1, LRU_WIDTH), const2),
        pl.BlockSpec((LRU_BLOCKS, LRU_BLOCK_W, 2 * LRU_BLOCK_W), lambda b, i: (0, 0, 0)),
        pl.BlockSpec((1, LRU_WIDTH), const2),
        pl.BlockSpec((1, LRU_WIDTH), const2),
        pl.BlockSpec((1, LRU_WIDTH), const2),
    ]
    args = [u, u, u, p["conv_w"], p["conv_b"], p["lru_w"][d], p["lru_ba"][d], p["lru_bx"][d], p["lru_lam"][d]]
    if reverse:
        in_specs += [pl.BlockSpec((tm, LRU_WIDTH), cur)] * 3
        args += [h_fwd, ggr, gate]
    return pl.pallas_call(
        functools.partial(_lru_kernel, reverse=reverse, tm=tm, nt=nt),
        grid=(batch, nt), in_specs=in_specs,
        out_specs=pl.BlockSpec((tm, LRU_WIDTH), cur),
        out_shape=jax.ShapeDtypeStruct((t, LRU_WIDTH), F32),
        scratch_shapes=[pltpu.VMEM((tm + 2 * SUBLANES, LRU_WIDTH), F32), pltpu.VMEM((tm, LRU_WIDTH), F32),
                        pltpu.VMEM((tm, LRU_WIDTH), F32), pltpu.VMEM((1, LRU_WIDTH), F32)],
        compiler_params=_cparams(("arbitrary", "arbitrary")), name="lru_bwd" if reverse else "lru_fwd",
    )(*args)


def _attn_kernel(bound_ref, qt_ref, k_ref, vt_ref, gb_ref, o_ref, m_s, l_s, acc_s, *, tq, tk, n_chunks, n_kv_blocks):
    nq_cols = GROUP * tq
    kv_blk = pl.program_id(3)
    bound = bound_ref[0]
    bounded = 2.0 * bound <= SAFE_EXP2_RANGE

    @pl.when(kv_blk == 0)
    def _():
        acc_s[...] = jnp.zeros_like(acc_s)
        l_s[...] = jnp.zeros_like(l_s)
        m_s[...] = jnp.full_like(m_s, -jnp.inf)

    @pl.when(bounded)
    def _():
        qt = qt_ref[0, 0]
        for c in range(n_chunks):
            rows = slice(c * tk, (c + 1) * tk)
            pr = jnp.exp2(jnp.dot(k_ref[rows, :], qt, preferred_element_type=F32) - bound)
            l_s[...] += jnp.sum(pr.reshape(tk // SUBLANES, SUBLANES, nq_cols), axis=0)
            acc_s[...] += jnp.dot(vt_ref[0, :, rows], pr.astype(BF16), preferred_element_type=F32)

    @pl.when(jnp.logical_not(bounded))
    def _():
        qt = qt_ref[0, 0]

        def chunk(c, _):
            off = pl.multiple_of(c * tk, tk)
            s = jnp.dot(k_ref[pl.ds(off, tk), :], qt, preferred_element_type=F32)
            m_old = m_s[...]
            m_new = jnp.maximum(m_old, jnp.max(s, axis=0, keepdims=True))
            alpha = jnp.exp2(m_old - m_new)
            pr = jnp.exp2(s - m_new)
            l_s[...] = alpha * l_s[...] + jnp.sum(pr.reshape(tk // SUBLANES, SUBLANES, nq_cols), axis=0)
            acc_s[...] = alpha * acc_s[...] + jnp.dot(vt_ref[0, :, pl.ds(off, tk)], pr.astype(BF16),
                                                      preferred_element_type=F32)
            m_s[...] = m_new
            return 0

        lax.fori_loop(0, n_chunks, chunk, 0)

    @pl.when(kv_blk == n_kv_blocks - 1)
    def _():
        out_t = acc_s[...] / jnp.sum(l_s[...], axis=0, keepdims=True)
        for g in range(GROUP):
            cols = slice(g * HEAD_DIM, (g + 1) * HEAD_DIM)
            o_ref[:, cols] = gb_ref[:, cols] * out_t[:, g * tq:(g + 1) * tq].T


def _attention(bound, qt, k, vt, gate, batch, seq_len, tq, tk, kv_block):
    t = k.shape[0]
    nq = seq_len // tq
    nkv = seq_len // kv_block
    half = GROUP * HEAD_DIM
    gate_col0 = D_MODEL // half
    return pl.pallas_call(
        functools.partial(_attn_kernel, tq=tq, tk=tk, n_chunks=kv_block // tk, n_kv_blocks=nkv),
        grid=(batch, N_KV_HEADS, nq, nkv),
        in_specs=[
            pl.BlockSpec(memory_space=pltpu.SMEM),
            pl.BlockSpec((1, 1, HEAD_DIM, GROUP * tq), lambda b, j, i, c: (b * nq + i, j, 0, 0)),
            pl.BlockSpec((kv_block, HEAD_DIM), lambda b, j, i, c: (b * nkv + c, j)),
            pl.BlockSpec((1, HEAD_DIM, kv_block), lambda b, j, i, c: (j, 0, b * nkv + c)),
            pl.BlockSpec((tq, half), lambda b, j, i, c: (b * nq + i, gate_col0 + j)),
        ],
        out_specs=pl.BlockSpec((tq, half), lambda b, j, i, c: (b * nq + i, j)),
        out_shape=jax.ShapeDtypeStruct((t, D_MODEL), F32),
        scratch_shapes=[pltpu.VMEM((1, GROUP * tq), F32), pltpu.VMEM((SUBLANES, GROUP * tq), F32),
                        pltpu.VMEM((HEAD_DIM, GROUP * tq), F32)],
        compiler_params=_cparams(("arbitrary", "arbitrary", "arbitrary", "arbitrary")), name="attention",
    )(bound, qt, k, vt, gate)


def _outproj_router_kernel(x_ref, ma_ref, mb_ref, wout_ref, gffn_ref, wr_ref, br_ref, cnt0_ref,
                           x1_ref, h2_ref, route_ref, route_t_ref, cnt_ref, cnt_s):
    tm = x_ref.shape[0]
    i = pl.program_id(0)

    @pl.when(i == 0)
    def _():
        cnt_s[...] = cnt0_ref[...]

    merged = (ma_ref[...] + mb_ref[...]).astype(BF16)
    x1 = x_ref[...] + jnp.dot(merged, wout_ref[...], preferred_element_type=F32)
    x1_ref[...] = x1
    h2 = x1 * lax.rsqrt(jnp.mean(x1 * x1, axis=-1, keepdims=True) + EPS) * gffn_ref[...]
    h2_ref[...] = _pack_bf16_pairs(h2)

    hi = h2.astype(BF16)
    lo = (h2 - hi.astype(F32)).astype(BF16)
    zt = lax.dot_general(wr_ref[...], jnp.concatenate([hi, hi, lo], axis=1), (((1,), (1,)), ((), ())),
                         preferred_element_type=F32) + br_ref[...]
    sub = lax.broadcasted_iota(jnp.int32, (ROUTE_ROWS, tm), 0)
    neg = -jnp.inf

    def first_argmax(v):
        m = jnp.max(v, axis=0, keepdims=True)
        return m, jnp.min(jnp.where(v == m, sub, ROUTE_ROWS), axis=0, keepdims=True)

    zg = jnp.where(sub < N_GROUPS, zt, neg)
    mg, grp = first_argmax(zg)
    p_sel = 1.0 / jnp.sum(jnp.exp(zg - mg), axis=0, keepdims=True)
    e_lo = EXPERT_ROW0 + EXPERTS_PER_GROUP * grp
    ze = jnp.where((sub >= e_lo) & (sub < e_lo + EXPERTS_PER_GROUP), zt, neg)
    m1, i1 = first_argmax(ze)
    m2, i2 = first_argmax(jnp.where(sub == i1, neg, ze))
    e2 = jnp.exp(m2 - m1)
    w0 = p_sel / (1.0 + e2)
    w1 = p_sel * e2 / (1.0 + e2)

    hot0 = sub == i1
    hot1 = sub == i2
    hot = jnp.where(hot0 | hot1, 1.0, 0.0)
    rr = lax.broadcasted_iota(jnp.int32, (tm, tm), 0)
    cc = lax.broadcasted_iota(jnp.int32, (tm, tm), 1)
    earlier = jnp.where(rr < cc, 1.0, 0.0).astype(BF16)
    cnt = cnt_s[...]
    before = (jnp.dot(hot.astype(BF16), earlier, preferred_element_type=F32)
              + jnp.concatenate([cnt] * (tm // LANES), axis=1))
    rank0 = jnp.sum(jnp.where(hot0, before, 0.0), axis=0, keepdims=True)
    rank1 = jnp.sum(jnp.where(hot1, before, 0.0), axis=0, keepdims=True)
    cnt_s[...] = cnt + jnp.sum(hot, axis=1, keepdims=True)
    cnt_ref[...] = cnt_s[...]

    eid0 = (i1 - EXPERT_ROW0).astype(F32)
    eid1 = (i2 - EXPERT_ROW0).astype(F32)
    zero = jnp.zeros_like(w0)
    rec_t = jnp.concatenate([eid0, eid1, w0, w1, rank0, rank1, zero, zero], axis=0)
    route_t_ref[...] = rec_t
    route_ref[...] = jnp.concatenate([rec_t, jnp.zeros((ROUTE_LANES - SUBLANES, tm), F32)], axis=0).T


def _outproj_router(x2, m_a, m_b, cnt0, p, tm):
    t = x2.shape[0]
    row = lambda i: (i, 0)
    return pl.pallas_call(
        _outproj_router_kernel, grid=(t // tm,),
        in_specs=[pl.BlockSpec((tm, D_MODEL), row), pl.BlockSpec((tm, D_MODEL), row), pl.BlockSpec((tm, D_MODEL), row),
                  _const_spec((D_MODEL, D_MODEL)), _const_spec((1, D_MODEL)), _const_spec((ROUTE_ROWS, 3 * D_MODEL)),
                  _const_spec((ROUTE_ROWS, tm)), _const_spec((ROUTE_ROWS, LANES))],
        out_specs=(pl.BlockSpec((tm, D_MODEL), row), pl.BlockSpec((tm, PACKED_COLS), row),
                   pl.BlockSpec((tm, ROUTE_LANES), row), pl.BlockSpec((SUBLANES, tm), lambda i: (0, i)),
                   _const_spec((ROUTE_ROWS, LANES))),
        out_shape=(jax.ShapeDtypeStruct((t, D_MODEL), F32), jax.ShapeDtypeStruct((t, PACKED_COLS), jnp.uint32),
                   jax.ShapeDtypeStruct((t, ROUTE_LANES), F32), jax.ShapeDtypeStruct((SUBLANES, t), F32),
                   jax.ShapeDtypeStruct((ROUTE_ROWS, LANES), F32)),
        scratch_shapes=[pltpu.VMEM((ROUTE_ROWS, LANES), F32)],
        compiler_params=_cparams(("arbitrary",)), name="outproj_router",
    )(x2, m_a, m_b, p["w_out"], p["g_ffn"], p["w_route"], p["b_route"], cnt0)


def _dispatch_kernel(dest_ref, h_ref, xs_in_ref, xs_ref, sem, *, tm, n_tok):
    del xs_in_ref
    base = pl.program_id(0) * tm

    def issue(r, _):
        for k in range(TOP_K):
            d = dest_ref[k * n_tok + base + r]
            pltpu.make_async_copy(h_ref.at[pl.ds(r, 1), :], xs_ref.at[pl.ds(d, 1), :], sem).start()
        return 0

    lax.fori_loop(0, tm, issue, 0, unroll=ROW_DMA_UNROLL)
    for k in range(TOP_K):
        pltpu.make_async_copy(h_ref, xs_ref.at[pl.ds(0, tm), :], sem).wait()


def _dispatch(dest_flat, h2, xs, tm):
    t = h2.shape[0]
    return pl.pallas_call(
        functools.partial(_dispatch_kernel, tm=tm, n_tok=t),
        grid_spec=pltpu.PrefetchScalarGridSpec(
            num_scalar_prefetch=1, grid=(t // tm,),
            in_specs=[pl.BlockSpec((tm, PACKED_COLS), lambda i, d: (i, 0)), pl.BlockSpec(memory_space=pl.ANY)],
            out_specs=pl.BlockSpec(memory_space=pl.ANY),
            scratch_shapes=[pltpu.SemaphoreType.DMA]),
        out_shape=jax.ShapeDtypeStruct(xs.shape, xs.dtype),
        input_output_aliases={2: 0},
        compiler_params=_cparams(("arbitrary",)), name="moe_dispatch",
    )(dest_flat, h2, xs)


def _experts_kernel(blk_e_ref, nxt_e_ref, slot_ref, n_used_ref, x_ref, wg_hbm, wu_hbm, wd_hbm, y_ref,
                    wg_f, wu_f, wd_f, wg_s, wu_s, wd_s, sems):
    i = pl.program_id(0)
    e = blk_e_ref[i]
    slot = slot_ref[i]
    first = jnp.logical_or(i == 0, e != blk_e_ref[jnp.maximum(i - 1, 0)])

    def weight_copies(expert, s):
        return (pltpu.make_async_copy(wg_hbm.at[expert], wg_f.at[s], sems.at[s, 0]),
                pltpu.make_async_copy(wu_hbm.at[expert], wu_f.at[s], sems.at[s, 1]),
                pltpu.make_async_copy(wd_hbm.at[expert], wd_f.at[s], sems.at[s, 2]))

    @pl.when(i == 0)
    def _():
        for cp in weight_copies(e, slot):
            cp.start()

    @pl.when(first)
    def _():
        for cp in weight_copies(e, slot):
            cp.wait()
        wg_s[...] = wg_f[slot].astype(BF16)
        wu_s[...] = wu_f[slot].astype(BF16)
        wd_s[...] = wd_f[slot].astype(BF16)
        nxt = nxt_e_ref[i]

        @pl.when(nxt != e)
        def _():
            for cp in weight_copies(nxt, 1 - slot):
                cp.start()

    @pl.when(i < n_used_ref[0])
    def _():
        xb = _unpack_bf16_pairs(x_ref[...]).astype(BF16)
        gate = jnp.dot(xb, wg_s[...], preferred_element_type=F32)
        up = jnp.dot(xb, wu_s[...], preferred_element_type=F32)
        act = (jax.nn.silu(gate) * up).astype(BF16)
        y_ref[...] = _pack_bf16_pairs(jnp.dot(act, wd_s[...], preferred_element_type=F32))

    @pl.when(i >= n_used_ref[0])
    def _():
        y_ref[...] = jnp.zeros_like(y_ref)


def _experts(blk_e, nxt_e, slot, n_used, xs, w_gate, w_up, w_down):
    n_rows = xs.shape[0]
    rb = MOE_ROWS
    row_blk = lambda i, *_: (i, 0)
    return pl.pallas_call(
        _experts_kernel,
        grid_spec=pltpu.PrefetchScalarGridSpec(
            num_scalar_prefetch=4, grid=(n_rows // rb,),
            in_specs=[pl.BlockSpec((rb, PACKED_COLS), row_blk),
                      pl.BlockSpec(memory_space=pl.ANY), pl.BlockSpec(memory_space=pl.ANY),
                      pl.BlockSpec(memory_space=pl.ANY)],
            out_specs=pl.BlockSpec((rb, PACKED_COLS), row_blk),
            scratch_shapes=[pltpu.VMEM((2, D_MODEL, D_EXPERT), F32), pltpu.VMEM((2, D_MODEL, D_EXPERT), F32),
                            pltpu.VMEM((2, D_EXPERT, D_MODEL), F32),
                            pltpu.VMEM((D_MODEL, D_EXPERT), BF16), pltpu.VMEM((D_MODEL, D_EXPERT), BF16),
                            pltpu.VMEM((D_EXPERT, D_MODEL), BF16), pltpu.SemaphoreType.DMA((2, 3))]),
        out_shape=jax.ShapeDtypeStruct((n_rows, PACKED_COLS), jnp.uint32),
        compiler_params=_cparams(("arbitrary",)), name="moe_experts",
    )(blk_e, nxt_e, slot, n_used, xs, w_gate, w_up, w_down)


def _combine_kernel(dest_ref, x1_ref, route_ref, ys_ref, o_ref, gbuf, sem, *, tm, n_tok):
    base = pl.program_id(0) * tm

    def issue(r, _):
        for k in range(TOP_K):
            d = dest_ref[k * n_tok + base + r]
            pltpu.make_async_copy(ys_ref.at[pl.ds(d, 1), :], gbuf.at[k, pl.ds(r, 1), :], sem).start()
        return 0

    lax.fori_loop(0, tm, issue, 0, unroll=ROW_DMA_UNROLL)
    for k in range(TOP_K):
        pltpu.make_async_copy(ys_ref.at[pl.ds(0, tm), :], gbuf.at[k], sem).wait()
    w0 = route_ref[:, 2:3]
    w1 = route_ref[:, 3:4]
    o_ref[...] = x1_ref[...] + (_unpack_bf16_pairs(gbuf[0]) * w0 + _unpack_bf16_pairs(gbuf[1]) * w1)


def _combine(dest_flat, x1, route, ys, tm):
    t = x1.shape[0]
    return pl.pallas_call(
        functools.partial(_combine_kernel, tm=tm, n_tok=t),
        grid_spec=pltpu.PrefetchScalarGridSpec(
            num_scalar_prefetch=1, grid=(t // tm,),
            in_specs=[pl.BlockSpec((tm, D_MODEL), lambda i, d: (i, 0)),
                      pl.BlockSpec((tm, ROUTE_LANES), lambda i, d: (i, 0)),
                      pl.BlockSpec(memory_space=pl.ANY)],
            out_specs=pl.BlockSpec((tm, D_MODEL), lambda i, d: (i, 0)),
            scratch_shapes=[pltpu.VMEM((TOP_K, tm, PACKED_COLS), jnp.uint32), pltpu.SemaphoreType.DMA]),
        out_shape=jax.ShapeDtypeStruct((t, D_MODEL), F32),
        compiler_params=_cparams(("arbitrary",)), name="moe_combine",
    )(dest_flat, x1, route, ys)


def _rope_tables(seq_len):
    inv = ROPE_THETA ** (-jnp.arange(0, AXIS_DIM, 2, dtype=F32) / AXIS_DIM)
    n_grid_rows = seq_len // GRID_W
    ang_r = jnp.arange(n_grid_rows, dtype=F32)[:, None] * inv
    ang_c = jnp.arange(GRID_W, dtype=F32)[:, None] * inv
    small = (jnp.cos(ang_r), jnp.sin(ang_r), jnp.cos(ang_c), jnp.sin(ang_c))

    def tables(pos_axis):
        reps = (n_grid_rows, 1) if pos_axis == 0 else (1, n_grid_rows)
        cos_r, sin_r, cos_c, sin_c = (a if pos_axis == 0 else a.T for a in small)
        cos_r, sin_r = jnp.repeat(cos_r, GRID_W, axis=pos_axis), jnp.repeat(sin_r, GRID_W, axis=pos_axis)
        cos_c, sin_c = jnp.tile(cos_c, reps), jnp.tile(sin_c, reps)
        return (jnp.concatenate([cos_r, cos_r, cos_c, cos_c], axis=1 - pos_axis),
                jnp.concatenate([-sin_r, sin_r, -sin_c, sin_c], axis=1 - pos_axis))

    return tables(0) + tables(1)


def _prepare_params(l, max_seq, tm, g_mix, w_in, b_gate, conv_w, conv_b, lru_wa, lru_ba, lru_wx, lru_bx, lru_lam,
                    q_gain, k_gain, w_out, g_ffn, w_rg, b_rg, w_re, b_re):
    c_u, c_gr, c_q, c_k, c_v = (LRU_WIDTH, 2 * LRU_WIDTH, 2 * LRU_WIDTH + D_MODEL,
                                2 * LRU_WIDTH + D_MODEL + ATTN_KV, 2 * LRU_WIDTH + D_MODEL + 2 * ATTN_KV)
    w = w_in[l]
    w_nat = jnp.concatenate([w[:, :c_gr], w[:, c_q:c_k], w[:, c_v:]], axis=1).astype(BF16)
    w_tr = jnp.concatenate([w[:, c_gr:c_q], w[:, c_k:c_v]], axis=1).T.astype(BF16)
    cos, sin, cos_t, sin_t = _rope_tables(max_seq)
    pad = ROUTE_ROWS - N_GROUPS - N_EXPERTS
    w_route = jnp.concatenate([w_rg[l], w_re[l], jnp.zeros((D_MODEL, pad), F32)], axis=1).T
    w_route_hi = w_route.astype(BF16)
    w_route_lo = (w_route - w_route_hi.astype(F32)).astype(BF16)
    w_route = jnp.concatenate([w_route_hi, w_route_lo, w_route_hi], axis=1)
    b_route = jnp.broadcast_to(jnp.concatenate([b_rg[l], b_re[l], jnp.zeros((pad,), F32)])[:, None], (ROUTE_ROWS, tm))
    lru_w = (0.5 * jnp.concatenate([lru_wa[l], lru_wx[l]], axis=-1)).astype(BF16)
    score_bound = (SCORE_BOUND_SLACK * LOG2E * math.sqrt(HEAD_DIM)
                   * jnp.max(jnp.abs(q_gain[l])) * jnp.max(jnp.abs(k_gain[l]))).reshape(1)
    return dict(
        score_bound=score_bound,
        g_mix=g_mix[l][None, :], w_nat=w_nat, w_tr=w_tr, b_gate=b_gate[l][None, :],
        k_gain=k_gain[l][None, :], q_gain_t=jnp.broadcast_to(q_gain[l][:, None], (HEAD_DIM, WIDE_TILE)),
        cos=cos, sin=sin, cos_t=cos_t, sin_t=sin_t,
        conv_w=conv_w[l], conv_b=conv_b[l][None, :], lru_w=lru_w,
        lru_ba=lru_ba[l].reshape(2, 1, LRU_WIDTH), lru_bx=lru_bx[l].reshape(2, 1, LRU_WIDTH),
        lru_lam=lru_lam[l].reshape(2, 1, LRU_WIDTH),
        w_out=w_out[l].astype(BF16), g_ffn=g_ffn[l][None, :], w_route=w_route, b_route=b_route,
    )


def _mixer(x, p, cnt0, tm, tk):
    batch, seq_len, _ = x.shape
    x2 = x.reshape(batch * seq_len, D_MODEL)
    u, ggr, k, gate, qt, vt = _inproj(x2, seq_len, p, WIDE_TILE, tm)
    h_fwd = _lru_scan(u, gate, ggr, None, p, 0, batch, seq_len, WIDE_TILE, reverse=False)
    m_a = _lru_scan(u, gate, ggr, h_fwd, p, 1, batch, seq_len, WIDE_TILE, reverse=True)
    m_b = _attention(p["score_bound"], qt, k, vt, gate, batch, seq_len, tm, min(tk, seq_len),
                     min(KV_BLOCK, seq_len))
    return _outproj_router(x2, m_a, m_b, cnt0, p, tm)


def _layer(xs_in, l, weights):
    (g_mix, w_in, b_gate, conv_w, conv_b, lru_wa, lru_ba, lru_wx, lru_bx, lru_lam,
     q_gain, k_gain, w_out, g_ffn, w_rg, b_rg, w_re, b_re, w_gate, w_up, w_down) = weights
    tm = ROW_TILE
    max_seq = max(x.shape[1] for x in xs_in)
    p = _prepare_params(l, max_seq, tm, g_mix, w_in, b_gate, conv_w, conv_b, lru_wa, lru_ba, lru_wx, lru_bx,
                        lru_lam, q_gain, k_gain, w_out, g_ffn, w_rg, b_rg, w_re, b_re)

    cnt = jnp.zeros((ROUTE_ROWS, LANES), F32)
    x1s, h2s, routes, routes_t = [], [], [], []
    for x in xs_in:
        x1, h2, route, route_t, cnt = _mixer(x, p, cnt, tm, KV_CHUNK)
        x1s.append(x1)
        h2s.append(h2)
        routes.append(route)
        routes_t.append(route_t)

    n_tok = sum(x1.shape[0] for x1 in x1s)
    n_rows = n_tok * TOP_K + N_EXPERTS * MOE_ROWS
    n_blk = n_rows // MOE_ROWS
    counts = cnt[EXPERT_ROW0:EXPERT_ROW0 + N_EXPERTS, 0].astype(jnp.int32)
    padded = (counts + MOE_ROWS - 1) // MOE_ROWS * MOE_ROWS
    pend = jnp.cumsum(padded)
    pstart = pend - padded
    blk_idx = jnp.arange(n_blk, dtype=jnp.int32)
    n_used = (pend[-1:] // MOE_ROWS).astype(jnp.int32)
    blk_e = jnp.minimum(jnp.sum((pend[None, :] <= (blk_idx * MOE_ROWS)[:, None]).astype(jnp.int32), axis=1),
                        N_EXPERTS - 1)
    blk_e = jnp.where(blk_idx < n_used, blk_e, blk_e[jnp.maximum(n_used[0] - 1, 0)])
    starts = jnp.concatenate([jnp.ones((1,), jnp.int32), (blk_e[1:] != blk_e[:-1]).astype(jnp.int32)])
    slot = (jnp.cumsum(starts) - 1) % 2
    later_start = lax.cummin(jnp.where(starts == 1, blk_idx, n_blk), reverse=True)
    next_start = jnp.concatenate([later_start[1:], jnp.full((1,), n_blk, jnp.int32)])
    nxt_e = jnp.where(next_start < n_blk, blk_e[jnp.minimum(next_start, n_blk - 1)], blk_e)

    dests = []
    xs = jnp.zeros((n_rows, PACKED_COLS), jnp.uint32)
    for h2, route_t in zip(h2s, routes_t):
        eid = route_t[0:TOP_K].astype(jnp.int32)
        rank = route_t[4:4 + TOP_K].astype(jnp.int32)
        experts = jnp.arange(N_EXPERTS, dtype=jnp.int32)[:, None, None]
        dest = (rank + jnp.sum(jnp.where(eid[None] == experts, pstart[:, None, None], 0), axis=0)).reshape(-1)
        dests.append(dest)
        xs = _dispatch(dest, h2, xs, WIDE_TILE)
    ys = _experts(blk_e, nxt_e, slot.astype(jnp.int32), n_used, xs, w_gate[l], w_up[l], w_down[l])
    outs = []
    for x, x1, route, dest in zip(xs_in, x1s, routes, dests):
        outs.append(_combine(dest, x1, route, ys, WIDE_TILE).reshape(x.shape))
    return outs


def kernel(x_prompt, x_sample, g_mix, w_in, b_gate, conv_w, conv_b, lru_wa, lru_ba, lru_wx, lru_bx, lru_lam,
           q_gain, k_gain, w_out, g_ffn, w_rg, b_rg, w_re, b_re, w_gate, w_up, w_down):
    weights = (g_mix, w_in, b_gate, conv_w, conv_b, lru_wa, lru_ba, lru_wx, lru_bx, lru_lam,
               q_gain, k_gain, w_out, g_ffn, w_rg, b_rg, w_re, b_re, w_gate, w_up, w_down)
    xs = [x_prompt, x_sample]
    for l in range(g_mix.shape[0]):
        xs = _layer(xs, l, weights)
    return tuple(xs)
```

```python
import functools
import math

import jax
import jax.numpy as jnp
from jax import lax
from jax.experimental import pallas as pl
from jax.experimental.pallas import tpu as pltpu

F32 = jnp.float32
BF16 = jnp.bfloat16

D_MODEL = 1024
N_HEADS = 8
N_KV_HEADS = 2
GROUP = N_HEADS // N_KV_HEADS
HEAD_DIM = D_MODEL // N_HEADS
AXIS_DIM = HEAD_DIM // 2
N_FREQ = AXIS_DIM // 2
GRID_W = 64
ROPE_THETA = 10000.0
LRU_WIDTH = D_MODEL
LRU_BLOCKS = 8
LRU_BLOCK_W = LRU_WIDTH // LRU_BLOCKS
CONV_W = 4
CONV_PAD_LEFT = 2
LRU_C = 8.0
N_GROUPS = 4
EXPERTS_PER_GROUP = 8
N_EXPERTS = N_GROUPS * EXPERTS_PER_GROUP
TOP_K = 2
D_EXPERT = D_MODEL // 2
ATTN_KV = N_KV_HEADS * HEAD_DIM
EPS = 1e-6

LANES = 128
SUBLANES = 8
VMEM_LIMIT_BYTES = 56 * 1024 * 1024

ROW_TILE = 256
WIDE_TILE = 512
KV_CHUNK = 2048
KV_BLOCK = 16384
MOE_ROWS = 512
ROW_DMA_UNROLL = 8
ROUTE_LANES = LANES
ROUTE_ROWS = 64
PACKED_COLS = D_MODEL // 2
EXPERT_ROW0 = N_GROUPS

NAT_COLS = 2 * LRU_WIDTH + ATTN_KV + 2 * D_MODEL
TR_ROWS = D_MODEL + ATTN_KV
LOG2E = math.log2(math.e)
SAFE_EXP2_RANGE = 100.0
SCORE_BOUND_SLACK = 1.01


def _cparams(semantics):
    return pltpu.CompilerParams(dimension_semantics=semantics, vmem_limit_bytes=VMEM_LIMIT_BYTES)


def _const_spec(shape, single_buffer=False):
    nd = len(shape)
    mode = pl.Buffered(1) if single_buffer else None
    return pl.BlockSpec(shape, lambda *_: (0,) * nd, pipeline_mode=mode)


def _sigmoid(x):
    return 0.5 * jnp.tanh(0.5 * x) + 0.5


def _pack_bf16_pairs(x):
    n = x.shape[1] // 2
    lo = lax.bitcast_convert_type(x[:, :n].astype(BF16).astype(F32), jnp.uint32)
    hi = lax.bitcast_convert_type(x[:, n:].astype(BF16).astype(F32), jnp.uint32)
    return (hi & jnp.uint32(0xFFFF0000)) | (lo >> 16)


def _unpack_bf16_pairs(w):
    lo = lax.bitcast_convert_type(w << 16, F32)
    hi = lax.bitcast_convert_type(w & jnp.uint32(0xFFFF0000), F32)
    return jnp.concatenate([lo, hi], axis=1)


def _swap_halves_rows(x):
    return jnp.concatenate([x[N_FREQ:AXIS_DIM], x[0:N_FREQ], x[AXIS_DIM + N_FREQ:], x[AXIS_DIM:AXIS_DIM + N_FREQ]], axis=0)


def _inproj_kernel(x_ref, gmix_ref, wnat_ref, wtr_ref, bgate_ref, kgain_ref, qgain_ref,
                   cos_ref, sin_ref, cost_ref, sint_ref,
                   u_ref, ggr_ref, k_ref, gate_ref, qt_ref, vt_ref):
    tm = x_ref.shape[0]
    x = x_ref[...]
    h = (x * lax.rsqrt(jnp.mean(x * x, axis=-1, keepdims=True) + EPS) * gmix_ref[...]).astype(BF16)

    def nat(lo, hi):
        return jnp.dot(h, wnat_ref[:, lo:hi], preferred_element_type=F32)

    u_ref[...] = nat(0, LRU_WIDTH)
    ggr_ref[...] = jax.nn.gelu(nat(LRU_WIDTH, 2 * LRU_WIDTH))
    k0 = 2 * LRU_WIDTH
    kraw = nat(k0, k0 + ATTN_KV)
    g0 = k0 + ATTN_KV
    gate_ref[...] = _sigmoid(nat(g0, g0 + 2 * D_MODEL) + bgate_ref[...])

    cos = cos_ref[...]
    sin = sin_ref[...]
    lane = lax.broadcasted_iota(jnp.int32, (tm, HEAD_DIM), 1)
    first_half = (lane % AXIS_DIM) < N_FREQ
    for j in range(N_KV_HEADS):
        kj = kraw[:, j * HEAD_DIM:(j + 1) * HEAD_DIM]
        kn = kj * lax.rsqrt(jnp.mean(kj * kj, axis=-1, keepdims=True) + EPS) * kgain_ref[...]
        partner = jnp.where(first_half, pltpu.roll(kn, HEAD_DIM - N_FREQ, 1), pltpu.roll(kn, N_FREQ, 1))
        k_ref[:, j * HEAD_DIM:(j + 1) * HEAD_DIM] = (kn * cos + partner * sin).astype(BF16)

    zt = lax.dot_general(wtr_ref[...], h, (((1,), (1,)), ((), ())), preferred_element_type=F32)
    cost = cost_ref[...]
    sint = sint_ref[...]
    qgain = qgain_ref[...]
    qscale = (HEAD_DIM ** -0.5) * LOG2E
    for hd in range(N_HEADS):
        xq = zt[hd * HEAD_DIM:(hd + 1) * HEAD_DIM, :]
        xn = xq * lax.rsqrt(jnp.mean(xq * xq, axis=0, keepdims=True) + EPS) * qgain
        rot = (xn * cost + _swap_halves_rows(xn) * sint) * qscale
        g = hd % GROUP
        tq = qt_ref.shape[3] // GROUP
        for qb in range(tm // tq):
            qt_ref[qb, hd // GROUP, :, g * tq:(g + 1) * tq] = rot[:, qb * tq:(qb + 1) * tq].astype(BF16)
    for j in range(N_KV_HEADS):
        r0 = D_MODEL + j * HEAD_DIM
        vt_ref[j] = zt[r0:r0 + HEAD_DIM, :].astype(BF16)


def _inproj(x2, seq_len, p, tm, tq):
    t = x2.shape[0]
    n_pos = seq_len // tm
    grid = (t // tm,)
    row = lambda i: (i, 0)
    in_specs = [
        pl.BlockSpec((tm, D_MODEL), row),
        _const_spec((1, D_MODEL)),
        _const_spec((D_MODEL, NAT_COLS), single_buffer=True),
        _const_spec((TR_ROWS, D_MODEL), single_buffer=True),
        _const_spec((1, 2 * D_MODEL)),
        _const_spec((1, HEAD_DIM)),
        _const_spec((HEAD_DIM, tm)),
        pl.BlockSpec((tm, HEAD_DIM), lambda i: (i % n_pos, 0)),
        pl.BlockSpec((tm, HEAD_DIM), lambda i: (i % n_pos, 0)),
        pl.BlockSpec((HEAD_DIM, tm), lambda i: (0, i % n_pos)),
        pl.BlockSpec((HEAD_DIM, tm), lambda i: (0, i % n_pos)),
    ]
    out_shape = (
        jax.ShapeDtypeStruct((t, LRU_WIDTH), F32),
        jax.ShapeDtypeStruct((t, LRU_WIDTH), F32),
        jax.ShapeDtypeStruct((t, ATTN_KV), BF16),
        jax.ShapeDtypeStruct((t, 2 * D_MODEL), F32),
        jax.ShapeDtypeStruct((t // tq, N_KV_HEADS, HEAD_DIM, GROUP * tq), BF16),
        jax.ShapeDtypeStruct((N_KV_HEADS, HEAD_DIM, t), BF16),
    )
    out_specs = (
        pl.BlockSpec((tm, LRU_WIDTH), row),
        pl.BlockSpec((tm, LRU_WIDTH), row),
        pl.BlockSpec((tm, ATTN_KV), row),
        pl.BlockSpec((tm, 2 * D_MODEL), row),
        pl.BlockSpec((tm // tq, N_KV_HEADS, HEAD_DIM, GROUP * tq), lambda i: (i, 0, 0, 0)),
        pl.BlockSpec((N_KV_HEADS, HEAD_DIM, tm), lambda i: (0, 0, i)),
    )
    return pl.pallas_call(
        _inproj_kernel, grid=grid, in_specs=in_specs, out_specs=out_specs, out_shape=out_shape,
        compiler_params=_cparams(("arbitrary",)), name="inproj",
    )(x2, p["g_mix"], p["w_nat"], p["w_tr"], p["b_gate"], p["k_gain"], p["q_gain_t"],
      p["cos"], p["sin"], p["cos_t"], p["sin_t"])


def _lru_kernel(*refs, reverse, tm, nt):
    if reverse:
        (up_ref, uc_ref, un_ref, cw_ref, cb_ref, w_ref, ba_ref, bx_ref, lam_ref, hf_ref, ggr_ref, ga_ref,
         out_ref, ubuf, a_s, g_s, carry) = refs
    else:
        (up_ref, uc_ref, un_ref, cw_ref, cb_ref, w_ref, ba_ref, bx_ref, lam_ref,
         out_ref, ubuf, a_s, g_s, carry) = refs
    i = pl.program_id(1)
    ti = (nt - 1 - i) if reverse else i

    @pl.when(i == 0)
    def _():
        carry[...] = jnp.zeros_like(carry)

    ubuf[0:SUBLANES] = jnp.where(ti == 0, 0.0, up_ref[...])
    ubuf[SUBLANES:SUBLANES + tm] = uc_ref[...]
    ubuf[SUBLANES + tm:2 * SUBLANES + tm] = jnp.where(ti == nt - 1, 0.0, un_ref[...])
    cw = cw_ref[...]
    ub = ubuf[...]
    n_buf = tm + 2 * SUBLANES
    xc = cb_ref[...]
    for j in range(CONV_W):
        back = CONV_PAD_LEFT - j
        tap = ub if back == 0 else pltpu.roll(ub, back % n_buf, 0)
        xc = xc + cw[j:j + 1] * tap[SUBLANES:SUBLANES + tm]

    xcb = xc.astype(BF16)
    lam = lam_ref[...]
    half_c = (0.5 * LRU_C) * (jnp.minimum(lam, 0.0) - jnp.log1p(jnp.exp(-jnp.abs(lam))))
    half_ba = 0.5 * ba_ref[...]
    half_bx = 0.5 * bx_ref[...]
    half_xc = 0.5 * xc
    for hb in range(LRU_BLOCKS):
        sl = slice(hb * LRU_BLOCK_W, (hb + 1) * LRU_BLOCK_W)
        gz = jnp.dot(xcb[:, sl], w_ref[hb], preferred_element_type=F32)
        tr = jnp.tanh(gz[:, :LRU_BLOCK_W] + half_ba[:, sl])
        ti = jnp.tanh(gz[:, LRU_BLOCK_W:] + half_bx[:, sl])
        log_a = tr * half_c[:, sl] + half_c[:, sl]
        a = jnp.exp(log_a)
        a_s[:, sl] = a
        y = jnp.tanh(log_a) * (-1.0 - a * a)
        root = jnp.where(y > 0.0, y * lax.rsqrt(y), 0.0)
        g_s[:, sl] = root * (ti * half_xc[:, sl] + half_xc[:, sl])

    n_chunk = tm // SUBLANES
    srow = lax.broadcasted_iota(jnp.int32, (SUBLANES, LRU_WIDTH), 0)

    def chunk(c, h_prev):
        ci = (n_chunk - 1 - c) if reverse else c
        off = pl.multiple_of(ci * SUBLANES, SUBLANES)
        a = a_s[pl.ds(off, SUBLANES), :]
        b = g_s[pl.ds(off, SUBLANES), :]
        edge = (SUBLANES - 1) if reverse else 0
        b = b + jnp.where(srow == edge, a * h_prev, 0.0)
        for d in (1, 2, 4):
            shift = (SUBLANES - d) if reverse else d
            valid = (srow < SUBLANES - d) if reverse else (srow >= d)
            b = a * jnp.where(valid, pltpu.roll(b, shift, 0), 0.0) + b
            if d != 4:
                a = a * pltpu.roll(a, shift, 0)
        h = b
        if reverse:
            rows = pl.ds(off, SUBLANES)
            out_ref[rows, :] = ga_ref[rows, :] * (ggr_ref[rows, :] * (hf_ref[rows, :] + h))
            return h[0:1]
        out_ref[pl.ds(off, SUBLANES), :] = h
        return h[SUBLANES - 1:SUBLANES]

    carry[...] = lax.fori_loop(0, n_chunk, chunk, carry[...])


def _lru_scan(u, gate, ggr, h_fwd, p, d, batch, seq_len, tm, reverse):
    t = u.shape[0]
    nt = seq_len // tm
    per8 = tm // SUBLANES
    n8 = t // SUBLANES

    def tile(b, i):
        return b * nt + ((nt - 1 - i) if reverse else i)

    cur = lambda b, i: (tile(b, i), 0)
    prev = lambda b, i: (jnp.maximum(tile(b, i) * per8 - 1, 0), 0)
    nxt = lambda b, i: (jnp.minimum((tile(b, i) + 1) * per8, n8 - 1), 0)
    const2 = lambda b, i: (0, 0)
    in_specs = [
        pl.BlockSpec((SUBLANES, LRU_WIDTH), prev),
        pl.BlockSpec((tm, LRU_WIDTH), cur),
        pl.BlockSpec((SUBLANES, LRU_WIDTH), nxt),
        pl.BlockSpec((CONV_W, LRU_WIDTH), const2),
        pl.BlockSpec((1, LRU_WIDTH), const2),
        pl.BlockSpec((LRU_BLOCKS, LRU_BLOCK_W, 2 * LRU_BLOCK_W), lambda b, i: (0, 0, 0)),
        pl.BlockSpec((1, LRU_WIDTH), const2),
        pl.BlockSpec((1, LRU_WIDTH), const2),
        pl.BlockSpec((1, LRU_WIDTH), const2),
    ]
    args = [u, u, u, p["conv_w"], p["conv_b"], p["lru_w"][d], p["lru_ba"][d], p["lru_bx"][d], p["lru_lam"][d]]
    if reverse:
        in_specs += [pl.BlockSpec((tm, LRU_WIDTH), cur)] * 3
        args += [h_fwd, ggr, gate]
    return pl.pallas_call(
        functools.partial(_lru_kernel, reverse=reverse, tm=tm, nt=nt),
        grid=(batch, nt), in_specs=in_specs,
        out_specs=pl.BlockSpec((tm, LRU_WIDTH), cur),
        out_shape=jax.ShapeDtypeStruct((t, LRU_WIDTH), F32),
        scratch_shapes=[pltpu.VMEM((tm + 2 * SUBLANES, LRU_WIDTH), F32), pltpu.VMEM((tm, LRU_WIDTH), F32),
                        pltpu.VMEM((tm, LRU_WIDTH), F32), pltpu.VMEM((1, LRU_WIDTH), F32)],
        compiler_params=_cparams(("arbitrary", "arbitrary")), name="lru_bwd" if reverse else "lru_fwd",
    )(*args)


def _attn_kernel(bound_ref, qt_ref, k_ref, vt_ref, gb_ref, o_ref, m_s, l_s, acc_s, *, tq, tk, n_chunks, n_kv_blocks):
    nq_cols = GROUP * tq
    kv_blk = pl.program_id(3)
    bound = bound_ref[0]
    bounded = 2.0 * bound <= SAFE_EXP2_RANGE

    @pl.when(kv_blk == 0)
    def _():
        acc_s[...] = jnp.zeros_like(acc_s)
        l_s[...] = jnp.zeros_like(l_s)
        m_s[...] = jnp.full_like(m_s, -jnp.inf)

    @pl.when(bounded)
    def _():
        qt = qt_ref[0, 0]
        for c in range(n_chunks):
            rows = slice(c * tk, (c + 1) * tk)
            pr = jnp.exp2(jnp.dot(k_ref[rows, :], qt, preferred_element_type=F32) - bound)
            l_s[...] += jnp.sum(pr.reshape(tk // SUBLANES, SUBLANES, nq_cols), axis=0)
            acc_s[...] += jnp.dot(vt_ref[0, :, rows], pr.astype(BF16), preferred_element_type=F32)

    @pl.when(jnp.logical_not(bounded))
    def _():
        qt = qt_ref[0, 0]

        def chunk(c, _):
            off = pl.multiple_of(c * tk, tk)
            s = jnp.dot(k_ref[pl.ds(off, tk), :], qt, preferred_element_type=F32)
            m_old = m_s[...]
            m_new = jnp.maximum(m_old, jnp.max(s, axis=0, keepdims=True))
            alpha = jnp.exp2(m_old - m_new)
            pr = jnp.exp2(s - m_new)
            l_s[...] = alpha * l_s[...] + jnp.sum(pr.reshape(tk // SUBLANES, SUBLANES, nq_cols), axis=0)
            acc_s[...] = alpha * acc_s[...] + jnp.dot(vt_ref[0, :, pl.ds(off, tk)], pr.astype(BF16),
                                                      preferred_element_type=F32)
            m_s[...] = m_new
            return 0

        lax.fori_loop(0, n_chunks, chunk, 0)

    @pl.when(kv_blk == n_kv_blocks - 1)
    def _():
        out_t = acc_s[...] / jnp.sum(l_s[...], axis=0, keepdims=True)
        for g in range(GROUP):
            cols = slice(g * HEAD_DIM, (g + 1) * HEAD_DIM)
            o_ref[:, cols] = gb_ref[:, cols] * out_t[:, g * tq:(g + 1) * tq].T


def _attention(bound, qt, k, vt, gate, batch, seq_len, tq, tk, kv_block):
    t = k.shape[0]
    nq = seq_len // tq
    nkv = seq_len // kv_block
    half = GROUP * HEAD_DIM
    gate_col0 = D_MODEL // half
    return pl.pallas_call(
        functools.partial(_attn_kernel, tq=tq, tk=tk, n_chunks=kv_block // tk, n_kv_blocks=nkv),
        grid=(batch, N_KV_HEADS, nq, nkv),
        in_specs=[
            pl.BlockSpec(memory_space=pltpu.SMEM),
            pl.BlockSpec((1, 1, HEAD_DIM, GROUP * tq), lambda b, j, i, c: (b * nq + i, j, 0, 0)),
            pl.BlockSpec((kv_block, HEAD_DIM), lambda b, j, i, c: (b * nkv + c, j)),
            pl.BlockSpec((1, HEAD_DIM, kv_block), lambda b, j, i, c: (j, 0, b * nkv + c)),
            pl.BlockSpec((tq, half), lambda b, j, i, c: (b * nq + i, gate_col0 + j)),
        ],
        out_specs=pl.BlockSpec((tq, half), lambda b, j, i, c: (b * nq + i, j)),
        out_shape=jax.ShapeDtypeStruct((t, D_MODEL), F32),
        scratch_shapes=[pltpu.VMEM((1, GROUP * tq), F32), pltpu.VMEM((SUBLANES, GROUP * tq), F32),
                        pltpu.VMEM((HEAD_DIM, GROUP * tq), F32)],
        compiler_params=_cparams(("arbitrary", "arbitrary", "arbitrary", "arbitrary")), name="attention",
    )(bound, qt, k, vt, gate)


def _outproj_router_kernel(x_ref, ma_ref, mb_ref, wout_ref, gffn_ref, wr_ref, br_ref, cnt0_ref,
                           x1_ref, h2_ref, route_ref, route_t_ref, cnt_ref, cnt_s):
    tm = x_ref.shape[0]
    i = pl.program_id(0)

    @pl.when(i == 0)
    def _():
        cnt_s[...] = cnt0_ref[...]

    merged = (ma_ref[...] + mb_ref[...]).astype(BF16)
    x1 = x_ref[...] + jnp.dot(merged, wout_ref[...], preferred_element_type=F32)
    x1_ref[...] = x1
    h2 = x1 * lax.rsqrt(jnp.mean(x1 * x1, axis=-1, keepdims=True) + EPS) * gffn_ref[...]
    h2_ref[...] = _pack_bf16_pairs(h2)

    hi = h2.astype(BF16)
    lo = (h2 - hi.astype(F32)).astype(BF16)
    zt = lax.dot_general(wr_ref[...], jnp.concatenate([hi, hi, lo], axis=1), (((1,), (1,)), ((), ())),
                         preferred_element_type=F32) + br_ref[...]
    sub = lax.broadcasted_iota(jnp.int32, (ROUTE_ROWS, tm), 0)
    neg = -jnp.inf

    def first_argmax(v):
        m = jnp.max(v, axis=0, keepdims=True)
        return m, jnp.min(jnp.where(v == m, sub, ROUTE_ROWS), axis=0, keepdims=True)

    zg = jnp.where(sub < N_GROUPS, zt, neg)
    mg, grp = first_argmax(zg)
    p_sel = 1.0 / jnp.sum(jnp.exp(zg - mg), axis=0, keepdims=True)
    e_lo = EXPERT_ROW0 + EXPERTS_PER_GROUP * grp
    ze = jnp.where((sub >= e_lo) & (sub < e_lo + EXPERTS_PER_GROUP), zt, neg)
    m1, i1 = first_argmax(ze)
    m2, i2 = first_argmax(jnp.where(sub == i1, neg, ze))
    e2 = jnp.exp(m2 - m1)
    w0 = p_sel / (1.0 + e2)
    w1 = p_sel * e2 / (1.0 + e2)

    hot0 = sub == i1
    hot1 = sub == i2
    hot = jnp.where(hot0 | hot1, 1.0, 0.0)
    rr = lax.broadcasted_iota(jnp.int32, (tm, tm), 0)
    cc = lax.broadcasted_iota(jnp.int32, (tm, tm), 1)
    earlier = jnp.where(rr < cc, 1.0, 0.0).astype(BF16)
    cnt = cnt_s[...]
    before = (jnp.dot(hot.astype(BF16), earlier, preferred_element_type=F32)
              + jnp.concatenate([cnt] * (tm // LANES), axis=1))
    rank0 = jnp.sum(jnp.where(hot0, before, 0.0), axis=0, keepdims=True)
    rank1 = jnp.sum(jnp.where(hot1, before, 0.0), axis=0, keepdims=True)
    cnt_s[...] = cnt + jnp.sum(hot, axis=1, keepdims=True)
    cnt_ref[...] = cnt_s[...]

    eid0 = (i1 - EXPERT_ROW0).astype(F32)
    eid1 = (i2 - EXPERT_ROW0).astype(F32)
    zero = jnp.zeros_like(w0)
    rec_t = jnp.concatenate([eid0, eid1, w0, w1, rank0, rank1, zero, zero], axis=0)
    route_t_ref[...] = rec_t
    route_ref[...] = jnp.concatenate([rec_t, jnp.zeros((ROUTE_LANES - SUBLANES, tm), F32)], axis=0).T


def _outproj_router(x2, m_a, m_b, cnt0, p, tm):
    t = x2.shape[0]
    row = lambda i: (i, 0)
    return pl.pallas_call(
        _outproj_router_kernel, grid=(t // tm,),
        in_specs=[pl.BlockSpec((tm, D_MODEL), row), pl.BlockSpec((tm, D_MODEL), row), pl.BlockSpec((tm, D_MODEL), row),
                  _const_spec((D_MODEL, D_MODEL)), _const_spec((1, D_MODEL)), _const_spec((ROUTE_ROWS, 3 * D_MODEL)),
                  _const_spec((ROUTE_ROWS, tm)), _const_spec((ROUTE_ROWS, LANES))],
        out_specs=(pl.BlockSpec((tm, D_MODEL), row), pl.BlockSpec((tm, PACKED_COLS), row),
                   pl.BlockSpec((tm, ROUTE_LANES), row), pl.BlockSpec((SUBLANES, tm), lambda i: (0, i)),
                   _const_spec((ROUTE_ROWS, LANES))),
        out_shape=(jax.ShapeDtypeStruct((t, D_MODEL), F32), jax.ShapeDtypeStruct((t, PACKED_COLS), jnp.uint32),
                   jax.ShapeDtypeStruct((t, ROUTE_LANES), F32), jax.ShapeDtypeStruct((SUBLANES, t), F32),
                   jax.ShapeDtypeStruct((ROUTE_ROWS, LANES), F32)),
        scratch_shapes=[pltpu.VMEM((ROUTE_ROWS, LANES), F32)],
        compiler_params=_cparams(("arbitrary",)), name="outproj_router",
    )(x2, m_a, m_b, p["w_out"], p["g_ffn"], p["w_route"], p["b_route"], cnt0)


def _dispatch_kernel(dest_ref, h_ref, xs_in_ref, xs_ref, sem, *, tm, n_tok):
    del xs_in_ref
    base = pl.program_id(0) * tm

    def issue(r, _):
        for k in range(TOP_K):
            d = dest_ref[k * n_tok + base + r]
            pltpu.make_async_copy(h_ref.at[pl.ds(r, 1), :], xs_ref.at[pl.ds(d, 1), :], sem).start()
        return 0

    lax.fori_loop(0, tm, issue, 0, unroll=ROW_DMA_UNROLL)
    for k in range(TOP_K):
        pltpu.make_async_copy(h_ref, xs_ref.at[pl.ds(0, tm), :], sem).wait()


def _dispatch(dest_flat, h2, xs, tm):
    t = h2.shape[0]
    return pl.pallas_call(
        functools.partial(_dispatch_kernel, tm=tm, n_tok=t),
        grid_spec=pltpu.PrefetchScalarGridSpec(
            num_scalar_prefetch=1, grid=(t // tm,),
            in_specs=[pl.BlockSpec((tm, PACKED_COLS), lambda i, d: (i, 0)), pl.BlockSpec(memory_space=pl.ANY)],
            out_specs=pl.BlockSpec(memory_space=pl.ANY),
            scratch_shapes=[pltpu.SemaphoreType.DMA]),
        out_shape=jax.ShapeDtypeStruct(xs.shape, xs.dtype),
        input_output_aliases={2: 0},
        compiler_params=_cparams(("arbitrary",)), name="moe_dispatch",
    )(dest_flat, h2, xs)


def _experts_kernel(blk_e_ref, nxt_e_ref, slot_ref, n_used_ref, x_ref, wg_hbm, wu_hbm, wd_hbm, y_ref,
                    wg_f, wu_f, wd_f, wg_s, wu_s, wd_s, sems):
    i = pl.program_id(0)
    e = blk_e_ref[i]
    slot = slot_ref[i]
    first = jnp.logical_or(i == 0, e != blk_e_ref[jnp.maximum(i - 1, 0)])

    def weight_copies(expert, s):
        return (pltpu.make_async_copy(wg_hbm.at[expert], wg_f.at[s], sems.at[s, 0]),
                pltpu.make_async_copy(wu_hbm.at[expert], wu_f.at[s], sems.at[s, 1]),
                pltpu.make_async_copy(wd_hbm.at[expert], wd_f.at[s], sems.at[s, 2]))

    @pl.when(i == 0)
    def _():
        for cp in weight_copies(e, slot):
            cp.start()

    @pl.when(first)
    def _():
        for cp in weight_copies(e, slot):
            cp.wait()
        wg_s[...] = wg_f[slot].astype(BF16)
        wu_s[...] = wu_f[slot].astype(BF16)
        wd_s[...] = wd_f[slot].astype(BF16)
        nxt = nxt_e_ref[i]

        @pl.when(nxt != e)
        def _():
            for cp in weight_copies(nxt, 1 - slot):
                cp.start()

    @pl.when(i < n_used_ref[0])
    def _():
        xb = _unpack_bf16_pairs(x_ref[...]).astype(BF16)
        gate = jnp.dot(xb, wg_s[...], preferred_element_type=F32)
        up = jnp.dot(xb, wu_s[...], preferred_element_type=F32)
        act = (jax.nn.silu(gate) * up).astype(BF16)
        y_ref[...] = _pack_bf16_pairs(jnp.dot(act, wd_s[...], preferred_element_type=F32))

    @pl.when(i >= n_used_ref[0])
    def _():
        y_ref[...] = jnp.zeros_like(y_ref)


def _experts(blk_e, nxt_e, slot, n_used, xs, w_gate, w_up, w_down):
    n_rows = xs.shape[0]
    rb = MOE_ROWS
    row_blk = lambda i, *_: (i, 0)
    return pl.pallas_call(
        _experts_kernel,
        grid_spec=pltpu.PrefetchScalarGridSpec(
            num_scalar_prefetch=4, grid=(n_rows // rb,),
            in_specs=[pl.BlockSpec((rb, PACKED_COLS), row_blk),
                      pl.BlockSpec(memory_space=pl.ANY), pl.BlockSpec(memory_space=pl.ANY),
                      pl.BlockSpec(memory_space=pl.ANY)],
            out_specs=pl.BlockSpec((rb, PACKED_COLS), row_blk),
            scratch_shapes=[pltpu.VMEM((2, D_MODEL, D_EXPERT), F32), pltpu.VMEM((2, D_MODEL, D_EXPERT), F32),
                            pltpu.VMEM((2, D_EXPERT, D_MODEL), F32),
                            pltpu.VMEM((D_MODEL, D_EXPERT), BF16), pltpu.VMEM((D_MODEL, D_EXPERT), BF16),
                            pltpu.VMEM((D_EXPERT, D_MODEL), BF16), pltpu.SemaphoreType.DMA((2, 3))]),
        out_shape=jax.ShapeDtypeStruct((n_rows, PACKED_COLS), jnp.uint32),
        compiler_params=_cparams(("arbitrary",)), name="moe_experts",
    )(blk_e, nxt_e, slot, n_used, xs, w_gate, w_up, w_down)


def _combine_kernel(dest_ref, x1_ref, route_ref, ys_ref, o_ref, gbuf, sems, *, tm, n_tok):
    step = pl.program_id(0)
    n_steps = pl.num_programs(0)

    def gather_rows(for_step, slot):
        base = for_step * tm

        def issue(r, _):
            for k in range(TOP_K):
                d = dest_ref[k * n_tok + base + r]
                pltpu.make_async_copy(ys_ref.at[pl.ds(d, 1), :], gbuf.at[slot, k, pl.ds(r, 1), :],
                                      sems.at[slot]).start()
            return 0

        lax.fori_loop(0, tm, issue, 0, unroll=ROW_DMA_UNROLL)

    @pl.when(step == 0)
    def _():
        gather_rows(0, 0)

    @pl.when(step + 1 < n_steps)
    def _():
        gather_rows(step + 1, (step + 1) % 2)

    slot = step % 2
    for k in range(TOP_K):
        pltpu.make_async_copy(ys_ref.at[pl.ds(0, tm), :], gbuf.at[slot, k], sems.at[slot]).wait()
    w0 = route_ref[:, 2:3]
    w1 = route_ref[:, 3:4]
    o_ref[...] = x1_ref[...] + (_unpack_bf16_pairs(gbuf[slot, 0]) * w0 + _unpack_bf16_pairs(gbuf[slot, 1]) * w1)


def _combine(dest_flat, x1, route, ys, tm):
    t = x1.shape[0]
    return pl.pallas_call(
        functools.partial(_combine_kernel, tm=tm, n_tok=t),
        grid_spec=pltpu.PrefetchScalarGridSpec(
            num_scalar_prefetch=1, grid=(t // tm,),
            in_specs=[pl.BlockSpec((tm, D_MODEL), lambda i, d: (i, 0)),
                      pl.BlockSpec((tm, ROUTE_LANES), lambda i, d: (i, 0)),
                      pl.BlockSpec(memory_space=pl.ANY)],
            out_specs=pl.BlockSpec((tm, D_MODEL), lambda i, d: (i, 0)),
            scratch_shapes=[pltpu.VMEM((2, TOP_K, tm, PACKED_COLS), jnp.uint32), pltpu.SemaphoreType.DMA((2,))]),
        out_shape=jax.ShapeDtypeStruct((t, D_MODEL), F32),
        compiler_params=_cparams(("arbitrary",)), name="moe_combine",
    )(dest_flat, x1, route, ys)


def _rope_tables(seq_len):
    inv = ROPE_THETA ** (-jnp.arange(0, AXIS_DIM, 2, dtype=F32) / AXIS_DIM)
    n_grid_rows = seq_len // GRID_W
    ang_r = jnp.arange(n_grid_rows, dtype=F32)[:, None] * inv
    ang_c = jnp.arange(GRID_W, dtype=F32)[:, None] * inv
    small = (jnp.cos(ang_r), jnp.sin(ang_r), jnp.cos(ang_c), jnp.sin(ang_c))

    def tables(pos_axis):
        reps = (n_grid_rows, 1) if pos_axis == 0 else (1, n_grid_rows)
        cos_r, sin_r, cos_c, sin_c = (a if pos_axis == 0 else a.T for a in small)
        cos_r, sin_r = jnp.repeat(cos_r, GRID_W, axis=pos_axis), jnp.repeat(sin_r, GRID_W, axis=pos_axis)
        cos_c, sin_c = jnp.tile(cos_c, reps), jnp.tile(sin_c, reps)
        return (jnp.concatenate([cos_r, cos_r, cos_c, cos_c], axis=1 - pos_axis),
                jnp.concatenate([-sin_r, sin_r, -sin_c, sin_c], axis=1 - pos_axis))

    return tables(0) + tables(1)


def _prepare_params(l, max_seq, tm, g_mix, w_in, b_gate, conv_w, conv_b, lru_wa, lru_ba, lru_wx, lru_bx, lru_lam,
                    q_gain, k_gain, w_out, g_ffn, w_rg, b_rg, w_re, b_re):
    c_u, c_gr, c_q, c_k, c_v = (LRU_WIDTH, 2 * LRU_WIDTH, 2 * LRU_WIDTH + D_MODEL,
                                2 * LRU_WIDTH + D_MODEL + ATTN_KV, 2 * LRU_WIDTH + D_MODEL + 2 * ATTN_KV)
    w = w_in[l]
    w_nat = jnp.concatenate([w[:, :c_gr], w[:, c_q:c_k], w[:, c_v:]], axis=1).astype(BF16)
    w_tr = jnp.concatenate([w[:, c_gr:c_q], w[:, c_k:c_v]], axis=1).T.astype(BF16)
    cos, sin, cos_t, sin_t = _rope_tables(max_seq)
    pad = ROUTE_ROWS - N_GROUPS - N_EXPERTS
    w_route = jnp.concatenate([w_rg[l], w_re[l], jnp.zeros((D_MODEL, pad), F32)], axis=1).T
    w_route_hi = w_route.astype(BF16)
    w_route_lo = (w_route - w_route_hi.astype(F32)).astype(BF16)
    w_route = jnp.concatenate([w_route_hi, w_route_lo, w_route_hi], axis=1)
    b_route = jnp.broadcast_to(jnp.concatenate([b_rg[l], b_re[l], jnp.zeros((pad,), F32)])[:, None], (ROUTE_ROWS, tm))
    lru_w = (0.5 * jnp.concatenate([lru_wa[l], lru_wx[l]], axis=-1)).astype(BF16)
    score_bound = (SCORE_BOUND_SLACK * LOG2E * math.sqrt(HEAD_DIM)
                   * jnp.max(jnp.abs(q_gain[l])) * jnp.max(jnp.abs(k_gain[l]))).reshape(1)
    return dict(
        score_bound=score_bound,
        g_mix=g_mix[l][None, :], w_nat=w_nat, w_tr=w_tr, b_gate=b_gate[l][None, :],
        k_gain=k_gain[l][None, :], q_gain_t=jnp.broadcast_to(q_gain[l][:, None], (HEAD_DIM, WIDE_TILE)),
        cos=cos, sin=sin, cos_t=cos_t, sin_t=sin_t,
        conv_w=conv_w[l], conv_b=conv_b[l][None, :], lru_w=lru_w,
        lru_ba=lru_ba[l].reshape(2, 1, LRU_WIDTH), lru_bx=lru_bx[l].reshape(2, 1, LRU_WIDTH),
        lru_lam=lru_lam[l].reshape(2, 1, LRU_WIDTH),
        w_out=w_out[l].astype(BF16), g_ffn=g_ffn[l][None, :], w_route=w_route, b_route=b_route,
    )


def _mixer(x, p, cnt0, tm, tk):
    batch, seq_len, _ = x.shape
    x2 = x.reshape(batch * seq_len, D_MODEL)
    u, ggr, k, gate, qt, vt = _inproj(x2, seq_len, p, WIDE_TILE, tm)
    h_fwd = _lru_scan(u, gate, ggr, None, p, 0, batch, seq_len, WIDE_TILE, reverse=False)
    m_a = _lru_scan(u, gate, ggr, h_fwd, p, 1, batch, seq_len, WIDE_TILE, reverse=True)
    m_b = _attention(p["score_bound"], qt, k, vt, gate, batch, seq_len, tm, min(tk, seq_len),
                     min(KV_BLOCK, seq_len))
    return _outproj_router(x2, m_a, m_b, cnt0, p, tm)


def _layer(xs_in, l, weights):
    (g_mix, w_in, b_gate, conv_w, conv_b, lru_wa, lru_ba, lru_wx, lru_bx, lru_lam,
     q_gain, k_gain, w_out, g_ffn, w_rg, b_rg, w_re, b_re, w_gate, w_up, w_down) = weights
    tm = ROW_TILE
    max_seq = max(x.shape[1] for x in xs_in)
    p = _prepare_params(l, max_seq, tm, g_mix, w_in, b_gate, conv_w, conv_b, lru_wa, lru_ba, lru_wx, lru_bx,
                        lru_lam, q_gain, k_gain, w_out, g_ffn, w_rg, b_rg, w_re, b_re)

    cnt = jnp.zeros((ROUTE_ROWS, LANES), F32)
    x1s, h2s, routes, routes_t = [], [], [], []
    for x in xs_in:
        x1, h2, route, route_t, cnt = _mixer(x, p, cnt, tm, KV_CHUNK)
        x1s.append(x1)
        h2s.append(h2)
        routes.append(route)
        routes_t.append(route_t)

    n_tok = sum(x1.shape[0] for x1 in x1s)
    n_rows = n_tok * TOP_K + N_EXPERTS * MOE_ROWS
    n_blk = n_rows // MOE_ROWS
    counts = cnt[EXPERT_ROW0:EXPERT_ROW0 + N_EXPERTS, 0].astype(jnp.int32)
    padded = (counts + MOE_ROWS - 1) // MOE_ROWS * MOE_ROWS
    pend = jnp.cumsum(padded)
    pstart = pend - padded
    blk_idx = jnp.arange(n_blk, dtype=jnp.int32)
    n_used = (pend[-1:] // MOE_ROWS).astype(jnp.int32)
    blk_e = jnp.minimum(jnp.sum((pend[None, :] <= (blk_idx * MOE_ROWS)[:, None]).astype(jnp.int32), axis=1),
                        N_EXPERTS - 1)
    blk_e = jnp.where(blk_idx < n_used, blk_e, blk_e[jnp.maximum(n_used[0] - 1, 0)])
    starts = jnp.concatenate([jnp.ones((1,), jnp.int32), (blk_e[1:] != blk_e[:-1]).astype(jnp.int32)])
    slot = (jnp.cumsum(starts) - 1) % 2
    later_start = lax.cummin(jnp.where(starts == 1, blk_idx, n_blk), reverse=True)
    next_start = jnp.concatenate([later_start[1:], jnp.full((1,), n_blk, jnp.int32)])
    nxt_e = jnp.where(next_start < n_blk, blk_e[jnp.minimum(next_start, n_blk - 1)], blk_e)

    dests = []
    xs = jnp.zeros((n_rows, PACKED_COLS), jnp.uint32)
    for h2, route_t in zip(h2s, routes_t):
        eid = route_t[0:TOP_K].astype(jnp.int32)
        rank = route_t[4:4 + TOP_K].astype(jnp.int32)
        experts = jnp.arange(N_EXPERTS, dtype=jnp.int32)[:, None, None]
        dest = (rank + jnp.sum(jnp.where(eid[None] == experts, pstart[:, None, None], 0), axis=0)).reshape(-1)
        dests.append(dest)
        xs = _dispatch(dest, h2, xs, WIDE_TILE)
    ys = _experts(blk_e, nxt_e, slot.astype(jnp.int32), n_used, xs, w_gate[l], w_up[l], w_down[l])
    outs = []
    for x, x1, route, dest in zip(xs_in, x1s, routes, dests):
        outs.append(_combine(dest, x1, route, ys, WIDE_TILE).reshape(x.shape))
    return outs


def kernel(x_prompt, x_sample, g_mix, w_in, b_gate, conv_w, conv_b, lru_wa, lru_ba, lru_wx, lru_bx, lru_lam,
           q_gain, k_gain, w_out, g_ffn, w_rg, b_rg, w_re, b_re, w_gate, w_up, w_down):
    weights = (g_mix, w_in, b_gate, conv_w, conv_b, lru_wa, lru_ba, lru_wx, lru_bx, lru_lam,
               q_gain, k_gain, w_out, g_ffn, w_rg, b_rg, w_re, b_re, w_gate, w_up, w_down)
    xs = [x_prompt, x_sample]
    for l in range(g_mix.shape[0]):
        xs = _layer(xs, l, weights)
    return tuple(xs)
```

```python
import functools
import math

import jax
import jax.numpy as jnp
import numpy as np
from jax import lax
from jax.experimental import pallas as pl
from jax.experimental.pallas import tpu as pltpu

F32 = jnp.float32
BF16 = jnp.bfloat16

D_MODEL = 1024
N_HEADS = 8
N_KV_HEADS = 2
GROUP = N_HEADS // N_KV_HEADS
HEAD_DIM = D_MODEL // N_HEADS
AXIS_DIM = HEAD_DIM // 2
N_FREQ = AXIS_DIM // 2
GRID_W = 64
ROPE_THETA = 10000.0
LRU_WIDTH = D_MODEL
LRU_BLOCKS = 8
LRU_BLOCK_W = LRU_WIDTH // LRU_BLOCKS
CONV_W = 4
CONV_PAD_LEFT = 2
LRU_C = 8.0
N_GROUPS = 4
EXPERTS_PER_GROUP = 8
N_EXPERTS = N_GROUPS * EXPERTS_PER_GROUP
TOP_K = 2
D_EXPERT = D_MODEL // 2
ATTN_KV = N_KV_HEADS * HEAD_DIM
EPS = 1e-6

LANES = 128
SUBLANES = 8
VMEM_LIMIT_BYTES = 56 * 1024 * 1024

ROW_TILE = 256
WIDE_TILE = 512
KV_CHUNK = 2048
KV_BLOCK = 16384
MOE_ROWS = 512
ROW_DMA_UNROLL = 8
ROUTE_LANES = LANES
ROUTE_ROWS = 64
PACKED_COLS = D_MODEL // 2
EXPERT_ROW0 = N_GROUPS

NAT_COLS = 2 * LRU_WIDTH + ATTN_KV + 2 * D_MODEL
TR_ROWS = D_MODEL + ATTN_KV
LOG2E = math.log2(math.e)
SAFE_EXP2_RANGE = 100.0
SCORE_BOUND_SLACK = 1.01


def _cparams(semantics):
    return pltpu.CompilerParams(dimension_semantics=semantics, vmem_limit_bytes=VMEM_LIMIT_BYTES)


def _const_spec(shape, single_buffer=False):
    nd = len(shape)
    mode = pl.Buffered(1) if single_buffer else None
    return pl.BlockSpec(shape, lambda *_: (0,) * nd, pipeline_mode=mode)


def _sigmoid(x):
    return 0.5 * jnp.tanh(0.5 * x) + 0.5


def _pack_bf16_pairs(x):
    n = x.shape[1] // 2
    lo = lax.bitcast_convert_type(x[:, :n].astype(BF16).astype(F32), jnp.uint32)
    hi = lax.bitcast_convert_type(x[:, n:].astype(BF16).astype(F32), jnp.uint32)
    return (hi & jnp.uint32(0xFFFF0000)) | (lo >> 16)


def _unpack_bf16_pairs(w):
    lo = lax.bitcast_convert_type(w << 16, F32)
    hi = lax.bitcast_convert_type(w & jnp.uint32(0xFFFF0000), F32)
    return jnp.concatenate([lo, hi], axis=1)


def _swap_halves_rows(x):
    return jnp.concatenate([x[N_FREQ:AXIS_DIM], x[0:N_FREQ], x[AXIS_DIM + N_FREQ:], x[AXIS_DIM:AXIS_DIM + N_FREQ]], axis=0)


def _inproj_kernel(x_ref, gmix_ref, wnat_ref, wtr_ref, bgate_ref, kgain_ref, qgain_ref,
                   cos_ref, sin_ref, cost_ref, sint_ref,
                   u_ref, ggr_ref, k_ref, gate_ref, qt_ref, vt_ref):
    tm = x_ref.shape[0]
    x = x_ref[...]
    h = (x * lax.rsqrt(jnp.mean(x * x, axis=-1, keepdims=True) + EPS) * gmix_ref[...]).astype(BF16)

    def nat(lo, hi):
        return jnp.dot(h, wnat_ref[:, lo:hi], preferred_element_type=F32)

    u_ref[...] = nat(0, LRU_WIDTH)
    ggr_ref[...] = jax.nn.gelu(nat(LRU_WIDTH, 2 * LRU_WIDTH))
    k0 = 2 * LRU_WIDTH
    kraw = nat(k0, k0 + ATTN_KV)
    g0 = k0 + ATTN_KV
    gate_ref[...] = _sigmoid(nat(g0, g0 + 2 * D_MODEL) + bgate_ref[...])

    cos = cos_ref[...]
    sin = sin_ref[...]
    lane = lax.broadcasted_iota(jnp.int32, (tm, HEAD_DIM), 1)
    first_half = (lane % AXIS_DIM) < N_FREQ
    for j in range(N_KV_HEADS):
        kj = kraw[:, j * HEAD_DIM:(j + 1) * HEAD_DIM]
        kn = kj * lax.rsqrt(jnp.mean(kj * kj, axis=-1, keepdims=True) + EPS) * kgain_ref[...]
        partner = jnp.where(first_half, pltpu.roll(kn, HEAD_DIM - N_FREQ, 1), pltpu.roll(kn, N_FREQ, 1))
        k_ref[:, j * HEAD_DIM:(j + 1) * HEAD_DIM] = (kn * cos + partner * sin).astype(BF16)

    zt = lax.dot_general(wtr_ref[...], h, (((1,), (1,)), ((), ())), preferred_element_type=F32)
    cost = cost_ref[...]
    sint = sint_ref[...]
    qgain = qgain_ref[...]
    qscale = (HEAD_DIM ** -0.5) * LOG2E
    for hd in range(N_HEADS):
        xq = zt[hd * HEAD_DIM:(hd + 1) * HEAD_DIM, :]
        xn = xq * lax.rsqrt(jnp.mean(xq * xq, axis=0, keepdims=True) + EPS) * qgain
        rot = (xn * cost + _swap_halves_rows(xn) * sint) * qscale
        g = hd % GROUP
        tq = qt_ref.shape[3] // GROUP
        for qb in range(tm // tq):
            qt_ref[qb, hd // GROUP, :, g * tq:(g + 1) * tq] = rot[:, qb * tq:(qb + 1) * tq].astype(BF16)
    for j in range(N_KV_HEADS):
        r0 = D_MODEL + j * HEAD_DIM
        vt_ref[j] = zt[r0:r0 + HEAD_DIM, :].astype(BF16)


def _inproj(x2, seq_len, p, tm, tq):
    t = x2.shape[0]
    n_pos = seq_len // tm
    grid = (t // tm,)
    row = lambda i: (i, 0)
    in_specs = [
        pl.BlockSpec((tm, D_MODEL), row),
        _const_spec((1, D_MODEL)),
        _const_spec((D_MODEL, NAT_COLS), single_buffer=True),
        _const_spec((TR_ROWS, D_MODEL), single_buffer=True),
        _const_spec((1, 2 * D_MODEL)),
        _const_spec((1, HEAD_DIM)),
        _const_spec((HEAD_DIM, tm)),
        pl.BlockSpec((tm, HEAD_DIM), lambda i: (i % n_pos, 0)),
        pl.BlockSpec((tm, HEAD_DIM), lambda i: (i % n_pos, 0)),
        pl.BlockSpec((HEAD_DIM, tm), lambda i: (0, i % n_pos)),
        pl.BlockSpec((HEAD_DIM, tm), lambda i: (0, i % n_pos)),
    ]
    out_shape = (
        jax.ShapeDtypeStruct((t, LRU_WIDTH), F32),
        jax.ShapeDtypeStruct((t, LRU_WIDTH), F32),
        jax.ShapeDtypeStruct((t, ATTN_KV), BF16),
        jax.ShapeDtypeStruct((t, 2 * D_MODEL), F32),
        jax.ShapeDtypeStruct((t // tq, N_KV_HEADS, HEAD_DIM, GROUP * tq), BF16),
        jax.ShapeDtypeStruct((N_KV_HEADS, HEAD_DIM, t), BF16),
    )
    out_specs = (
        pl.BlockSpec((tm, LRU_WIDTH), row),
        pl.BlockSpec((tm, LRU_WIDTH), row),
        pl.BlockSpec((tm, ATTN_KV), row),
        pl.BlockSpec((tm, 2 * D_MODEL), row),
        pl.BlockSpec((tm // tq, N_KV_HEADS, HEAD_DIM, GROUP * tq), lambda i: (i, 0, 0, 0)),
        pl.BlockSpec((N_KV_HEADS, HEAD_DIM, tm), lambda i: (0, 0, i)),
    )
    return pl.pallas_call(
        _inproj_kernel, grid=grid, in_specs=in_specs, out_specs=out_specs, out_shape=out_shape,
        compiler_params=_cparams(("arbitrary",)), name="inproj",
    )(x2, p["g_mix"], p["w_nat"], p["w_tr"], p["b_gate"], p["k_gain"], p["q_gain_t"],
      p["cos"], p["sin"], p["cos_t"], p["sin_t"])


def _lru_kernel(*refs, reverse, tm, nt):
    if reverse:
        (up_ref, uc_ref, un_ref, cw_ref, cb_ref, w_ref, ba_ref, bx_ref, lam_ref, hf_ref, ggr_ref, ga_ref,
         out_ref, ubuf, a_s, g_s, carry) = refs
    else:
        (up_ref, uc_ref, un_ref, cw_ref, cb_ref, w_ref, ba_ref, bx_ref, lam_ref,
         out_ref, ubuf, a_s, g_s, carry) = refs
    i = pl.program_id(1)
    ti = (nt - 1 - i) if reverse else i

    @pl.when(i == 0)
    def _():
        carry[...] = jnp.zeros_like(carry)

    ubuf[0:SUBLANES] = jnp.where(ti == 0, 0.0, up_ref[...])
    ubuf[SUBLANES:SUBLANES + tm] = uc_ref[...]
    ubuf[SUBLANES + tm:2 * SUBLANES + tm] = jnp.where(ti == nt - 1, 0.0, un_ref[...])
    cw = cw_ref[...]
    ub = ubuf[...]
    n_buf = tm + 2 * SUBLANES
    xc = cb_ref[...]
    for j in range(CONV_W):
        back = CONV_PAD_LEFT - j
        tap = ub if back == 0 else pltpu.roll(ub, back % n_buf, 0)
        xc = xc + cw[j:j + 1] * tap[SUBLANES:SUBLANES + tm]

    xcb = xc.astype(BF16)
    lam = lam_ref[...]
    half_c = (0.5 * LRU_C) * (jnp.minimum(lam, 0.0) - jnp.log1p(jnp.exp(-jnp.abs(lam))))
    half_ba = 0.5 * ba_ref[...]
    half_bx = 0.5 * bx_ref[...]
    half_xc = 0.5 * xc
    for hb in range(LRU_BLOCKS):
        sl = slice(hb * LRU_BLOCK_W, (hb + 1) * LRU_BLOCK_W)
        gz = jnp.dot(xcb[:, sl], w_ref[hb], preferred_element_type=F32)
        tr = jnp.tanh(gz[:, :LRU_BLOCK_W] + half_ba[:, sl])
        ti = jnp.tanh(gz[:, LRU_BLOCK_W:] + half_bx[:, sl])
        log_a = tr * half_c[:, sl] + half_c[:, sl]
        a = jnp.exp(log_a)
        a_s[:, sl] = a
        y = jnp.tanh(log_a) * (-1.0 - a * a)
        root = jnp.where(y > 0.0, y * lax.rsqrt(y), 0.0)
        g_s[:, sl] = root * (ti * half_xc[:, sl] + half_xc[:, sl])

    n_chunk = tm // SUBLANES
    srow = lax.broadcasted_iota(jnp.int32, (SUBLANES, LRU_WIDTH), 0)

    def chunk(c, h_prev):
        ci = (n_chunk - 1 - c) if reverse else c
        off = pl.multiple_of(ci * SUBLANES, SUBLANES)
        a = a_s[pl.ds(off, SUBLANES), :]
        b = g_s[pl.ds(off, SUBLANES), :]
        edge = (SUBLANES - 1) if reverse else 0
        b = b + jnp.where(srow == edge, a * h_prev, 0.0)
        for d in (1, 2, 4):
            shift = (SUBLANES - d) if reverse else d
            valid = (srow < SUBLANES - d) if reverse else (srow >= d)
            b = a * jnp.where(valid, pltpu.roll(b, shift, 0), 0.0) + b
            if d != 4:
                a = a * pltpu.roll(a, shift, 0)
        h = b
        if reverse:
            rows = pl.ds(off, SUBLANES)
            out_ref[rows, :] = ga_ref[rows, :] * (ggr_ref[rows, :] * (hf_ref[rows, :] + h))
            return h[0:1]
        out_ref[pl.ds(off, SUBLANES), :] = h
        return h[SUBLANES - 1:SUBLANES]

    carry[...] = lax.fori_loop(0, n_chunk, chunk, carry[...])


def _lru_scan(u, gate, ggr, h_fwd, p, d, batch, seq_len, tm, reverse):
    t = u.shape[0]
    nt = seq_len // tm
    per8 = tm // SUBLANES
    n8 = t // SUBLANES

    def tile(b, i):
        return b * nt + ((nt - 1 - i) if reverse else i)

    cur = lambda b, i: (tile(b, i), 0)
    prev = lambda b, i: (jnp.maximum(tile(b, i) * per8 - 1, 0), 0)
    nxt = lambda b, i: (jnp.minimum((tile(b, i) + 1) * per8, n8 - 1), 0)
    const2 = lambda b, i: (0, 0)
    in_specs = [
        pl.BlockSpec((SUBLANES, LRU_WIDTH), prev),
        pl.BlockSpec((tm, LRU_WIDTH), cur),
        pl.BlockSpec((SUBLANES, LRU_WIDTH), nxt),
        pl.BlockSpec((CONV_W, LRU_WIDTH), const2),
        pl.BlockSpec((1, LRU_WIDTH), const2),
        pl.BlockSpec((LRU_BLOCKS, LRU_BLOCK_W, 2 * LRU_BLOCK_W), lambda b, i: (0, 0, 0)),
        pl.BlockSpec((1, LRU_WIDTH), const2),
        pl.BlockSpec((1, LRU_WIDTH), const2),
        pl.BlockSpec((1, LRU_WIDTH), const2),
    ]
    args = [u, u, u, p["conv_w"], p["conv_b"], p["lru_w"][d], p["lru_ba"][d], p["lru_bx"][d], p["lru_lam"][d]]
    if reverse:
        in_specs += [pl.BlockSpec((tm, LRU_WIDTH), cur)] * 3
        args += [h_fwd, ggr, gate]
    return pl.pallas_call(
        functools.partial(_lru_kernel, reverse=reverse, tm=tm, nt=nt),
        grid=(batch, nt), in_specs=in_specs,
        out_specs=pl.BlockSpec((tm, LRU_WIDTH), cur),
        out_shape=jax.ShapeDtypeStruct((t, LRU_WIDTH), F32),
        scratch_shapes=[pltpu.VMEM((tm + 2 * SUBLANES, LRU_WIDTH), F32), pltpu.VMEM((tm, LRU_WIDTH), F32),
                        pltpu.VMEM((tm, LRU_WIDTH), F32), pltpu.VMEM((1, LRU_WIDTH), F32)],
        compiler_params=_cparams(("arbitrary", "arbitrary")), name="lru_bwd" if reverse else "lru_fwd",
    )(*args)


def _attn_kernel(bound_ref, qt_ref, k_ref, vt_ref, gb_ref, o_ref, m_s, l_s, acc_s, *, tq, tk, n_chunks, n_kv_blocks):
    nq_cols = GROUP * tq
    kv_blk = pl.program_id(3)
    bound = bound_ref[0]
    bounded = 2.0 * bound <= SAFE_EXP2_RANGE

    @pl.when(kv_blk == 0)
    def _():
        acc_s[...] = jnp.zeros_like(acc_s)
        l_s[...] = jnp.zeros_like(l_s)
        m_s[...] = jnp.full_like(m_s, -jnp.inf)

    @pl.when(bounded)
    def _():
        qt = qt_ref[0, 0]
        for c in range(n_chunks):
            rows = slice(c * tk, (c + 1) * tk)
            pr = jnp.exp2(jnp.dot(k_ref[rows, :], qt, preferred_element_type=F32) - bound)
            l_s[...] += jnp.sum(pr.reshape(tk // SUBLANES, SUBLANES, nq_cols), axis=0)
            acc_s[...] += jnp.dot(vt_ref[0, :, rows], pr.astype(BF16), preferred_element_type=F32)

    @pl.when(jnp.logical_not(bounded))
    def _():
        qt = qt_ref[0, 0]

        def chunk(c, _):
            off = pl.multiple_of(c * tk, tk)
            s = jnp.dot(k_ref[pl.ds(off, tk), :], qt, preferred_element_type=F32)
            m_old = m_s[...]
            m_new = jnp.maximum(m_old, jnp.max(s, axis=0, keepdims=True))
            alpha = jnp.exp2(m_old - m_new)
            pr = jnp.exp2(s - m_new)
            l_s[...] = alpha * l_s[...] + jnp.sum(pr.reshape(tk // SUBLANES, SUBLANES, nq_cols), axis=0)
            acc_s[...] = alpha * acc_s[...] + jnp.dot(vt_ref[0, :, pl.ds(off, tk)], pr.astype(BF16),
                                                      preferred_element_type=F32)
            m_s[...] = m_new
            return 0

        lax.fori_loop(0, n_chunks, chunk, 0)

    @pl.when(kv_blk == n_kv_blocks - 1)
    def _():
        out_t = acc_s[...] / jnp.sum(l_s[...], axis=0, keepdims=True)
        for g in range(GROUP):
            cols = slice(g * HEAD_DIM, (g + 1) * HEAD_DIM)
            o_ref[:, cols] = gb_ref[:, cols] * out_t[:, g * tq:(g + 1) * tq].T


def _attention(bound, qt, k, vt, gate, batch, seq_len, tq, tk, kv_block):
    t = k.shape[0]
    nq = seq_len // tq
    nkv = seq_len // kv_block
    half = GROUP * HEAD_DIM
    gate_col0 = D_MODEL // half
    return pl.pallas_call(
        functools.partial(_attn_kernel, tq=tq, tk=tk, n_chunks=kv_block // tk, n_kv_blocks=nkv),
        grid=(batch, N_KV_HEADS, nq, nkv),
        in_specs=[
            pl.BlockSpec(memory_space=pltpu.SMEM),
            pl.BlockSpec((1, 1, HEAD_DIM, GROUP * tq), lambda b, j, i, c: (b * nq + i, j, 0, 0)),
            pl.BlockSpec((kv_block, HEAD_DIM), lambda b, j, i, c: (b * nkv + c, j)),
            pl.BlockSpec((1, HEAD_DIM, kv_block), lambda b, j, i, c: (j, 0, b * nkv + c)),
            pl.BlockSpec((tq, half), lambda b, j, i, c: (b * nq + i, gate_col0 + j)),
        ],
        out_specs=pl.BlockSpec((tq, half), lambda b, j, i, c: (b * nq + i, j)),
        out_shape=jax.ShapeDtypeStruct((t, D_MODEL), F32),
        scratch_shapes=[pltpu.VMEM((1, GROUP * tq), F32), pltpu.VMEM((SUBLANES, GROUP * tq), F32),
                        pltpu.VMEM((HEAD_DIM, GROUP * tq), F32)],
        compiler_params=_cparams(("arbitrary", "arbitrary", "arbitrary", "arbitrary")), name="attention",
    )(bound, qt, k, vt, gate)


def _outproj_router_kernel(x_ref, ma_ref, mb_ref, wout_ref, gffn_ref, wr_ref, br_ref, cnt0_ref,
                           x1_ref, h2_ref, route_ref, route_t_ref, cnt_ref, cnt_s):
    tm = x_ref.shape[0]
    i = pl.program_id(0)

    @pl.when(i == 0)
    def _():
        cnt_s[...] = cnt0_ref[...]

    merged = (ma_ref[...] + mb_ref[...]).astype(BF16)
    x1 = x_ref[...] + jnp.dot(merged, wout_ref[...], preferred_element_type=F32)
    x1_ref[...] = x1
    h2 = x1 * lax.rsqrt(jnp.mean(x1 * x1, axis=-1, keepdims=True) + EPS) * gffn_ref[...]
    h2_ref[...] = _pack_bf16_pairs(h2)

    hi = h2.astype(BF16)
    lo = (h2 - hi.astype(F32)).astype(BF16)
    zt = lax.dot_general(wr_ref[...], jnp.concatenate([hi, hi, lo], axis=1), (((1,), (1,)), ((), ())),
                         preferred_element_type=F32) + br_ref[...]
    sub = lax.broadcasted_iota(jnp.int32, (ROUTE_ROWS, tm), 0)
    neg = -jnp.inf

    def first_argmax(v):
        m = jnp.max(v, axis=0, keepdims=True)
        return m, jnp.min(jnp.where(v == m, sub, ROUTE_ROWS), axis=0, keepdims=True)

    zg = jnp.where(sub < N_GROUPS, zt, neg)
    mg, grp = first_argmax(zg)
    p_sel = 1.0 / jnp.sum(jnp.exp(zg - mg), axis=0, keepdims=True)
    e_lo = EXPERT_ROW0 + EXPERTS_PER_GROUP * grp
    ze = jnp.where((sub >= e_lo) & (sub < e_lo + EXPERTS_PER_GROUP), zt, neg)
    m1, i1 = first_argmax(ze)
    m2, i2 = first_argmax(jnp.where(sub == i1, neg, ze))
    e2 = jnp.exp(m2 - m1)
    w0 = p_sel / (1.0 + e2)
    w1 = p_sel * e2 / (1.0 + e2)

    hot0 = sub == i1
    hot1 = sub == i2
    hot = jnp.where(hot0 | hot1, 1.0, 0.0)
    rr = lax.broadcasted_iota(jnp.int32, (tm, tm), 0)
    cc = lax.broadcasted_iota(jnp.int32, (tm, tm), 1)
    earlier = jnp.where(rr < cc, 1.0, 0.0).astype(BF16)
    cnt = cnt_s[...]
    before = (jnp.dot(hot.astype(BF16), earlier, preferred_element_type=F32)
              + jnp.concatenate([cnt] * (tm // LANES), axis=1))
    rank0 = jnp.sum(jnp.where(hot0, before, 0.0), axis=0, keepdims=True)
    rank1 = jnp.sum(jnp.where(hot1, before, 0.0), axis=0, keepdims=True)
    cnt_s[...] = cnt + jnp.sum(hot, axis=1, keepdims=True)
    cnt_ref[...] = cnt_s[...]

    eid0 = (i1 - EXPERT_ROW0).astype(F32)
    eid1 = (i2 - EXPERT_ROW0).astype(F32)
    zero = jnp.zeros_like(w0)
    rec_t = jnp.concatenate([eid0, eid1, w0, w1, rank0, rank1, zero, zero], axis=0)
    route_t_ref[...] = rec_t
    route_ref[...] = jnp.concatenate([rec_t, jnp.zeros((ROUTE_LANES - SUBLANES, tm), F32)], axis=0).T


def _outproj_router(x2, m_a, m_b, cnt0, p, tm):
    t = x2.shape[0]
    row = lambda i: (i, 0)
    return pl.pallas_call(
        _outproj_router_kernel, grid=(t // tm,),
        in_specs=[pl.BlockSpec((tm, D_MODEL), row), pl.BlockSpec((tm, D_MODEL), row), pl.BlockSpec((tm, D_MODEL), row),
                  _const_spec((D_MODEL, D_MODEL)), _const_spec((1, D_MODEL)), _const_spec((ROUTE_ROWS, 3 * D_MODEL)),
                  _const_spec((ROUTE_ROWS, tm)), _const_spec((ROUTE_ROWS, LANES))],
        out_specs=(pl.BlockSpec((tm, D_MODEL), row), pl.BlockSpec((tm, PACKED_COLS), row),
                   pl.BlockSpec((tm, ROUTE_LANES), row), pl.BlockSpec((SUBLANES, tm), lambda i: (0, i)),
                   _const_spec((ROUTE_ROWS, LANES))),
        out_shape=(jax.ShapeDtypeStruct((t, D_MODEL), F32), jax.ShapeDtypeStruct((t, PACKED_COLS), jnp.uint32),
                   jax.ShapeDtypeStruct((t, ROUTE_LANES), F32), jax.ShapeDtypeStruct((SUBLANES, t), F32),
                   jax.ShapeDtypeStruct((ROUTE_ROWS, LANES), F32)),
        scratch_shapes=[pltpu.VMEM((ROUTE_ROWS, LANES), F32)],
        compiler_params=_cparams(("arbitrary",)), name="outproj_router",
    )(x2, m_a, m_b, p["w_out"], p["g_ffn"], p["w_route"], p["b_route"], cnt0)


def _dispatch_kernel(dest_ref, h_ref, xs_in_ref, xs_ref, sem, *, tm, n_tok):
    del xs_in_ref
    base = pl.program_id(0) * tm

    def issue(r, _):
        for k in range(TOP_K):
            d = dest_ref[k * n_tok + base + r]
            pltpu.make_async_copy(h_ref.at[pl.ds(r, 1), :], xs_ref.at[pl.ds(d, 1), :], sem).start()
        return 0

    lax.fori_loop(0, tm, issue, 0, unroll=ROW_DMA_UNROLL)
    for k in range(TOP_K):
        pltpu.make_async_copy(h_ref, xs_ref.at[pl.ds(0, tm), :], sem).wait()


def _dispatch(dest_flat, h2, xs, tm):
    t = h2.shape[0]
    return pl.pallas_call(
        functools.partial(_dispatch_kernel, tm=tm, n_tok=t),
        grid_spec=pltpu.PrefetchScalarGridSpec(
            num_scalar_prefetch=1, grid=(t // tm,),
            in_specs=[pl.BlockSpec((tm, PACKED_COLS), lambda i, d: (i, 0)), pl.BlockSpec(memory_space=pl.ANY)],
            out_specs=pl.BlockSpec(memory_space=pl.ANY),
            scratch_shapes=[pltpu.SemaphoreType.DMA]),
        out_shape=jax.ShapeDtypeStruct(xs.shape, xs.dtype),
        input_output_aliases={2: 0},
        compiler_params=_cparams(("arbitrary",)), name="moe_dispatch",
    )(dest_flat, h2, xs)


def _experts_kernel(blk_e_ref, nxt_e_ref, slot_ref, n_used_ref, x_ref, wg_hbm, wu_hbm, wd_hbm, y_ref,
                    wg_f, wu_f, wd_f, wg_s, wu_s, wd_s, sems):
    i = pl.program_id(0)
    e = blk_e_ref[i]
    slot = slot_ref[i]
    first = jnp.logical_or(i == 0, e != blk_e_ref[jnp.maximum(i - 1, 0)])

    def weight_copies(expert, s):
        return (pltpu.make_async_copy(wg_hbm.at[expert], wg_f.at[s], sems.at[s, 0]),
                pltpu.make_async_copy(wu_hbm.at[expert], wu_f.at[s], sems.at[s, 1]),
                pltpu.make_async_copy(wd_hbm.at[expert], wd_f.at[s], sems.at[s, 2]))

    @pl.when(i == 0)
    def _():
        for cp in weight_copies(e, slot):
            cp.start()

    @pl.when(first)
    def _():
        for cp in weight_copies(e, slot):
            cp.wait()
        wg_s[...] = wg_f[slot].astype(BF16)
        wu_s[...] = wu_f[slot].astype(BF16)
        wd_s[...] = wd_f[slot].astype(BF16)
        nxt = nxt_e_ref[i]

        @pl.when(nxt != e)
        def _():
            for cp in weight_copies(nxt, 1 - slot):
                cp.start()

    @pl.when(i < n_used_ref[0])
    def _():
        xb = _unpack_bf16_pairs(x_ref[...]).astype(BF16)
        gate = jnp.dot(xb, wg_s[...], preferred_element_type=F32)
        up = jnp.dot(xb, wu_s[...], preferred_element_type=F32)
        act = (jax.nn.silu(gate) * up).astype(BF16)
        y_ref[...] = _pack_bf16_pairs(jnp.dot(act, wd_s[...], preferred_element_type=F32))

    @pl.when(i >= n_used_ref[0])
    def _():
        y_ref[...] = jnp.zeros_like(y_ref)


def _experts(blk_e, nxt_e, slot, n_used, xs, w_gate, w_up, w_down):
    n_rows = xs.shape[0]
    rb = MOE_ROWS
    row_blk = lambda i, *_: (i, 0)
    return pl.pallas_call(
        _experts_kernel,
        grid_spec=pltpu.PrefetchScalarGridSpec(
            num_scalar_prefetch=4, grid=(n_rows // rb,),
            in_specs=[pl.BlockSpec((rb, PACKED_COLS), row_blk),
                      pl.BlockSpec(memory_space=pl.ANY), pl.BlockSpec(memory_space=pl.ANY),
                      pl.BlockSpec(memory_space=pl.ANY)],
            out_specs=pl.BlockSpec((rb, PACKED_COLS), row_blk),
            scratch_shapes=[pltpu.VMEM((2, D_MODEL, D_EXPERT), F32), pltpu.VMEM((2, D_MODEL, D_EXPERT), F32),
                            pltpu.VMEM((2, D_EXPERT, D_MODEL), F32),
                            pltpu.VMEM((D_MODEL, D_EXPERT), BF16), pltpu.VMEM((D_MODEL, D_EXPERT), BF16),
                            pltpu.VMEM((D_EXPERT, D_MODEL), BF16), pltpu.SemaphoreType.DMA((2, 3))]),
        out_shape=jax.ShapeDtypeStruct((n_rows, PACKED_COLS), jnp.uint32),
        compiler_params=_cparams(("arbitrary",)), name="moe_experts",
    )(blk_e, nxt_e, slot, n_used, xs, w_gate, w_up, w_down)


def _combine_kernel(dest_ref, x1_ref, route_ref, ys_ref, o_ref, gbuf, sems, *, tm, n_tok):
    step = pl.program_id(0)
    n_steps = pl.num_programs(0)

    def gather_rows(for_step, slot):
        base = for_step * tm

        def issue(r, _):
            for k in range(TOP_K):
                d = dest_ref[k * n_tok + base + r]
                pltpu.make_async_copy(ys_ref.at[pl.ds(d, 1), :], gbuf.at[slot, k, pl.ds(r, 1), :],
                                      sems.at[slot]).start()
            return 0

        lax.fori_loop(0, tm, issue, 0, unroll=ROW_DMA_UNROLL)

    @pl.when(step == 0)
    def _():
        gather_rows(0, 0)

    @pl.when(step + 1 < n_steps)
    def _():
        gather_rows(step + 1, (step + 1) % 2)

    slot = step % 2
    for k in range(TOP_K):
        pltpu.make_async_copy(ys_ref.at[pl.ds(0, tm), :], gbuf.at[slot, k], sems.at[slot]).wait()
    w0 = route_ref[:, 2:3]
    w1 = route_ref[:, 3:4]
    o_ref[...] = x1_ref[...] + (_unpack_bf16_pairs(gbuf[slot, 0]) * w0 + _unpack_bf16_pairs(gbuf[slot, 1]) * w1)


def _combine(dest_flat, x1, route, ys, tm):
    t = x1.shape[0]
    return pl.pallas_call(
        functools.partial(_combine_kernel, tm=tm, n_tok=t),
        grid_spec=pltpu.PrefetchScalarGridSpec(
            num_scalar_prefetch=1, grid=(t // tm,),
            in_specs=[pl.BlockSpec((tm, D_MODEL), lambda i, d: (i, 0)),
                      pl.BlockSpec((tm, ROUTE_LANES), lambda i, d: (i, 0)),
                      pl.BlockSpec(memory_space=pl.ANY)],
            out_specs=pl.BlockSpec((tm, D_MODEL), lambda i, d: (i, 0)),
            scratch_shapes=[pltpu.VMEM((2, TOP_K, tm, PACKED_COLS), jnp.uint32), pltpu.SemaphoreType.DMA((2,))]),
        out_shape=jax.ShapeDtypeStruct((t, D_MODEL), F32),
        compiler_params=_cparams(("arbitrary",)), name="moe_combine",
    )(dest_flat, x1, route, ys)


def _rope_tables(seq_len):
    inv = (ROPE_THETA ** (-np.arange(0, AXIS_DIM, 2, dtype=np.float32) / AXIS_DIM)).astype(np.float32)
    pos = np.arange(seq_len)
    ang_r = (pos // GRID_W).astype(np.float32)[:, None] * inv
    ang_c = (pos % GRID_W).astype(np.float32)[:, None] * inv
    cos = np.concatenate([np.cos(ang_r), np.cos(ang_r), np.cos(ang_c), np.cos(ang_c)], axis=1)
    sin = np.concatenate([-np.sin(ang_r), np.sin(ang_r), -np.sin(ang_c), np.sin(ang_c)], axis=1)
    return cos, sin, np.ascontiguousarray(cos.T), np.ascontiguousarray(sin.T)


def _prepare_params(l, max_seq, tm, g_mix, w_in, b_gate, conv_w, conv_b, lru_wa, lru_ba, lru_wx, lru_bx, lru_lam,
                    q_gain, k_gain, w_out, g_ffn, w_rg, b_rg, w_re, b_re):
    c_u, c_gr, c_q, c_k, c_v = (LRU_WIDTH, 2 * LRU_WIDTH, 2 * LRU_WIDTH + D_MODEL,
                                2 * LRU_WIDTH + D_MODEL + ATTN_KV, 2 * LRU_WIDTH + D_MODEL + 2 * ATTN_KV)
    w = w_in[l]
    w_nat = jnp.concatenate([w[:, :c_gr], w[:, c_q:c_k], w[:, c_v:]], axis=1).astype(BF16)
    w_tr = jnp.concatenate([w[:, c_gr:c_q], w[:, c_k:c_v]], axis=1).T.astype(BF16)
    cos, sin, cos_t, sin_t = _rope_tables(max_seq)
    pad = ROUTE_ROWS - N_GROUPS - N_EXPERTS
    w_route = jnp.concatenate([w_rg[l], w_re[l], jnp.zeros((D_MODEL, pad), F32)], axis=1).T
    w_route_hi = w_route.astype(BF16)
    w_route_lo = (w_route - w_route_hi.astype(F32)).astype(BF16)
    w_route = jnp.concatenate([w_route_hi, w_route_lo, w_route_hi], axis=1)
    b_route = jnp.broadcast_to(jnp.concatenate([b_rg[l], b_re[l], jnp.zeros((pad,), F32)])[:, None], (ROUTE_ROWS, tm))
    lru_w = (0.5 * jnp.concatenate([lru_wa[l], lru_wx[l]], axis=-1)).astype(BF16)
    score_bound = (SCORE_BOUND_SLACK * LOG2E * math.sqrt(HEAD_DIM)
                   * jnp.max(jnp.abs(q_gain[l])) * jnp.max(jnp.abs(k_gain[l]))).reshape(1)
    return dict(
        score_bound=score_bound,
        g_mix=g_mix[l][None, :], w_nat=w_nat, w_tr=w_tr, b_gate=b_gate[l][None, :],
        k_gain=k_gain[l][None, :], q_gain_t=jnp.broadcast_to(q_gain[l][:, None], (HEAD_DIM, WIDE_TILE)),
        cos=cos, sin=sin, cos_t=cos_t, sin_t=sin_t,
        conv_w=conv_w[l], conv_b=conv_b[l][None, :], lru_w=lru_w,
        lru_ba=lru_ba[l].reshape(2, 1, LRU_WIDTH), lru_bx=lru_bx[l].reshape(2, 1, LRU_WIDTH),
        lru_lam=lru_lam[l].reshape(2, 1, LRU_WIDTH),
        w_out=w_out[l].astype(BF16), g_ffn=g_ffn[l][None, :], w_route=w_route, b_route=b_route,
    )


def _mixer(x, p, cnt0, tm, tk):
    batch, seq_len, _ = x.shape
    x2 = x.reshape(batch * seq_len, D_MODEL)
    u, ggr, k, gate, qt, vt = _inproj(x2, seq_len, p, WIDE_TILE, tm)
    h_fwd = _lru_scan(u, gate, ggr, None, p, 0, batch, seq_len, WIDE_TILE, reverse=False)
    m_a = _lru_scan(u, gate, ggr, h_fwd, p, 1, batch, seq_len, WIDE_TILE, reverse=True)
    m_b = _attention(p["score_bound"], qt, k, vt, gate, batch, seq_len, tm, min(tk, seq_len),
                     min(KV_BLOCK, seq_len))
    return _outproj_router(x2, m_a, m_b, cnt0, p, tm)


def _layer(xs_in, l, weights):
    (g_mix, w_in, b_gate, conv_w, conv_b, lru_wa, lru_ba, lru_wx, lru_bx, lru_lam,
     q_gain, k_gain, w_out, g_ffn, w_rg, b_rg, w_re, b_re, w_gate, w_up, w_down) = weights
    tm = ROW_TILE
    max_seq = max(x.shape[1] for x in xs_in)
    p = _prepare_params(l, max_seq, tm, g_mix, w_in, b_gate, conv_w, conv_b, lru_wa, lru_ba, lru_wx, lru_bx,
                        lru_lam, q_gain, k_gain, w_out, g_ffn, w_rg, b_rg, w_re, b_re)

    cnt = jnp.zeros((ROUTE_ROWS, LANES), F32)
    x1s, h2s, routes, routes_t = [], [], [], []
    for x in xs_in:
        x1, h2, route, route_t, cnt = _mixer(x, p, cnt, tm, KV_CHUNK)
        x1s.append(x1)
        h2s.append(h2)
        routes.append(route)
        routes_t.append(route_t)

    n_tok = sum(x1.shape[0] for x1 in x1s)
    n_rows = n_tok * TOP_K + N_EXPERTS * MOE_ROWS
    n_blk = n_rows // MOE_ROWS
    counts = cnt[EXPERT_ROW0:EXPERT_ROW0 + N_EXPERTS, 0].astype(jnp.int32)
    padded = (counts + MOE_ROWS - 1) // MOE_ROWS * MOE_ROWS
    pend = jnp.cumsum(padded)
    pstart = pend - padded
    blk_idx = jnp.arange(n_blk, dtype=jnp.int32)
    n_used = (pend[-1:] // MOE_ROWS).astype(jnp.int32)
    blk_e = jnp.minimum(jnp.sum((pend[None, :] <= (blk_idx * MOE_ROWS)[:, None]).astype(jnp.int32), axis=1),
                        N_EXPERTS - 1)
    blk_e = jnp.where(blk_idx < n_used, blk_e, blk_e[jnp.maximum(n_used[0] - 1, 0)])
    starts = jnp.concatenate([jnp.ones((1,), jnp.int32), (blk_e[1:] != blk_e[:-1]).astype(jnp.int32)])
    slot = (jnp.cumsum(starts) - 1) % 2
    later_start = lax.cummin(jnp.where(starts == 1, blk_idx, n_blk), reverse=True)
    next_start = jnp.concatenate([later_start[1:], jnp.full((1,), n_blk, jnp.int32)])
    nxt_e = jnp.where(next_start < n_blk, blk_e[jnp.minimum(next_start, n_blk - 1)], blk_e)

    dests = []
    xs = jnp.zeros((n_rows, PACKED_COLS), jnp.uint32)
    for h2, route_t in zip(h2s, routes_t):
        eid = route_t[0:TOP_K].astype(jnp.int32)
        rank = route_t[4:4 + TOP_K].astype(jnp.int32)
        experts = jnp.arange(N_EXPERTS, dtype=jnp.int32)[:, None, None]
        dest = (rank + jnp.sum(jnp.where(eid[None] == experts, pstart[:, None, None], 0), axis=0)).reshape(-1)
        dests.append(dest)
        xs = _dispatch(dest, h2, xs, WIDE_TILE)
    ys = _experts(blk_e, nxt_e, slot.astype(jnp.int32), n_used, xs, w_gate[l], w_up[l], w_down[l])
    outs = []
    for x, x1, route, dest in zip(xs_in, x1s, routes, dests):
        outs.append(_combine(dest, x1, route, ys, WIDE_TILE).reshape(x.shape))
    return outs


def kernel(x_prompt, x_sample, g_mix, w_in, b_gate, conv_w, conv_b, lru_wa, lru_ba, lru_wx, lru_bx, lru_lam,
           q_gain, k_gain, w_out, g_ffn, w_rg, b_rg, w_re, b_re, w_gate, w_up, w_down):
    weights = (g_mix, w_in, b_gate, conv_w, conv_b, lru_wa, lru_ba, lru_wx, lru_bx, lru_lam,
               q_gain, k_gain, w_out, g_ffn, w_rg, b_rg, w_re, b_re, w_gate, w_up, w_down)
    xs = [x_prompt, x_sample]
    for l in range(g_mix.shape[0]):
        xs = _layer(xs, l, weights)
    return tuple(xs)
```

```python
import functools
import math

import jax
import jax.numpy as jnp
import numpy as np
from jax import lax
from jax.experimental import pallas as pl
from jax.experimental.pallas import tpu as pltpu

F32 = jnp.float32
BF16 = jnp.bfloat16

D_MODEL = 1024
N_HEADS = 8
N_KV_HEADS = 2
GROUP = N_HEADS // N_KV_HEADS
HEAD_DIM = D_MODEL // N_HEADS
AXIS_DIM = HEAD_DIM // 2
N_FREQ = AXIS_DIM // 2
GRID_W = 64
ROPE_THETA = 10000.0
LRU_WIDTH = D_MODEL
LRU_BLOCKS = 8
LRU_BLOCK_W = LRU_WIDTH // LRU_BLOCKS
CONV_W = 4
CONV_PAD_LEFT = 2
LRU_C = 8.0
N_GROUPS = 4
EXPERTS_PER_GROUP = 8
N_EXPERTS = N_GROUPS * EXPERTS_PER_GROUP
TOP_K = 2
D_EXPERT = D_MODEL // 2
ATTN_KV = N_KV_HEADS * HEAD_DIM
EPS = 1e-6

LANES = 128
SUBLANES = 8
VMEM_LIMIT_BYTES = 56 * 1024 * 1024

ROW_TILE = 256
WIDE_TILE = 512
KV_CHUNK = 2048
KV_BLOCK = 16384
MOE_ROWS = 512
ROW_DMA_UNROLL = 8
ROUTE_LANES = LANES
ROUTE_ROWS = 64
PACKED_COLS = D_MODEL // 2
EXPERT_ROW0 = N_GROUPS

NAT_COLS = 2 * LRU_WIDTH + ATTN_KV + 2 * D_MODEL
TR_ROWS = D_MODEL + ATTN_KV
LOG2E = math.log2(math.e)
SAFE_EXP2_RANGE = 100.0
SCORE_BOUND_SLACK = 1.01


def _cparams(semantics):
    return pltpu.CompilerParams(dimension_semantics=semantics, vmem_limit_bytes=VMEM_LIMIT_BYTES)


def _const_spec(shape, single_buffer=False):
    nd = len(shape)
    mode = pl.Buffered(1) if single_buffer else None
    return pl.BlockSpec(shape, lambda *_: (0,) * nd, pipeline_mode=mode)


def _sigmoid(x):
    return 0.5 * jnp.tanh(0.5 * x) + 0.5


def _pack_bf16_pairs(x):
    n = x.shape[1] // 2
    lo = lax.bitcast_convert_type(x[:, :n].astype(BF16).astype(F32), jnp.uint32)
    hi = lax.bitcast_convert_type(x[:, n:].astype(BF16).astype(F32), jnp.uint32)
    return (hi & jnp.uint32(0xFFFF0000)) | (lo >> 16)


def _unpack_bf16_pairs(w):
    lo = lax.bitcast_convert_type(w << 16, F32)
    hi = lax.bitcast_convert_type(w & jnp.uint32(0xFFFF0000), F32)
    return jnp.concatenate([lo, hi], axis=1)


def _swap_halves_rows(x):
    return jnp.concatenate([x[N_FREQ:AXIS_DIM], x[0:N_FREQ], x[AXIS_DIM + N_FREQ:], x[AXIS_DIM:AXIS_DIM + N_FREQ]], axis=0)


def _inproj_kernel(x_ref, gmix_ref, wnat_ref, wtr_ref, bgate_ref, kgain_ref, qgain_ref,
                   cos_ref, sin_ref, cost_ref, sint_ref,
                   u_ref, ggr_ref, k_ref, gate_ref, qt_ref, vt_ref):
    tm = x_ref.shape[0]
    x = x_ref[...]
    h = (x * lax.rsqrt(jnp.mean(x * x, axis=-1, keepdims=True) + EPS) * gmix_ref[...]).astype(BF16)

    def nat(lo, hi):
        return jnp.dot(h, wnat_ref[:, lo:hi], preferred_element_type=F32)

    u_ref[...] = nat(0, LRU_WIDTH)
    ggr_ref[...] = jax.nn.gelu(nat(LRU_WIDTH, 2 * LRU_WIDTH))
    k0 = 2 * LRU_WIDTH
    kraw = nat(k0, k0 + ATTN_KV)
    g0 = k0 + ATTN_KV
    gate_ref[...] = _sigmoid(nat(g0, g0 + 2 * D_MODEL) + bgate_ref[...])

    cos = cos_ref[...]
    sin = sin_ref[...]
    lane = lax.broadcasted_iota(jnp.int32, (tm, HEAD_DIM), 1)
    first_half = (lane % AXIS_DIM) < N_FREQ
    for j in range(N_KV_HEADS):
        kj = kraw[:, j * HEAD_DIM:(j + 1) * HEAD_DIM]
        kn = kj * lax.rsqrt(jnp.mean(kj * kj, axis=-1, keepdims=True) + EPS) * kgain_ref[...]
        partner = jnp.where(first_half, pltpu.roll(kn, HEAD_DIM - N_FREQ, 1), pltpu.roll(kn, N_FREQ, 1))
        k_ref[:, j * HEAD_DIM:(j + 1) * HEAD_DIM] = (kn * cos + partner * sin).astype(BF16)

    zt = lax.dot_general(wtr_ref[...], h, (((1,), (1,)), ((), ())), preferred_element_type=F32)
    cost = cost_ref[...]
    sint = sint_ref[...]
    qgain = qgain_ref[...]
    qscale = (HEAD_DIM ** -0.5) * LOG2E
    for hd in range(N_HEADS):
        xq = zt[hd * HEAD_DIM:(hd + 1) * HEAD_DIM, :]
        xn = xq * lax.rsqrt(jnp.mean(xq * xq, axis=0, keepdims=True) + EPS) * qgain
        rot = (xn * cost + _swap_halves_rows(xn) * sint) * qscale
        g = hd % GROUP
        tq = qt_ref.shape[3] // GROUP
        for qb in range(tm // tq):
            qt_ref[qb, hd // GROUP, :, g * tq:(g + 1) * tq] = rot[:, qb * tq:(qb + 1) * tq].astype(BF16)
    for j in range(N_KV_HEADS):
        r0 = D_MODEL + j * HEAD_DIM
        vt_ref[j] = zt[r0:r0 + HEAD_DIM, :].astype(BF16)


def _inproj(x2, seq_len, p, tm, tq):
    t = x2.shape[0]
    n_pos = seq_len // tm
    grid = (t // tm,)
    row = lambda i: (i, 0)
    in_specs = [
        pl.BlockSpec((tm, D_MODEL), row),
        _const_spec((1, D_MODEL)),
        _const_spec((D_MODEL, NAT_COLS), single_buffer=True),
        _const_spec((TR_ROWS, D_MODEL), single_buffer=True),
        _const_spec((1, 2 * D_MODEL)),
        _const_spec((1, HEAD_DIM)),
        _const_spec((HEAD_DIM, tm)),
        pl.BlockSpec((tm, HEAD_DIM), lambda i: (i % n_pos, 0)),
        pl.BlockSpec((tm, HEAD_DIM), lambda i: (i % n_pos, 0)),
        pl.BlockSpec((HEAD_DIM, tm), lambda i: (0, i % n_pos)),
        pl.BlockSpec((HEAD_DIM, tm), lambda i: (0, i % n_pos)),
    ]
    out_shape = (
        jax.ShapeDtypeStruct((t, LRU_WIDTH), F32),
        jax.ShapeDtypeStruct((t, LRU_WIDTH), F32),
        jax.ShapeDtypeStruct((t, ATTN_KV), BF16),
        jax.ShapeDtypeStruct((t, 2 * D_MODEL), F32),
        jax.ShapeDtypeStruct((t // tq, N_KV_HEADS, HEAD_DIM, GROUP * tq), BF16),
        jax.ShapeDtypeStruct((N_KV_HEADS, HEAD_DIM, t), BF16),
    )
    out_specs = (
        pl.BlockSpec((tm, LRU_WIDTH), row),
        pl.BlockSpec((tm, LRU_WIDTH), row),
        pl.BlockSpec((tm, ATTN_KV), row),
        pl.BlockSpec((tm, 2 * D_MODEL), row),
        pl.BlockSpec((tm // tq, N_KV_HEADS, HEAD_DIM, GROUP * tq), lambda i: (i, 0, 0, 0)),
        pl.BlockSpec((N_KV_HEADS, HEAD_DIM, tm), lambda i: (0, 0, i)),
    )
    return pl.pallas_call(
        _inproj_kernel, grid=grid, in_specs=in_specs, out_specs=out_specs, out_shape=out_shape,
        compiler_params=_cparams(("arbitrary",)), name="inproj",
    )(x2, p["g_mix"], p["w_nat"], p["w_tr"], p["b_gate"], p["k_gain"], p["q_gain_t"],
      p["cos"], p["sin"], p["cos_t"], p["sin_t"])


def _lru_kernel(*refs, reverse, tm, nt):
    if reverse:
        (up_ref, uc_ref, un_ref, cw_ref, cb_ref, w_ref, ba_ref, bx_ref, lam_ref, hf_ref, ggr_ref, ga_ref,
         out_ref, ubuf, a_s, g_s, carry) = refs
    else:
        (up_ref, uc_ref, un_ref, cw_ref, cb_ref, w_ref, ba_ref, bx_ref, lam_ref,
         out_ref, ubuf, a_s, g_s, carry) = refs
    i = pl.program_id(1)
    ti = (nt - 1 - i) if reverse else i

    @pl.when(i == 0)
    def _():
        carry[...] = jnp.zeros_like(carry)

    ubuf[0:SUBLANES] = jnp.where(ti == 0, 0.0, up_ref[...])
    ubuf[SUBLANES:SUBLANES + tm] = uc_ref[...]
    ubuf[SUBLANES + tm:2 * SUBLANES + tm] = jnp.where(ti == nt - 1, 0.0, un_ref[...])
    cw = cw_ref[...]
    ub = ubuf[...]
    n_buf = tm + 2 * SUBLANES
    xc = cb_ref[...]
    for j in range(CONV_W):
        back = CONV_PAD_LEFT - j
        tap = ub if back == 0 else pltpu.roll(ub, back % n_buf, 0)
        xc = xc + cw[j:j + 1] * tap[SUBLANES:SUBLANES + tm]

    xcb = xc.astype(BF16)
    lam = lam_ref[...]
    half_c = (0.5 * LRU_C) * (jnp.minimum(lam, 0.0) - jnp.log1p(jnp.exp(-jnp.abs(lam))))
    half_ba = 0.5 * ba_ref[...]
    half_bx = 0.5 * bx_ref[...]
    half_xc = 0.5 * xc
    for hb in range(LRU_BLOCKS):
        sl = slice(hb * LRU_BLOCK_W, (hb + 1) * LRU_BLOCK_W)
        gz = jnp.dot(xcb[:, sl], w_ref[hb], preferred_element_type=F32)
        tr = jnp.tanh(gz[:, :LRU_BLOCK_W] + half_ba[:, sl])
        ti = jnp.tanh(gz[:, LRU_BLOCK_W:] + half_bx[:, sl])
        log_a = tr * half_c[:, sl] + half_c[:, sl]
        a = jnp.exp(log_a)
        a_s[:, sl] = a
        y = jnp.tanh(log_a) * (-1.0 - a * a)
        root = jnp.where(y > 0.0, y * lax.rsqrt(y), 0.0)
        g_s[:, sl] = root * (ti * half_xc[:, sl] + half_xc[:, sl])

    n_chunk = tm // SUBLANES
    srow = lax.broadcasted_iota(jnp.int32, (SUBLANES, LRU_WIDTH), 0)

    def chunk(c, h_prev):
        ci = (n_chunk - 1 - c) if reverse else c
        off = pl.multiple_of(ci * SUBLANES, SUBLANES)
        a = a_s[pl.ds(off, SUBLANES), :]
        b = g_s[pl.ds(off, SUBLANES), :]
        edge = (SUBLANES - 1) if reverse else 0
        b = b + jnp.where(srow == edge, a * h_prev, 0.0)
        for d in (1, 2, 4):
            shift = (SUBLANES - d) if reverse else d
            valid = (srow < SUBLANES - d) if reverse else (srow >= d)
            b = a * jnp.where(valid, pltpu.roll(b, shift, 0), 0.0) + b
            if d != 4:
                a = a * pltpu.roll(a, shift, 0)
        h = b
        if reverse:
            rows = pl.ds(off, SUBLANES)
            out_ref[rows, :] = ga_ref[rows, :] * (ggr_ref[rows, :] * (hf_ref[rows, :] + h))
            return h[0:1]
        out_ref[pl.ds(off, SUBLANES), :] = h
        return h[SUBLANES - 1:SUBLANES]

    carry[...] = lax.fori_loop(0, n_chunk, chunk, carry[...])


def _lru_scan(u, gate, ggr, h_fwd, p, d, batch, seq_len, tm, reverse):
    t = u.shape[0]
    nt = seq_len // tm
    per8 = tm // SUBLANES
    n8 = t // SUBLANES

    def tile(b, i):
        return b * nt + ((nt - 1 - i) if reverse else i)

    cur = lambda b, i: (tile(b, i), 0)
    prev = lambda b, i: (jnp.maximum(tile(b, i) * per8 - 1, 0), 0)
    nxt = lambda b, i: (jnp.minimum((tile(b, i) + 1) * per8, n8 - 1), 0)
    const2 = lambda b, i: (0, 0)
    in_specs = [
        pl.BlockSpec((SUBLANES, LRU_WIDTH), prev),
        pl.BlockSpec((tm, LRU_WIDTH), cur),
        pl.BlockSpec((SUBLANES, LRU_WIDTH), nxt),
        pl.BlockSpec((CONV_W, LRU_WIDTH), const2),
        pl.BlockSpec((1, LRU_WIDTH), const2),
        pl.BlockSpec((LRU_BLOCKS, LRU_BLOCK_W, 2 * LRU_BLOCK_W), lambda b, i: (0, 0, 0)),
        pl.BlockSpec((1, LRU_WIDTH), const2),
        pl.BlockSpec((1, LRU_WIDTH), const2),
        pl.BlockSpec((1, LRU_WIDTH), const2),
    ]
    args = [u, u, u, p["conv_w"], p["conv_b"], p["lru_w"][d], p["lru_ba"][d], p["lru_bx"][d], p["lru_lam"][d]]
    if reverse:
        in_specs += [pl.BlockSpec((tm, LRU_WIDTH), cur)] * 3
        args += [h_fwd, ggr, gate]
    return pl.pallas_call(
        functools.partial(_lru_kernel, reverse=reverse, tm=tm, nt=nt),
        grid=(batch, nt), in_specs=in_specs,
        out_specs=pl.BlockSpec((tm, LRU_WIDTH), cur),
        out_shape=jax.ShapeDtypeStruct((t, LRU_WIDTH), F32),
        scratch_shapes=[pltpu.VMEM((tm + 2 * SUBLANES, LRU_WIDTH), F32), pltpu.VMEM((tm, LRU_WIDTH), F32),
                        pltpu.VMEM((tm, LRU_WIDTH), F32), pltpu.VMEM((1, LRU_WIDTH), F32)],
        compiler_params=_cparams(("arbitrary", "arbitrary")), name="lru_bwd" if reverse else "lru_fwd",
    )(*args)


def _attn_kernel(bound_ref, qt_ref, k_ref, vt_ref, gb_ref, o_ref, m_s, l_s, acc_s, *, tq, tk, n_chunks, n_kv_blocks):
    nq_cols = GROUP * tq
    kv_blk = pl.program_id(3)
    bound = bound_ref[0]
    bounded = 2.0 * bound <= SAFE_EXP2_RANGE

    @pl.when(kv_blk == 0)
    def _():
        acc_s[...] = jnp.zeros_like(acc_s)
        l_s[...] = jnp.zeros_like(l_s)
        m_s[...] = jnp.full_like(m_s, -jnp.inf)

    @pl.when(bounded)
    def _():
        qt = qt_ref[0, 0]
        for c in range(n_chunks):
            rows = slice(c * tk, (c + 1) * tk)
            pr = jnp.exp2(jnp.dot(k_ref[rows, :], qt, preferred_element_type=F32) - bound)
            l_s[...] += jnp.sum(pr.reshape(tk // SUBLANES, SUBLANES, nq_cols), axis=0)
            acc_s[...] += jnp.dot(vt_ref[0, :, rows], pr.astype(BF16), preferred_element_type=F32)

    @pl.when(jnp.logical_not(bounded))
    def _():
        qt = qt_ref[0, 0]

        def chunk(c, _):
            off = pl.multiple_of(c * tk, tk)
            s = jnp.dot(k_ref[pl.ds(off, tk), :], qt, preferred_element_type=F32)
            m_old = m_s[...]
            m_new = jnp.maximum(m_old, jnp.max(s, axis=0, keepdims=True))
            alpha = jnp.exp2(m_old - m_new)
            pr = jnp.exp2(s - m_new)
            l_s[...] = alpha * l_s[...] + jnp.sum(pr.reshape(tk // SUBLANES, SUBLANES, nq_cols), axis=0)
            acc_s[...] = alpha * acc_s[...] + jnp.dot(vt_ref[0, :, pl.ds(off, tk)], pr.astype(BF16),
                                                      preferred_element_type=F32)
            m_s[...] = m_new
            return 0

        lax.fori_loop(0, n_chunks, chunk, 0)

    @pl.when(kv_blk == n_kv_blocks - 1)
    def _():
        out_t = acc_s[...] / jnp.sum(l_s[...], axis=0, keepdims=True)
        for g in range(GROUP):
            cols = slice(g * HEAD_DIM, (g + 1) * HEAD_DIM)
            o_ref[:, cols] = gb_ref[:, cols] * out_t[:, g * tq:(g + 1) * tq].T


def _attention(bound, qt, k, vt, gate, batch, seq_len, tq, tk, kv_block):
    t = k.shape[0]
    nq = seq_len // tq
    nkv = seq_len // kv_block
    half = GROUP * HEAD_DIM
    gate_col0 = D_MODEL // half
    return pl.pallas_call(
        functools.partial(_attn_kernel, tq=tq, tk=tk, n_chunks=kv_block // tk, n_kv_blocks=nkv),
        grid=(batch, N_KV_HEADS, nq, nkv),
        in_specs=[
            pl.BlockSpec(memory_space=pltpu.SMEM),
            pl.BlockSpec((1, 1, HEAD_DIM, GROUP * tq), lambda b, j, i, c: (b * nq + i, j, 0, 0)),
            pl.BlockSpec((kv_block, HEAD_DIM), lambda b, j, i, c: (b * nkv + c, j)),
            pl.BlockSpec((1, HEAD_DIM, kv_block), lambda b, j, i, c: (j, 0, b * nkv + c)),
            pl.BlockSpec((tq, half), lambda b, j, i, c: (b * nq + i, gate_col0 + j)),
        ],
        out_specs=pl.BlockSpec((tq, half), lambda b, j, i, c: (b * nq + i, j)),
        out_shape=jax.ShapeDtypeStruct((t, D_MODEL), F32),
        scratch_shapes=[pltpu.VMEM((1, GROUP * tq), F32), pltpu.VMEM((SUBLANES, GROUP * tq), F32),
                        pltpu.VMEM((HEAD_DIM, GROUP * tq), F32)],
        compiler_params=_cparams(("arbitrary", "arbitrary", "arbitrary", "arbitrary")), name="attention",
    )(bound, qt, k, vt, gate)


def _outproj_router_kernel(x_ref, ma_ref, mb_ref, wout_ref, gffn_ref, wr_ref, br_ref, cnt0_ref,
                           x1_ref, h2_ref, route_ref, route_t_ref, cnt_ref, cnt_s):
    tm = x_ref.shape[0]
    i = pl.program_id(0)

    @pl.when(i == 0)
    def _():
        cnt_s[...] = cnt0_ref[...]

    merged = (ma_ref[...] + mb_ref[...]).astype(BF16)
    x1 = x_ref[...] + jnp.dot(merged, wout_ref[...], preferred_element_type=F32)
    x1_ref[...] = x1
    h2 = x1 * lax.rsqrt(jnp.mean(x1 * x1, axis=-1, keepdims=True) + EPS) * gffn_ref[...]
    h2_ref[...] = _pack_bf16_pairs(h2)

    hi = h2.astype(BF16)
    lo = (h2 - hi.astype(F32)).astype(BF16)
    zt = lax.dot_general(wr_ref[...], jnp.concatenate([hi, hi, lo], axis=1), (((1,), (1,)), ((), ())),
                         preferred_element_type=F32) + br_ref[...]
    sub = lax.broadcasted_iota(jnp.int32, (ROUTE_ROWS, tm), 0)
    neg = -jnp.inf

    def first_argmax(v):
        m = jnp.max(v, axis=0, keepdims=True)
        return m, jnp.min(jnp.where(v == m, sub, ROUTE_ROWS), axis=0, keepdims=True)

    zg = jnp.where(sub < N_GROUPS, zt, neg)
    mg, grp = first_argmax(zg)
    p_sel = 1.0 / jnp.sum(jnp.exp(zg - mg), axis=0, keepdims=True)
    e_lo = EXPERT_ROW0 + EXPERTS_PER_GROUP * grp
    ze = jnp.where((sub >= e_lo) & (sub < e_lo + EXPERTS_PER_GROUP), zt, neg)
    m1, i1 = first_argmax(ze)
    m2, i2 = first_argmax(jnp.where(sub == i1, neg, ze))
    e2 = jnp.exp(m2 - m1)
    w0 = p_sel / (1.0 + e2)
    w1 = p_sel * e2 / (1.0 + e2)

    hot0 = sub == i1
    hot1 = sub == i2
    hot = jnp.where(hot0 | hot1, 1.0, 0.0)
    rr = lax.broadcasted_iota(jnp.int32, (tm, tm), 0)
    cc = lax.broadcasted_iota(jnp.int32, (tm, tm), 1)
    earlier = jnp.where(rr < cc, 1.0, 0.0).astype(BF16)
    cnt = cnt_s[...]
    before = (jnp.dot(hot.astype(BF16), earlier, preferred_element_type=F32)
              + jnp.concatenate([cnt] * (tm // LANES), axis=1))
    rank0 = jnp.sum(jnp.where(hot0, before, 0.0), axis=0, keepdims=True)
    rank1 = jnp.sum(jnp.where(hot1, before, 0.0), axis=0, keepdims=True)
    cnt_s[...] = cnt + jnp.sum(hot, axis=1, keepdims=True)
    cnt_ref[...] = cnt_s[...]

    eid0 = (i1 - EXPERT_ROW0).astype(F32)
    eid1 = (i2 - EXPERT_ROW0).astype(F32)
    zero = jnp.zeros_like(w0)
    rec_t = jnp.concatenate([eid0, eid1, w0, w1, rank0, rank1, zero, zero], axis=0)
    route_t_ref[...] = rec_t
    route_ref[...] = jnp.concatenate([rec_t, jnp.zeros((ROUTE_LANES - SUBLANES, tm), F32)], axis=0).T


def _outproj_router(x2, m_a, m_b, cnt0, p, tm):
    t = x2.shape[0]
    row = lambda i: (i, 0)
    return pl.pallas_call(
        _outproj_router_kernel, grid=(t // tm,),
        in_specs=[pl.BlockSpec((tm, D_MODEL), row), pl.BlockSpec((tm, D_MODEL), row), pl.BlockSpec((tm, D_MODEL), row),
                  _const_spec((D_MODEL, D_MODEL)), _const_spec((1, D_MODEL)), _const_spec((ROUTE_ROWS, 3 * D_MODEL)),
                  _const_spec((ROUTE_ROWS, tm)), _const_spec((ROUTE_ROWS, LANES))],
        out_specs=(pl.BlockSpec((tm, D_MODEL), row), pl.BlockSpec((tm, PACKED_COLS), row),
                   pl.BlockSpec((tm, ROUTE_LANES), row), pl.BlockSpec((SUBLANES, tm), lambda i: (0, i)),
                   _const_spec((ROUTE_ROWS, LANES))),
        out_shape=(jax.ShapeDtypeStruct((t, D_MODEL), F32), jax.ShapeDtypeStruct((t, PACKED_COLS), jnp.uint32),
                   jax.ShapeDtypeStruct((t, ROUTE_LANES), F32), jax.ShapeDtypeStruct((SUBLANES, t), F32),
                   jax.ShapeDtypeStruct((ROUTE_ROWS, LANES), F32)),
        scratch_shapes=[pltpu.VMEM((ROUTE_ROWS, LANES), F32)],
        compiler_params=_cparams(("arbitrary",)), name="outproj_router",
    )(x2, m_a, m_b, p["w_out"], p["g_ffn"], p["w_route"], p["b_route"], cnt0)


def _dispatch_kernel(dest_ref, h_ref, xs_in_ref, xs_ref, sem, *, tm, n_tok):
    del xs_in_ref
    base = pl.program_id(0) * tm

    def issue(r, _):
        for k in range(TOP_K):
            d = dest_ref[k * n_tok + base + r]
            pltpu.make_async_copy(h_ref.at[pl.ds(r, 1), :], xs_ref.at[pl.ds(d, 1), :], sem).start()
        return 0

    lax.fori_loop(0, tm, issue, 0, unroll=ROW_DMA_UNROLL)
    for k in range(TOP_K):
        pltpu.make_async_copy(h_ref, xs_ref.at[pl.ds(0, tm), :], sem).wait()


def _dispatch(dest_flat, h2, xs, tm):
    t = h2.shape[0]
    return pl.pallas_call(
        functools.partial(_dispatch_kernel, tm=tm, n_tok=t),
        grid_spec=pltpu.PrefetchScalarGridSpec(
            num_scalar_prefetch=1, grid=(t // tm,),
            in_specs=[pl.BlockSpec((tm, PACKED_COLS), lambda i, d: (i, 0)), pl.BlockSpec(memory_space=pl.ANY)],
            out_specs=pl.BlockSpec(memory_space=pl.ANY),
            scratch_shapes=[pltpu.SemaphoreType.DMA]),
        out_shape=jax.ShapeDtypeStruct(xs.shape, xs.dtype),
        input_output_aliases={2: 0},
        compiler_params=_cparams(("arbitrary",)), name="moe_dispatch",
    )(dest_flat, h2, xs)


def _experts_kernel(blk_e_ref, nxt_e_ref, slot_ref, n_used_ref, x_ref, wg_hbm, wu_hbm, wd_hbm, y_ref,
                    wg_f, wu_f, wd_f, wg_s, wu_s, wd_s, sems):
    i = pl.program_id(0)
    e = blk_e_ref[i]
    slot = slot_ref[i]
    first = jnp.logical_or(i == 0, e != blk_e_ref[jnp.maximum(i - 1, 0)])

    def weight_copies(expert, s):
        return (pltpu.make_async_copy(wg_hbm.at[expert], wg_f.at[s], sems.at[s, 0]),
                pltpu.make_async_copy(wu_hbm.at[expert], wu_f.at[s], sems.at[s, 1]),
                pltpu.make_async_copy(wd_hbm.at[expert], wd_f.at[s], sems.at[s, 2]))

    @pl.when(i == 0)
    def _():
        for cp in weight_copies(e, slot):
            cp.start()

    @pl.when(first)
    def _():
        for cp in weight_copies(e, slot):
            cp.wait()
        wg_s[...] = wg_f[slot].astype(BF16)
        wu_s[...] = wu_f[slot].astype(BF16)
        wd_s[...] = wd_f[slot].astype(BF16)
        nxt = nxt_e_ref[i]

        @pl.when(nxt != e)
        def _():
            for cp in weight_copies(nxt, 1 - slot):
                cp.start()

    @pl.when(i < n_used_ref[0])
    def _():
        xb = _unpack_bf16_pairs(x_ref[...]).astype(BF16)
        gate = jnp.dot(xb, wg_s[...], preferred_element_type=F32)
        up = jnp.dot(xb, wu_s[...], preferred_element_type=F32)
        act = (jax.nn.silu(gate) * up).astype(BF16)
        y_ref[...] = _pack_bf16_pairs(jnp.dot(act, wd_s[...], preferred_element_type=F32))

    @pl.when(i >= n_used_ref[0])
    def _():
        y_ref[...] = jnp.zeros_like(y_ref)


def _experts(blk_e, nxt_e, slot, n_used, xs, w_gate, w_up, w_down):
    n_rows = xs.shape[0]
    rb = MOE_ROWS
    row_blk = lambda i, *_: (i, 0)
    return pl.pallas_call(
        _experts_kernel,
        grid_spec=pltpu.PrefetchScalarGridSpec(
            num_scalar_prefetch=4, grid=(n_rows // rb,),
            in_specs=[pl.BlockSpec((rb, PACKED_COLS), row_blk),
                      pl.BlockSpec(memory_space=pl.ANY), pl.BlockSpec(memory_space=pl.ANY),
                      pl.BlockSpec(memory_space=pl.ANY)],
            out_specs=pl.BlockSpec((rb, PACKED_COLS), row_blk),
            scratch_shapes=[pltpu.VMEM((2, D_MODEL, D_EXPERT), F32), pltpu.VMEM((2, D_MODEL, D_EXPERT), F32),
                            pltpu.VMEM((2, D_EXPERT, D_MODEL), F32),
                            pltpu.VMEM((D_MODEL, D_EXPERT), BF16), pltpu.VMEM((D_MODEL, D_EXPERT), BF16),
                            pltpu.VMEM((D_EXPERT, D_MODEL), BF16), pltpu.SemaphoreType.DMA((2, 3))]),
        out_shape=jax.ShapeDtypeStruct((n_rows, PACKED_COLS), jnp.uint32),
        compiler_params=_cparams(("arbitrary",)), name="moe_experts",
    )(blk_e, nxt_e, slot, n_used, xs, w_gate, w_up, w_down)


def _combine_kernel(dest_ref, x1_ref, route_ref, ys_ref, o_ref, gbuf, sems, *, tm, n_tok):
    step = pl.program_id(0)
    n_steps = pl.num_programs(0)

    def gather_rows(for_step, slot):
        base = for_step * tm

        def issue(r, _):
            for k in range(TOP_K):
                d = dest_ref[k * n_tok + base + r]
                pltpu.make_async_copy(ys_ref.at[pl.ds(d, 1), :], gbuf.at[slot, k, pl.ds(r, 1), :],
                                      sems.at[slot]).start()
            return 0

        lax.fori_loop(0, tm, issue, 0, unroll=ROW_DMA_UNROLL)

    @pl.when(step == 0)
    def _():
        gather_rows(0, 0)

    @pl.when(step + 1 < n_steps)
    def _():
        gather_rows(step + 1, (step + 1) % 2)

    slot = step % 2
    for k in range(TOP_K):
        pltpu.make_async_copy(ys_ref.at[pl.ds(0, tm), :], gbuf.at[slot, k], sems.at[slot]).wait()
    w0 = route_ref[:, 2:3]
    w1 = route_ref[:, 3:4]
    o_ref[...] = x1_ref[...] + (_unpack_bf16_pairs(gbuf[slot, 0]) * w0 + _unpack_bf16_pairs(gbuf[slot, 1]) * w1)


def _combine(dest_flat, x1, route, ys, tm):
    t = x1.shape[0]
    return pl.pallas_call(
        functools.partial(_combine_kernel, tm=tm, n_tok=t),
        grid_spec=pltpu.PrefetchScalarGridSpec(
            num_scalar_prefetch=1, grid=(t // tm,),
            in_specs=[pl.BlockSpec((tm, D_MODEL), lambda i, d: (i, 0)),
                      pl.BlockSpec((tm, ROUTE_LANES), lambda i, d: (i, 0)),
                      pl.BlockSpec(memory_space=pl.ANY)],
            out_specs=pl.BlockSpec((tm, D_MODEL), lambda i, d: (i, 0)),
            scratch_shapes=[pltpu.VMEM((2, TOP_K, tm, PACKED_COLS), jnp.uint32), pltpu.SemaphoreType.DMA((2,))]),
        out_shape=jax.ShapeDtypeStruct((t, D_MODEL), F32),
        compiler_params=_cparams(("arbitrary",)), name="moe_combine",
    )(dest_flat, x1, route, ys)


def _rope_tables(seq_len):
    inv = (ROPE_THETA ** (-np.arange(0, AXIS_DIM, 2, dtype=np.float32) / AXIS_DIM)).astype(np.float32)
    pos = np.arange(seq_len)
    ang_r = (pos // GRID_W).astype(np.float32)[:, None] * inv
    ang_c = (pos % GRID_W).astype(np.float32)[:, None] * inv
    cos = np.concatenate([np.cos(ang_r), np.cos(ang_r), np.cos(ang_c), np.cos(ang_c)], axis=1)
    sin = np.concatenate([-np.sin(ang_r), np.sin(ang_r), -np.sin(ang_c), np.sin(ang_c)], axis=1)
    return cos, sin, np.ascontiguousarray(cos.T), np.ascontiguousarray(sin.T)


def _prepare_params(l, max_seq, tm, g_mix, w_in, b_gate, conv_w, conv_b, lru_wa, lru_ba, lru_wx, lru_bx, lru_lam,
                    q_gain, k_gain, w_out, g_ffn, w_rg, b_rg, w_re, b_re):
    c_u, c_gr, c_q, c_k, c_v = (LRU_WIDTH, 2 * LRU_WIDTH, 2 * LRU_WIDTH + D_MODEL,
                                2 * LRU_WIDTH + D_MODEL + ATTN_KV, 2 * LRU_WIDTH + D_MODEL + 2 * ATTN_KV)
    w = w_in[l]
    w_nat = jnp.concatenate([w[:, :c_gr], w[:, c_q:c_k], w[:, c_v:]], axis=1).astype(BF16)
    w_tr = jnp.concatenate([w[:, c_gr:c_q], w[:, c_k:c_v]], axis=1).T.astype(BF16)
    cos, sin, cos_t, sin_t = _rope_tables(max_seq)
    pad = ROUTE_ROWS - N_GROUPS - N_EXPERTS
    w_route = jnp.concatenate([w_rg[l], w_re[l], jnp.zeros((D_MODEL, pad), F32)], axis=1).T
    w_route_hi = w_route.astype(BF16)
    w_route_lo = (w_route - w_route_hi.astype(F32)).astype(BF16)
    w_route = jnp.concatenate([w_route_hi, w_route_lo, w_route_hi], axis=1)
    b_route = jnp.broadcast_to(jnp.concatenate([b_rg[l], b_re[l], jnp.zeros((pad,), F32)])[:, None],
                               (ROUTE_ROWS, WIDE_TILE))
    lru_w = (0.5 * jnp.concatenate([lru_wa[l], lru_wx[l]], axis=-1)).astype(BF16)
    score_bound = (SCORE_BOUND_SLACK * LOG2E * math.sqrt(HEAD_DIM)
                   * jnp.max(jnp.abs(q_gain[l])) * jnp.max(jnp.abs(k_gain[l]))).reshape(1)
    return dict(
        score_bound=score_bound,
        g_mix=g_mix[l][None, :], w_nat=w_nat, w_tr=w_tr, b_gate=b_gate[l][None, :],
        k_gain=k_gain[l][None, :], q_gain_t=jnp.broadcast_to(q_gain[l][:, None], (HEAD_DIM, WIDE_TILE)),
        cos=cos, sin=sin, cos_t=cos_t, sin_t=sin_t,
        conv_w=conv_w[l], conv_b=conv_b[l][None, :], lru_w=lru_w,
        lru_ba=lru_ba[l].reshape(2, 1, LRU_WIDTH), lru_bx=lru_bx[l].reshape(2, 1, LRU_WIDTH),
        lru_lam=lru_lam[l].reshape(2, 1, LRU_WIDTH),
        w_out=w_out[l].astype(BF16), g_ffn=g_ffn[l][None, :], w_route=w_route, b_route=b_route,
    )


def _mixer(x, p, cnt0, tm, tk):
    batch, seq_len, _ = x.shape
    x2 = x.reshape(batch * seq_len, D_MODEL)
    u, ggr, k, gate, qt, vt = _inproj(x2, seq_len, p, WIDE_TILE, tm)
    h_fwd = _lru_scan(u, gate, ggr, None, p, 0, batch, seq_len, WIDE_TILE, reverse=False)
    m_a = _lru_scan(u, gate, ggr, h_fwd, p, 1, batch, seq_len, WIDE_TILE, reverse=True)
    m_b = _attention(p["score_bound"], qt, k, vt, gate, batch, seq_len, tm, min(tk, seq_len),
                     min(KV_BLOCK, seq_len))
    return _outproj_router(x2, m_a, m_b, cnt0, p, WIDE_TILE)


def _layer(xs_in, l, weights):
    (g_mix, w_in, b_gate, conv_w, conv_b, lru_wa, lru_ba, lru_wx, lru_bx, lru_lam,
     q_gain, k_gain, w_out, g_ffn, w_rg, b_rg, w_re, b_re, w_gate, w_up, w_down) = weights
    tm = ROW_TILE
    max_seq = max(x.shape[1] for x in xs_in)
    p = _prepare_params(l, max_seq, tm, g_mix, w_in, b_gate, conv_w, conv_b, lru_wa, lru_ba, lru_wx, lru_bx,
                        lru_lam, q_gain, k_gain, w_out, g_ffn, w_rg, b_rg, w_re, b_re)

    cnt = jnp.zeros((ROUTE_ROWS, LANES), F32)
    x1s, h2s, routes, routes_t = [], [], [], []
    for x in xs_in:
        x1, h2, route, route_t, cnt = _mixer(x, p, cnt, tm, KV_CHUNK)
        x1s.append(x1)
        h2s.append(h2)
        routes.append(route)
        routes_t.append(route_t)

    n_tok = sum(x1.shape[0] for x1 in x1s)
    n_rows = n_tok * TOP_K + N_EXPERTS * MOE_ROWS
    n_blk = n_rows // MOE_ROWS
    counts = cnt[EXPERT_ROW0:EXPERT_ROW0 + N_EXPERTS, 0].astype(jnp.int32)
    padded = (counts + MOE_ROWS - 1) // MOE_ROWS * MOE_ROWS
    pend = jnp.cumsum(padded)
    pstart = pend - padded
    blk_idx = jnp.arange(n_blk, dtype=jnp.int32)
    n_used = (pend[-1:] // MOE_ROWS).astype(jnp.int32)
    blk_e = jnp.minimum(jnp.sum((pend[None, :] <= (blk_idx * MOE_ROWS)[:, None]).astype(jnp.int32), axis=1),
                        N_EXPERTS - 1)
    blk_e = jnp.where(blk_idx < n_used, blk_e, blk_e[jnp.maximum(n_used[0] - 1, 0)])
    starts = jnp.concatenate([jnp.ones((1,), jnp.int32), (blk_e[1:] != blk_e[:-1]).astype(jnp.int32)])
    slot = (jnp.cumsum(starts) - 1) % 2
    later_start = lax.cummin(jnp.where(starts == 1, blk_idx, n_blk), reverse=True)
    next_start = jnp.concatenate([later_start[1:], jnp.full((1,), n_blk, jnp.int32)])
    nxt_e = jnp.where(next_start < n_blk, blk_e[jnp.minimum(next_start, n_blk - 1)], blk_e)

    dests = []
    xs = jnp.zeros((n_rows, PACKED_COLS), jnp.uint32)
    for h2, route_t in zip(h2s, routes_t):
        eid = route_t[0:TOP_K].astype(jnp.int32)
        rank = route_t[4:4 + TOP_K].astype(jnp.int32)
        experts = jnp.arange(N_EXPERTS, dtype=jnp.int32)[:, None, None]
        dest = (rank + jnp.sum(jnp.where(eid[None] == experts, pstart[:, None, None], 0), axis=0)).reshape(-1)
        dests.append(dest)
        xs = _dispatch(dest, h2, xs, WIDE_TILE)
    ys = _experts(blk_e, nxt_e, slot.astype(jnp.int32), n_used, xs, w_gate[l], w_up[l], w_down[l])
    outs = []
    for x, x1, route, dest in zip(xs_in, x1s, routes, dests):
        outs.append(_combine(dest, x1, route, ys, WIDE_TILE).reshape(x.shape))
    return outs


def kernel(x_prompt, x_sample, g_mix, w_in, b_gate, conv_w, conv_b, lru_wa, lru_ba, lru_wx, lru_bx, lru_lam,
           q_gain, k_gain, w_out, g_ffn, w_rg, b_rg, w_re, b_re, w_gate, w_up, w_down):
    weights = (g_mix, w_in, b_gate, conv_w, conv_b, lru_wa, lru_ba, lru_wx, lru_bx, lru_lam,
               q_gain, k_gain, w_out, g_ffn, w_rg, b_rg, w_re, b_re, w_gate, w_up, w_down)
    xs = [x_prompt, x_sample]
    for l in range(g_mix.shape[0]):
        xs = _layer(xs, l, weights)
    return tuple(xs)
```

```python
import functools
import math

import jax
import jax.numpy as jnp
import numpy as np
from jax import lax
from jax.experimental import pallas as pl
from jax.experimental.pallas import tpu as pltpu

F32 = jnp.float32
BF16 = jnp.bfloat16

D_MODEL = 1024
N_HEADS = 8
N_KV_HEADS = 2
GROUP = N_HEADS // N_KV_HEADS
HEAD_DIM = D_MODEL // N_HEADS
AXIS_DIM = HEAD_DIM // 2
N_FREQ = AXIS_DIM // 2
GRID_W = 64
ROPE_THETA = 10000.0
LRU_WIDTH = D_MODEL
LRU_BLOCKS = 8
LRU_BLOCK_W = LRU_WIDTH // LRU_BLOCKS
CONV_W = 4
CONV_PAD_LEFT = 2
LRU_C = 8.0
N_GROUPS = 4
EXPERTS_PER_GROUP = 8
N_EXPERTS = N_GROUPS * EXPERTS_PER_GROUP
TOP_K = 2
D_EXPERT = D_MODEL // 2
ATTN_KV = N_KV_HEADS * HEAD_DIM
EPS = 1e-6

LANES = 128
SUBLANES = 8
VMEM_LIMIT_BYTES = 56 * 1024 * 1024

ROW_TILE = 256
WIDE_TILE = 512
KV_CHUNK = 2048
KV_BLOCK = 16384
MOE_ROWS = 512
ROW_DMA_UNROLL = 8
ROUTE_LANES = LANES
ROUTE_ROWS = 64
PACKED_COLS = D_MODEL // 2
EXPERT_ROW0 = N_GROUPS

NAT_COLS = 2 * LRU_WIDTH + ATTN_KV + 2 * D_MODEL
TR_ROWS = D_MODEL + ATTN_KV
LOG2E = math.log2(math.e)
SAFE_EXP2_RANGE = 100.0
SCORE_BOUND_SLACK = 1.01


def _cparams(semantics):
    return pltpu.CompilerParams(dimension_semantics=semantics, vmem_limit_bytes=VMEM_LIMIT_BYTES)


def _const_spec(shape, single_buffer=False):
    nd = len(shape)
    mode = pl.Buffered(1) if single_buffer else None
    return pl.BlockSpec(shape, lambda *_: (0,) * nd, pipeline_mode=mode)


def _sigmoid(x):
    return 0.5 * jnp.tanh(0.5 * x) + 0.5


def _pack_bf16_pairs(x):
    n = x.shape[1] // 2
    lo = lax.bitcast_convert_type(x[:, :n].astype(BF16).astype(F32), jnp.uint32)
    hi = lax.bitcast_convert_type(x[:, n:].astype(BF16).astype(F32), jnp.uint32)
    return (hi & jnp.uint32(0xFFFF0000)) | (lo >> 16)


def _unpack_bf16_pairs(w):
    lo = lax.bitcast_convert_type(w << 16, F32)
    hi = lax.bitcast_convert_type(w & jnp.uint32(0xFFFF0000), F32)
    return jnp.concatenate([lo, hi], axis=1)


def _swap_halves_rows(x):
    return jnp.concatenate([x[N_FREQ:AXIS_DIM], x[0:N_FREQ], x[AXIS_DIM + N_FREQ:], x[AXIS_DIM:AXIS_DIM + N_FREQ]], axis=0)


def _inproj_kernel(x_ref, gmix_ref, wnat_ref, wtr_ref, bgate_ref, kgain_ref, qgain_ref,
                   cos_ref, sin_ref, cost_ref, sint_ref,
                   u_ref, ggr_ref, k_ref, gate_ref, qt_ref, vt_ref):
    tm = x_ref.shape[0]
    x = x_ref[...]
    h = (x * lax.rsqrt(jnp.mean(x * x, axis=-1, keepdims=True) + EPS) * gmix_ref[...]).astype(BF16)

    def nat(lo, hi):
        return jnp.dot(h, wnat_ref[:, lo:hi], preferred_element_type=F32)

    u_ref[...] = nat(0, LRU_WIDTH)
    ggr_ref[...] = jax.nn.gelu(nat(LRU_WIDTH, 2 * LRU_WIDTH))
    k0 = 2 * LRU_WIDTH
    kraw = nat(k0, k0 + ATTN_KV)
    g0 = k0 + ATTN_KV
    gate_ref[...] = _sigmoid(nat(g0, g0 + 2 * D_MODEL) + bgate_ref[...])

    cos = cos_ref[...]
    sin = sin_ref[...]
    lane = lax.broadcasted_iota(jnp.int32, (tm, HEAD_DIM), 1)
    first_half = (lane % AXIS_DIM) < N_FREQ
    for j in range(N_KV_HEADS):
        kj = kraw[:, j * HEAD_DIM:(j + 1) * HEAD_DIM]
        kn = kj * lax.rsqrt(jnp.mean(kj * kj, axis=-1, keepdims=True) + EPS) * kgain_ref[...]
        partner = jnp.where(first_half, pltpu.roll(kn, HEAD_DIM - N_FREQ, 1), pltpu.roll(kn, N_FREQ, 1))
        k_ref[:, j * HEAD_DIM:(j + 1) * HEAD_DIM] = (kn * cos + partner * sin).astype(BF16)

    zt = lax.dot_general(wtr_ref[...], h, (((1,), (1,)), ((), ())), preferred_element_type=F32)
    cost = cost_ref[...]
    sint = sint_ref[...]
    qgain = qgain_ref[...]
    qscale = (HEAD_DIM ** -0.5) * LOG2E
    for hd in range(N_HEADS):
        xq = zt[hd * HEAD_DIM:(hd + 1) * HEAD_DIM, :]
        xn = xq * lax.rsqrt(jnp.mean(xq * xq, axis=0, keepdims=True) + EPS) * qgain
        rot = (xn * cost + _swap_halves_rows(xn) * sint) * qscale
        g = hd % GROUP
        tq = qt_ref.shape[3] // GROUP
        for qb in range(tm // tq):
            qt_ref[qb, hd // GROUP, :, g * tq:(g + 1) * tq] = rot[:, qb * tq:(qb + 1) * tq].astype(BF16)
    for j in range(N_KV_HEADS):
        r0 = D_MODEL + j * HEAD_DIM
        vt_ref[j] = zt[r0:r0 + HEAD_DIM, :].astype(BF16)


def _inproj(x2, seq_len, p, tm, tq):
    t = x2.shape[0]
    n_pos = seq_len // tm
    grid = (t // tm,)
    row = lambda i: (i, 0)
    in_specs = [
        pl.BlockSpec((tm, D_MODEL), row),
        _const_spec((1, D_MODEL)),
        _const_spec((D_MODEL, NAT_COLS), single_buffer=True),
        _const_spec((TR_ROWS, D_MODEL), single_buffer=True),
        _const_spec((1, 2 * D_MODEL)),
        _const_spec((1, HEAD_DIM)),
        _const_spec((HEAD_DIM, tm)),
        pl.BlockSpec((tm, HEAD_DIM), lambda i: (i % n_pos, 0)),
        pl.BlockSpec((tm, HEAD_DIM), lambda i: (i % n_pos, 0)),
        pl.BlockSpec((HEAD_DIM, tm), lambda i: (0, i % n_pos)),
        pl.BlockSpec((HEAD_DIM, tm), lambda i: (0, i % n_pos)),
    ]
    out_shape = (
        jax.ShapeDtypeStruct((t, LRU_WIDTH), F32),
        jax.ShapeDtypeStruct((t, LRU_WIDTH), F32),
        jax.ShapeDtypeStruct((t, ATTN_KV), BF16),
        jax.ShapeDtypeStruct((t, 2 * D_MODEL), F32),
        jax.ShapeDtypeStruct((t // tq, N_KV_HEADS, HEAD_DIM, GROUP * tq), BF16),
        jax.ShapeDtypeStruct((N_KV_HEADS, HEAD_DIM, t), BF16),
    )
    out_specs = (
        pl.BlockSpec((tm, LRU_WIDTH), row),
        pl.BlockSpec((tm, LRU_WIDTH), row),
        pl.BlockSpec((tm, ATTN_KV), row),
        pl.BlockSpec((tm, 2 * D_MODEL), row),
        pl.BlockSpec((tm // tq, N_KV_HEADS, HEAD_DIM, GROUP * tq), lambda i: (i, 0, 0, 0)),
        pl.BlockSpec((N_KV_HEADS, HEAD_DIM, tm), lambda i: (0, 0, i)),
    )
    return pl.pallas_call(
        _inproj_kernel, grid=grid, in_specs=in_specs, out_specs=out_specs, out_shape=out_shape,
        compiler_params=_cparams(("arbitrary",)), name="inproj",
    )(x2, p["g_mix"], p["w_nat"], p["w_tr"], p["b_gate"], p["k_gain"], p["q_gain_t"],
      p["cos"], p["sin"], p["cos_t"], p["sin_t"])


def _lru_kernel(*refs, reverse, tm, nt):
    if reverse:
        (up_ref, uc_ref, un_ref, cw_ref, cb_ref, w_ref, ba_ref, bx_ref, lam_ref, hf_ref, ggr_ref, ga_ref,
         out_ref, ubuf, a_s, g_s, carry) = refs
    else:
        (up_ref, uc_ref, un_ref, cw_ref, cb_ref, w_ref, ba_ref, bx_ref, lam_ref,
         out_ref, ubuf, a_s, g_s, carry) = refs
    i = pl.program_id(1)
    ti = (nt - 1 - i) if reverse else i

    @pl.when(i == 0)
    def _():
        carry[...] = jnp.zeros_like(carry)

    ubuf[0:SUBLANES] = jnp.where(ti == 0, 0.0, up_ref[...])
    ubuf[SUBLANES:SUBLANES + tm] = uc_ref[...]
    ubuf[SUBLANES + tm:2 * SUBLANES + tm] = jnp.where(ti == nt - 1, 0.0, un_ref[...])
    cw = cw_ref[...]
    ub = ubuf[...]
    n_buf = tm + 2 * SUBLANES
    xc = cb_ref[...]
    for j in range(CONV_W):
        back = CONV_PAD_LEFT - j
        tap = ub if back == 0 else pltpu.roll(ub, back % n_buf, 0)
        xc = xc + cw[j:j + 1] * tap[SUBLANES:SUBLANES + tm]

    xcb = xc.astype(BF16)
    lam = lam_ref[...]
    half_c = (0.5 * LRU_C) * (jnp.minimum(lam, 0.0) - jnp.log1p(jnp.exp(-jnp.abs(lam))))
    half_ba = 0.5 * ba_ref[...]
    half_bx = 0.5 * bx_ref[...]
    half_xc = 0.5 * xc
    for hb in range(LRU_BLOCKS):
        sl = slice(hb * LRU_BLOCK_W, (hb + 1) * LRU_BLOCK_W)
        gz = jnp.dot(xcb[:, sl], w_ref[hb], preferred_element_type=F32)
        tr = jnp.tanh(gz[:, :LRU_BLOCK_W] + half_ba[:, sl])
        ti = jnp.tanh(gz[:, LRU_BLOCK_W:] + half_bx[:, sl])
        log_a = tr * half_c[:, sl] + half_c[:, sl]
        a = jnp.exp(log_a)
        a_s[:, sl] = a
        y = jnp.tanh(log_a) * (-1.0 - a * a)
        root = jnp.where(y > 0.0, y * lax.rsqrt(y), 0.0)
        g_s[:, sl] = root * (ti * half_xc[:, sl] + half_xc[:, sl])

    n_chunk = tm // SUBLANES
    srow = lax.broadcasted_iota(jnp.int32, (SUBLANES, LRU_WIDTH), 0)

    def chunk(c, h_prev):
        ci = (n_chunk - 1 - c) if reverse else c
        off = pl.multiple_of(ci * SUBLANES, SUBLANES)
        a = a_s[pl.ds(off, SUBLANES), :]
        b = g_s[pl.ds(off, SUBLANES), :]
        edge = (SUBLANES - 1) if reverse else 0
        b = b + jnp.where(srow == edge, a * h_prev, 0.0)
        for d in (1, 2, 4):
            shift = (SUBLANES - d) if reverse else d
            valid = (srow < SUBLANES - d) if reverse else (srow >= d)
            b = a * jnp.where(valid, pltpu.roll(b, shift, 0), 0.0) + b
            if d != 4:
                a = a * pltpu.roll(a, shift, 0)
        h = b
        if reverse:
            rows = pl.ds(off, SUBLANES)
            out_ref[rows, :] = ga_ref[rows, :] * (ggr_ref[rows, :] * (hf_ref[rows, :] + h))
            return h[0:1]
        out_ref[pl.ds(off, SUBLANES), :] = h
        return h[SUBLANES - 1:SUBLANES]

    carry[...] = lax.fori_loop(0, n_chunk, chunk, carry[...])


def _lru_scan(u, gate, ggr, h_fwd, p, d, batch, seq_len, tm, reverse):
    t = u.shape[0]
    nt = seq_len // tm
    per8 = tm // SUBLANES
    n8 = t // SUBLANES

    def tile(b, i):
        return b * nt + ((nt - 1 - i) if reverse else i)

    cur = lambda b, i: (tile(b, i), 0)
    prev = lambda b, i: (jnp.maximum(tile(b, i) * per8 - 1, 0), 0)
    nxt = lambda b, i: (jnp.minimum((tile(b, i) + 1) * per8, n8 - 1), 0)
    const2 = lambda b, i: (0, 0)
    in_specs = [
        pl.BlockSpec((SUBLANES, LRU_WIDTH), prev),
        pl.BlockSpec((tm, LRU_WIDTH), cur),
        pl.BlockSpec((SUBLANES, LRU_WIDTH), nxt),
        pl.BlockSpec((CONV_W, LRU_WIDTH), const2),
        pl.BlockSpec((1, LRU_WIDTH), const2),
        pl.BlockSpec((LRU_BLOCKS, LRU_BLOCK_W, 2 * LRU_BLOCK_W), lambda b, i: (0, 0, 0)),
        pl.BlockSpec((1, LRU_WIDTH), const2),
        pl.BlockSpec((1, LRU_WIDTH), const2),
        pl.BlockSpec((1, LRU_WIDTH), const2),
    ]
    args = [u, u, u, p["conv_w"], p["conv_b"], p["lru_w"][d], p["lru_ba"][d], p["lru_bx"][d], p["lru_lam"][d]]
    if reverse:
        in_specs += [pl.BlockSpec((tm, LRU_WIDTH), cur)] * 3
        args += [h_fwd, ggr, gate]
    return pl.pallas_call(
        functools.partial(_lru_kernel, reverse=reverse, tm=tm, nt=nt),
        grid=(batch, nt), in_specs=in_specs,
        out_specs=pl.BlockSpec((tm, LRU_WIDTH), cur),
        out_shape=jax.ShapeDtypeStruct((t, LRU_WIDTH), F32),
        scratch_shapes=[pltpu.VMEM((tm + 2 * SUBLANES, LRU_WIDTH), F32), pltpu.VMEM((tm, LRU_WIDTH), F32),
                        pltpu.VMEM((tm, LRU_WIDTH), F32), pltpu.VMEM((1, LRU_WIDTH), F32)],
        compiler_params=_cparams(("arbitrary", "arbitrary")), name="lru_bwd" if reverse else "lru_fwd",
    )(*args)


def _attn_kernel(bound_ref, qt_ref, k_ref, vt_ref, gb_ref, o_ref, m_s, l_s, acc_s, *, tq, tk, n_chunks, n_kv_blocks):
    nq_cols = GROUP * tq
    kv_blk = pl.program_id(3)
    bound = bound_ref[0]
    bounded = 2.0 * bound <= SAFE_EXP2_RANGE

    @pl.when(kv_blk == 0)
    def _():
        acc_s[...] = jnp.zeros_like(acc_s)
        l_s[...] = jnp.zeros_like(l_s)
        m_s[...] = jnp.full_like(m_s, -jnp.inf)

    @pl.when(bounded)
    def _():
        qt = qt_ref[0, 0]
        for c in range(n_chunks):
            rows = slice(c * tk, (c + 1) * tk)
            pr = jnp.exp2(jnp.dot(k_ref[rows, :], qt, preferred_element_type=F32) - bound)
            l_s[...] += jnp.sum(pr.reshape(tk // SUBLANES, SUBLANES, nq_cols), axis=0)
            acc_s[...] += jnp.dot(vt_ref[0, :, rows], pr.astype(BF16), preferred_element_type=F32)

    @pl.when(jnp.logical_not(bounded))
    def _():
        qt = qt_ref[0, 0]

        def chunk(c, _):
            off = pl.multiple_of(c * tk, tk)
            s = jnp.dot(k_ref[pl.ds(off, tk), :], qt, preferred_element_type=F32)
            m_old = m_s[...]
            m_new = jnp.maximum(m_old, jnp.max(s, axis=0, keepdims=True))
            alpha = jnp.exp2(m_old - m_new)
            pr = jnp.exp2(s - m_new)
            l_s[...] = alpha * l_s[...] + jnp.sum(pr.reshape(tk // SUBLANES, SUBLANES, nq_cols), axis=0)
            acc_s[...] = alpha * acc_s[...] + jnp.dot(vt_ref[0, :, pl.ds(off, tk)], pr.astype(BF16),
                                                      preferred_element_type=F32)
            m_s[...] = m_new
            return 0

        lax.fori_loop(0, n_chunks, chunk, 0)

    @pl.when(kv_blk == n_kv_blocks - 1)
    def _():
        out_t = acc_s[...] / jnp.sum(l_s[...], axis=0, keepdims=True)
        for g in range(GROUP):
            cols = slice(g * HEAD_DIM, (g + 1) * HEAD_DIM)
            o_ref[:, cols] = gb_ref[:, cols] * out_t[:, g * tq:(g + 1) * tq].T


def _attention(bound, qt, k, vt, gate, batch, seq_len, tq, tk, kv_block):
    t = k.shape[0]
    nq = seq_len // tq
    nkv = seq_len // kv_block
    half = GROUP * HEAD_DIM
    gate_col0 = D_MODEL // half
    return pl.pallas_call(
        functools.partial(_attn_kernel, tq=tq, tk=tk, n_chunks=kv_block // tk, n_kv_blocks=nkv),
        grid=(batch, N_KV_HEADS, nq, nkv),
        in_specs=[
            pl.BlockSpec(memory_space=pltpu.SMEM),
            pl.BlockSpec((1, 1, HEAD_DIM, GROUP * tq), lambda b, j, i, c: (b * nq + i, j, 0, 0)),
            pl.BlockSpec((kv_block, HEAD_DIM), lambda b, j, i, c: (b * nkv + c, j)),
            pl.BlockSpec((1, HEAD_DIM, kv_block), lambda b, j, i, c: (j, 0, b * nkv + c)),
            pl.BlockSpec((tq, half), lambda b, j, i, c: (b * nq + i, gate_col0 + j)),
        ],
        out_specs=pl.BlockSpec((tq, half), lambda b, j, i, c: (b * nq + i, j)),
        out_shape=jax.ShapeDtypeStruct((t, D_MODEL), F32),
        scratch_shapes=[pltpu.VMEM((1, GROUP * tq), F32), pltpu.VMEM((SUBLANES, GROUP * tq), F32),
                        pltpu.VMEM((HEAD_DIM, GROUP * tq), F32)],
        compiler_params=_cparams(("arbitrary", "arbitrary", "arbitrary", "arbitrary")), name="attention",
    )(bound, qt, k, vt, gate)


def _outproj_router_kernel(x_ref, ma_ref, mb_ref, wout_ref, gffn_ref, wr_ref, br_ref, cnt0_ref,
                           x1_ref, h2_ref, route_ref, route_t_ref, cnt_ref, cnt_s):
    tm = x_ref.shape[0]
    i = pl.program_id(0)

    @pl.when(i == 0)
    def _():
        cnt_s[...] = cnt0_ref[...]

    merged = (ma_ref[...] + mb_ref[...]).astype(BF16)
    x1 = x_ref[...] + jnp.dot(merged, wout_ref[...], preferred_element_type=F32)
    x1_ref[...] = x1
    h2 = x1 * lax.rsqrt(jnp.mean(x1 * x1, axis=-1, keepdims=True) + EPS) * gffn_ref[...]
    h2_ref[...] = _pack_bf16_pairs(h2)

    hi = h2.astype(BF16)
    lo = (h2 - hi.astype(F32)).astype(BF16)
    zt = lax.dot_general(wr_ref[...], jnp.concatenate([hi, hi, lo], axis=1), (((1,), (1,)), ((), ())),
                         preferred_element_type=F32) + br_ref[...]
    sub = lax.broadcasted_iota(jnp.int32, (ROUTE_ROWS, tm), 0)
    neg = -jnp.inf

    def first_argmax(v):
        m = jnp.max(v, axis=0, keepdims=True)
        return m, jnp.min(jnp.where(v == m, sub, ROUTE_ROWS), axis=0, keepdims=True)

    zg = jnp.where(sub < N_GROUPS, zt, neg)
    mg, grp = first_argmax(zg)
    p_sel = 1.0 / jnp.sum(jnp.exp(zg - mg), axis=0, keepdims=True)
    e_lo = EXPERT_ROW0 + EXPERTS_PER_GROUP * grp
    ze = jnp.where((sub >= e_lo) & (sub < e_lo + EXPERTS_PER_GROUP), zt, neg)
    m1, i1 = first_argmax(ze)
    m2, i2 = first_argmax(jnp.where(sub == i1, neg, ze))
    e2 = jnp.exp(m2 - m1)
    w0 = p_sel / (1.0 + e2)
    w1 = p_sel * e2 / (1.0 + e2)

    hot0 = sub == i1
    hot1 = sub == i2
    hot = jnp.where(hot0 | hot1, 1.0, 0.0)
    rr = lax.broadcasted_iota(jnp.int32, (tm, tm), 0)
    cc = lax.broadcasted_iota(jnp.int32, (tm, tm), 1)
    earlier = jnp.where(rr < cc, 1.0, 0.0).astype(BF16)
    cnt = cnt_s[...]
    before = (jnp.dot(hot.astype(BF16), earlier, preferred_element_type=F32)
              + jnp.concatenate([cnt] * (tm // LANES), axis=1))
    rank0 = jnp.sum(jnp.where(hot0, before, 0.0), axis=0, keepdims=True)
    rank1 = jnp.sum(jnp.where(hot1, before, 0.0), axis=0, keepdims=True)
    cnt_s[...] = cnt + jnp.sum(hot, axis=1, keepdims=True)
    cnt_ref[...] = cnt_s[...]

    eid0 = (i1 - EXPERT_ROW0).astype(F32)
    eid1 = (i2 - EXPERT_ROW0).astype(F32)
    zero = jnp.zeros_like(w0)
    rec_t = jnp.concatenate([eid0, eid1, w0, w1, rank0, rank1, zero, zero], axis=0)
    route_t_ref[...] = rec_t
    route_ref[...] = jnp.concatenate([rec_t, jnp.zeros((ROUTE_LANES - SUBLANES, tm), F32)], axis=0).T


def _outproj_router(x2, m_a, m_b, cnt0, p, tm):
    t = x2.shape[0]
    row = lambda i: (i, 0)
    return pl.pallas_call(
        _outproj_router_kernel, grid=(t // tm,),
        in_specs=[pl.BlockSpec((tm, D_MODEL), row), pl.BlockSpec((tm, D_MODEL), row), pl.BlockSpec((tm, D_MODEL), row),
                  _const_spec((D_MODEL, D_MODEL)), _const_spec((1, D_MODEL)), _const_spec((ROUTE_ROWS, 3 * D_MODEL)),
                  _const_spec((ROUTE_ROWS, tm)), _const_spec((ROUTE_ROWS, LANES))],
        out_specs=(pl.BlockSpec((tm, D_MODEL), row), pl.BlockSpec((tm, PACKED_COLS), row),
                   pl.BlockSpec((tm, ROUTE_LANES), row), pl.BlockSpec((SUBLANES, tm), lambda i: (0, i)),
                   _const_spec((ROUTE_ROWS, LANES))),
        out_shape=(jax.ShapeDtypeStruct((t, D_MODEL), F32), jax.ShapeDtypeStruct((t, PACKED_COLS), jnp.uint32),
                   jax.ShapeDtypeStruct((t, ROUTE_LANES), F32), jax.ShapeDtypeStruct((SUBLANES, t), F32),
                   jax.ShapeDtypeStruct((ROUTE_ROWS, LANES), F32)),
        scratch_shapes=[pltpu.VMEM((ROUTE_ROWS, LANES), F32)],
        compiler_params=_cparams(("arbitrary",)), name="outproj_router",
    )(x2, m_a, m_b, p["w_out"], p["g_ffn"], p["w_route"], p["b_route"], cnt0)


def _dispatch_kernel(dest_ref, h_ref, xs_in_ref, xs_ref, sem, *, tm, n_tok):
    del xs_in_ref
    base = pl.program_id(0) * tm

    def issue(r, _):
        for k in range(TOP_K):
            d = dest_ref[k * n_tok + base + r]
            pltpu.make_async_copy(h_ref.at[pl.ds(r, 1), :], xs_ref.at[pl.ds(d, 1), :], sem).start(priority=k)
        return 0

    lax.fori_loop(0, tm, issue, 0, unroll=ROW_DMA_UNROLL)
    for k in range(TOP_K):
        pltpu.make_async_copy(h_ref, xs_ref.at[pl.ds(0, tm), :], sem).wait()


def _dispatch(dest_flat, h2, xs, tm):
    t = h2.shape[0]
    return pl.pallas_call(
        functools.partial(_dispatch_kernel, tm=tm, n_tok=t),
        grid_spec=pltpu.PrefetchScalarGridSpec(
            num_scalar_prefetch=1, grid=(t // tm,),
            in_specs=[pl.BlockSpec((tm, PACKED_COLS), lambda i, d: (i, 0)), pl.BlockSpec(memory_space=pl.ANY)],
            out_specs=pl.BlockSpec(memory_space=pl.ANY),
            scratch_shapes=[pltpu.SemaphoreType.DMA]),
        out_shape=jax.ShapeDtypeStruct(xs.shape, xs.dtype),
        input_output_aliases={2: 0},
        compiler_params=_cparams(("arbitrary",)), name="moe_dispatch",
    )(dest_flat, h2, xs)


def _experts_kernel(blk_e_ref, nxt_e_ref, slot_ref, n_used_ref, x_ref, wg_hbm, wu_hbm, wd_hbm, y_ref,
                    wg_f, wu_f, wd_f, wg_s, wu_s, wd_s, sems):
    i = pl.program_id(0)
    e = blk_e_ref[i]
    slot = slot_ref[i]
    first = jnp.logical_or(i == 0, e != blk_e_ref[jnp.maximum(i - 1, 0)])

    def weight_copies(expert, s):
        return (pltpu.make_async_copy(wg_hbm.at[expert], wg_f.at[s], sems.at[s, 0]),
                pltpu.make_async_copy(wu_hbm.at[expert], wu_f.at[s], sems.at[s, 1]),
                pltpu.make_async_copy(wd_hbm.at[expert], wd_f.at[s], sems.at[s, 2]))

    @pl.when(i == 0)
    def _():
        for cp in weight_copies(e, slot):
            cp.start()

    @pl.when(first)
    def _():
        for cp in weight_copies(e, slot):
            cp.wait()
        wg_s[...] = wg_f[slot].astype(BF16)
        wu_s[...] = wu_f[slot].astype(BF16)
        wd_s[...] = wd_f[slot].astype(BF16)
        nxt = nxt_e_ref[i]

        @pl.when(nxt != e)
        def _():
            for cp in weight_copies(nxt, 1 - slot):
                cp.start()

    @pl.when(i < n_used_ref[0])
    def _():
        xb = _unpack_bf16_pairs(x_ref[...]).astype(BF16)
        gate = jnp.dot(xb, wg_s[...], preferred_element_type=F32)
        up = jnp.dot(xb, wu_s[...], preferred_element_type=F32)
        act = (jax.nn.silu(gate) * up).astype(BF16)
        y_ref[...] = _pack_bf16_pairs(jnp.dot(act, wd_s[...], preferred_element_type=F32))

    @pl.when(i >= n_used_ref[0])
    def _():
        y_ref[...] = jnp.zeros_like(y_ref)


def _experts(blk_e, nxt_e, slot, n_used, xs, w_gate, w_up, w_down):
    n_rows = xs.shape[0]
    rb = MOE_ROWS
    row_blk = lambda i, *_: (i, 0)
    return pl.pallas_call(
        _experts_kernel,
        grid_spec=pltpu.PrefetchScalarGridSpec(
            num_scalar_prefetch=4, grid=(n_rows // rb,),
            in_specs=[pl.BlockSpec((rb, PACKED_COLS), row_blk),
                      pl.BlockSpec(memory_space=pl.ANY), pl.BlockSpec(memory_space=pl.ANY),
                      pl.BlockSpec(memory_space=pl.ANY)],
            out_specs=pl.BlockSpec((rb, PACKED_COLS), row_blk),
            scratch_shapes=[pltpu.VMEM((2, D_MODEL, D_EXPERT), F32), pltpu.VMEM((2, D_MODEL, D_EXPERT), F32),
                            pltpu.VMEM((2, D_EXPERT, D_MODEL), F32),
                            pltpu.VMEM((D_MODEL, D_EXPERT), BF16), pltpu.VMEM((D_MODEL, D_EXPERT), BF16),
                            pltpu.VMEM((D_EXPERT, D_MODEL), BF16), pltpu.SemaphoreType.DMA((2, 3))]),
        out_shape=jax.ShapeDtypeStruct((n_rows, PACKED_COLS), jnp.uint32),
        compiler_params=_cparams(("arbitrary",)), name="moe_experts",
    )(blk_e, nxt_e, slot, n_used, xs, w_gate, w_up, w_down)


def _combine_kernel(dest_ref, x1_ref, route_ref, ys_ref, o_ref, gbuf, sems, *, tm, n_tok):
    step = pl.program_id(0)
    n_steps = pl.num_programs(0)

    def gather_rows(for_step, slot):
        base = for_step * tm

        def issue(r, _):
            for k in range(TOP_K):
                d = dest_ref[k * n_tok + base + r]
                pltpu.make_async_copy(ys_ref.at[pl.ds(d, 1), :], gbuf.at[slot, k, pl.ds(r, 1), :],
                                      sems.at[slot]).start(priority=k)
            return 0

        lax.fori_loop(0, tm, issue, 0, unroll=ROW_DMA_UNROLL)

    @pl.when(step == 0)
    def _():
        gather_rows(0, 0)

    @pl.when(step + 1 < n_steps)
    def _():
        gather_rows(step + 1, (step + 1) % 2)

    slot = step % 2
    for k in range(TOP_K):
        pltpu.make_async_copy(ys_ref.at[pl.ds(0, tm), :], gbuf.at[slot, k], sems.at[slot]).wait()
    w0 = route_ref[:, 2:3]
    w1 = route_ref[:, 3:4]
    o_ref[...] = x1_ref[...] + (_unpack_bf16_pairs(gbuf[slot, 0]) * w0 + _unpack_bf16_pairs(gbuf[slot, 1]) * w1)


def _combine(dest_flat, x1, route, ys, tm):
    t = x1.shape[0]
    return pl.pallas_call(
        functools.partial(_combine_kernel, tm=tm, n_tok=t),
        grid_spec=pltpu.PrefetchScalarGridSpec(
            num_scalar_prefetch=1, grid=(t // tm,),
            in_specs=[pl.BlockSpec((tm, D_MODEL), lambda i, d: (i, 0)),
                      pl.BlockSpec((tm, ROUTE_LANES), lambda i, d: (i, 0)),
                      pl.BlockSpec(memory_space=pl.ANY)],
            out_specs=pl.BlockSpec((tm, D_MODEL), lambda i, d: (i, 0)),
            scratch_shapes=[pltpu.VMEM((2, TOP_K, tm, PACKED_COLS), jnp.uint32), pltpu.SemaphoreType.DMA((2,))]),
        out_shape=jax.ShapeDtypeStruct((t, D_MODEL), F32),
        compiler_params=_cparams(("arbitrary",)), name="moe_combine",
    )(dest_flat, x1, route, ys)


def _rope_tables(seq_len):
    inv = (ROPE_THETA ** (-np.arange(0, AXIS_DIM, 2, dtype=np.float32) / AXIS_DIM)).astype(np.float32)
    pos = np.arange(seq_len)
    ang_r = (pos // GRID_W).astype(np.float32)[:, None] * inv
    ang_c = (pos % GRID_W).astype(np.float32)[:, None] * inv
    cos = np.concatenate([np.cos(ang_r), np.cos(ang_r), np.cos(ang_c), np.cos(ang_c)], axis=1)
    sin = np.concatenate([-np.sin(ang_r), np.sin(ang_r), -np.sin(ang_c), np.sin(ang_c)], axis=1)
    return cos, sin, np.ascontiguousarray(cos.T), np.ascontiguousarray(sin.T)


def _prepare_params(l, max_seq, tm, g_mix, w_in, b_gate, conv_w, conv_b, lru_wa, lru_ba, lru_wx, lru_bx, lru_lam,
                    q_gain, k_gain, w_out, g_ffn, w_rg, b_rg, w_re, b_re):
    c_u, c_gr, c_q, c_k, c_v = (LRU_WIDTH, 2 * LRU_WIDTH, 2 * LRU_WIDTH + D_MODEL,
                                2 * LRU_WIDTH + D_MODEL + ATTN_KV, 2 * LRU_WIDTH + D_MODEL + 2 * ATTN_KV)
    w = w_in[l]
    w_nat = jnp.concatenate([w[:, :c_gr], w[:, c_q:c_k], w[:, c_v:]], axis=1).astype(BF16)
    w_tr = jnp.concatenate([w[:, c_gr:c_q], w[:, c_k:c_v]], axis=1).T.astype(BF16)
    cos, sin, cos_t, sin_t = _rope_tables(max_seq)
    pad = ROUTE_ROWS - N_GROUPS - N_EXPERTS
    w_route = jnp.concatenate([w_rg[l], w_re[l], jnp.zeros((D_MODEL, pad), F32)], axis=1).T
    w_route_hi = w_route.astype(BF16)
    w_route_lo = (w_route - w_route_hi.astype(F32)).astype(BF16)
    w_route = jnp.concatenate([w_route_hi, w_route_lo, w_route_hi], axis=1)
    b_route = jnp.broadcast_to(jnp.concatenate([b_rg[l], b_re[l], jnp.zeros((pad,), F32)])[:, None],
                               (ROUTE_ROWS, WIDE_TILE))
    lru_w = (0.5 * jnp.concatenate([lru_wa[l], lru_wx[l]], axis=-1)).astype(BF16)
    score_bound = (SCORE_BOUND_SLACK * LOG2E * math.sqrt(HEAD_DIM)
                   * jnp.max(jnp.abs(q_gain[l])) * jnp.max(jnp.abs(k_gain[l]))).reshape(1)
    return dict(
        score_bound=score_bound,
        g_mix=g_mix[l][None, :], w_nat=w_nat, w_tr=w_tr, b_gate=b_gate[l][None, :],
        k_gain=k_gain[l][None, :], q_gain_t=jnp.broadcast_to(q_gain[l][:, None], (HEAD_DIM, WIDE_TILE)),
        cos=cos, sin=sin, cos_t=cos_t, sin_t=sin_t,
        conv_w=conv_w[l], conv_b=conv_b[l][None, :], lru_w=lru_w,
        lru_ba=lru_ba[l].reshape(2, 1, LRU_WIDTH), lru_bx=lru_bx[l].reshape(2, 1, LRU_WIDTH),
        lru_lam=lru_lam[l].reshape(2, 1, LRU_WIDTH),
        w_out=w_out[l].astype(BF16), g_ffn=g_ffn[l][None, :], w_route=w_route, b_route=b_route,
    )


def _mixer(x, p, cnt0, tm, tk):
    batch, seq_len, _ = x.shape
    x2 = x.reshape(batch * seq_len, D_MODEL)
    u, ggr, k, gate, qt, vt = _inproj(x2, seq_len, p, WIDE_TILE, tm)
    h_fwd = _lru_scan(u, gate, ggr, None, p, 0, batch, seq_len, WIDE_TILE, reverse=False)
    m_a = _lru_scan(u, gate, ggr, h_fwd, p, 1, batch, seq_len, WIDE_TILE, reverse=True)
    m_b = _attention(p["score_bound"], qt, k, vt, gate, batch, seq_len, tm, min(tk, seq_len),
                     min(KV_BLOCK, seq_len))
    return _outproj_router(x2, m_a, m_b, cnt0, p, WIDE_TILE)


def _layer(xs_in, l, weights):
    (g_mix, w_in, b_gate, conv_w, conv_b, lru_wa, lru_ba, lru_wx, lru_bx, lru_lam,
     q_gain, k_gain, w_out, g_ffn, w_rg, b_rg, w_re, b_re, w_gate, w_up, w_down) = weights
    tm = ROW_TILE
    max_seq = max(x.shape[1] for x in xs_in)
    p = _prepare_params(l, max_seq, tm, g_mix, w_in, b_gate, conv_w, conv_b, lru_wa, lru_ba, lru_wx, lru_bx,
                        lru_lam, q_gain, k_gain, w_out, g_ffn, w_rg, b_rg, w_re, b_re)

    cnt = jnp.zeros((ROUTE_ROWS, LANES), F32)
    x1s, h2s, routes, routes_t = [], [], [], []
    for x in xs_in:
        x1, h2, route, route_t, cnt = _mixer(x, p, cnt, tm, KV_CHUNK)
        x1s.append(x1)
        h2s.append(h2)
        routes.append(route)
        routes_t.append(route_t)

    n_tok = sum(x1.shape[0] for x1 in x1s)
    n_rows = n_tok * TOP_K + N_EXPERTS * MOE_ROWS
    n_blk = n_rows // MOE_ROWS
    counts = cnt[EXPERT_ROW0:EXPERT_ROW0 + N_EXPERTS, 0].astype(jnp.int32)
    padded = (counts + MOE_ROWS - 1) // MOE_ROWS * MOE_ROWS
    pend = jnp.cumsum(padded)
    pstart = pend - padded
    blk_idx = jnp.arange(n_blk, dtype=jnp.int32)
    n_used = (pend[-1:] // MOE_ROWS).astype(jnp.int32)
    blk_e = jnp.minimum(jnp.sum((pend[None, :] <= (blk_idx * MOE_ROWS)[:, None]).astype(jnp.int32), axis=1),
                        N_EXPERTS - 1)
    blk_e = jnp.where(blk_idx < n_used, blk_e, blk_e[jnp.maximum(n_used[0] - 1, 0)])
    starts = jnp.concatenate([jnp.ones((1,), jnp.int32), (blk_e[1:] != blk_e[:-1]).astype(jnp.int32)])
    slot = (jnp.cumsum(starts) - 1) % 2
    later_start = lax.cummin(jnp.where(starts == 1, blk_idx, n_blk), reverse=True)
    next_start = jnp.concatenate([later_start[1:], jnp.full((1,), n_blk, jnp.int32)])
    nxt_e = jnp.where(next_start < n_blk, blk_e[jnp.minimum(next_start, n_blk - 1)], blk_e)

    dests = []
    xs = jnp.zeros((n_rows, PACKED_COLS), jnp.uint32)
    for h2, route_t in zip(h2s, routes_t):
        eid = route_t[0:TOP_K].astype(jnp.int32)
        rank = route_t[4:4 + TOP_K].astype(jnp.int32)
        experts = jnp.arange(N_EXPERTS, dtype=jnp.int32)[:, None, None]
        dest = (rank + jnp.sum(jnp.where(eid[None] == experts, pstart[:, None, None], 0), axis=0)).reshape(-1)
        dests.append(dest)
        xs = _dispatch(dest, h2, xs, WIDE_TILE)
    ys = _experts(blk_e, nxt_e, slot.astype(jnp.int32), n_used, xs, w_gate[l], w_up[l], w_down[l])
    outs = []
    for x, x1, route, dest in zip(xs_in, x1s, routes, dests):
        outs.append(_combine(dest, x1, route, ys, WIDE_TILE).reshape(x.shape))
    return outs


def kernel(x_prompt, x_sample, g_mix, w_in, b_gate, conv_w, conv_b, lru_wa, lru_ba, lru_wx, lru_bx, lru_lam,
           q_gain, k_gain, w_out, g_ffn, w_rg, b_rg, w_re, b_re, w_gate, w_up, w_down):
    weights = (g_mix, w_in, b_gate, conv_w, conv_b, lru_wa, lru_ba, lru_wx, lru_bx, lru_lam,
               q_gain, k_gain, w_out, g_ffn, w_rg, b_rg, w_re, b_re, w_gate, w_up, w_down)
    xs = [x_prompt, x_sample]
    for l in range(g_mix.shape[0]):
        xs = _layer(xs, l, weights)
    return tuple(xs)
```
